```python
import math
import jax, jax.numpy as jnp
from jax import lax
import numpy as np


D_MODEL = 1024
BATCH = 2
SEQ = 8192
DEPTH = 1
DEC_BATCH = 32
DEC_SEQ = 4
PAST_LEN = 8192
PAGE_SIZE = 128

N_HEADS_A = 4
HEAD_DIM_A = (D_MODEL // 2) // (2 * N_HEADS_A)
N_HEADS_B = 4
KEY_DIM_B = (D_MODEL // 2) // N_HEADS_B
VAL_DIM_B = KEY_DIM_B
D_FF = 2816
CONV_W = 3

ROPE_THETA = 10000.0
Q_BLOCK = 128
GLA_CHUNK = 64
LN_EPS = 1e-5
RMS_EPS = 1e-5
ALPHA = (2 * DEPTH) ** 0.25
BETA = (8 * DEPTH) ** -0.25

QA_DIM = N_HEADS_A * 2 * HEAD_DIM_A
KA_DIM = N_HEADS_A * 2 * HEAD_DIM_A
VA_DIM = N_HEADS_A * 2 * HEAD_DIM_A
QB_DIM = N_HEADS_B * KEY_DIM_B
FB_DIM = N_HEADS_B * KEY_DIM_B
IB_DIM = N_HEADS_B * VAL_DIM_B
GB_DIM = N_HEADS_B * VAL_DIM_B
IN_SIZES = (QA_DIM, KA_DIM, VA_DIM, QB_DIM, FB_DIM, IB_DIM, GB_DIM, D_MODEL, D_MODEL)
IN_COLS = sum(IN_SIZES)

kernel_name = 'hybrid_diffattn_hgrn2_convffn_step'


def layer_norm(x, g, b):
    xf = x.astype(jnp.float32)
    mu = jnp.mean(xf, -1, keepdims=True)
    var = jnp.mean(jnp.square(xf - mu), -1, keepdims=True)
    return ((xf - mu) * lax.rsqrt(var + LN_EPS) * g.astype(jnp.float32) + b.astype(jnp.float32)).astype(x.dtype)


def rms_norm(x, g):
    xf = x.astype(jnp.float32)
    return (xf * lax.rsqrt(jnp.mean(xf * xf, -1, keepdims=True) + RMS_EPS) * g.astype(jnp.float32)).astype(x.dtype)


def rope(x, pos):
    half = HEAD_DIM_A // 2
    inv = ROPE_THETA ** (-jnp.arange(half, dtype=jnp.float32) * 2.0 / HEAD_DIM_A)
    ang = pos.astype(jnp.float32)[:, None] * inv[None, :]
    cos = jnp.cos(ang)[:, None, None, :].astype(x.dtype)
    sin = jnp.sin(ang)[:, None, None, :].astype(x.dtype)
    x1, x2 = x[..., :half], x[..., half:]
    return jnp.concatenate([x1 * cos - x2 * sin, x2 * cos + x1 * sin], -1)


def mixer_inputs(x, pos, w_in_l, lb_l):
    B, T, _ = x.shape
    proj = x @ w_in_l
    offs = np.cumsum(IN_SIZES)[:-1].tolist()
    qa, ka, va, qb, fb, ib, gb, ga, gg = jnp.split(proj, offs, axis=-1)
    qa = rope(qa.reshape(B, T, N_HEADS_A, 2, HEAD_DIM_A), pos)
    ka = rope(ka.reshape(B, T, N_HEADS_A, 2, HEAD_DIM_A), pos)
    va = va.reshape(B, T, N_HEADS_A, 2 * HEAD_DIM_A)
    lb = lb_l.astype(jnp.float32)
    fr = fb.astype(jnp.float32)
    logf = jnp.logaddexp(jnp.log(lb), jnp.log1p(-lb) + jax.nn.log_sigmoid(fr))
    kb = (1.0 - lb) * jax.nn.sigmoid(-fr)
    qb = qb.reshape(B, T, N_HEADS_B, KEY_DIM_B)
    kb = kb.reshape(B, T, N_HEADS_B, KEY_DIM_B)
    logf = logf.reshape(B, T, N_HEADS_B, KEY_DIM_B)
    ib = ib.reshape(B, T, N_HEADS_B, VAL_DIM_B)
    gb = gb.reshape(B, T, N_HEADS_B, VAL_DIM_B)
    return qa, ka, va, qb, kb, logf, ib, gb, ga, gg


def diff_attention(q, ks, vs, masks, lam):
    scale = HEAD_DIM_A ** -0.5
    s = jnp.concatenate([jnp.einsum('bqhmd,bkhmd->bhmqk', q, k, preferred_element_type=jnp.float32) * scale
                         for k in ks], -1)
    mask = jnp.concatenate(masks, -1)
    p = jax.nn.softmax(jnp.where(mask, s, -jnp.inf), axis=-1)
    a = (p[:, :, 0] - lam * p[:, :, 1]).astype(vs[0].dtype)
    offs = np.cumsum([0] + [k.shape[1] for k in ks]).tolist()
    return sum(jnp.einsum('bhqk,bkhe->bqhe', a[..., offs[i]:offs[i + 1]], vs[i]) for i in range(len(ks)))


def prompt_diff_attention(q, k, v, lam):
    B, T = q.shape[0], q.shape[1]
    nb = T // Q_BLOCK
    qb = q.reshape(B, nb, Q_BLOCK, N_HEADS_A, 2, HEAD_DIM_A).swapaxes(0, 1)
    starts = jnp.arange(nb) * Q_BLOCK
    kpos = jnp.arange(T)

    def one_block(args):
        qblk, st = args
        mask = kpos[None, :] <= (st + jnp.arange(Q_BLOCK))[:, None]
        return diff_attention(qblk, (k,), (v,), (mask,), lam)

    o = lax.map(one_block, (qb, starts))
    return o.swapaxes(0, 1).reshape(B, T, N_HEADS_A, 2 * HEAD_DIM_A)


def hgrn2_scan(q, k, v, logf, s0):
    B, T, H, _ = q.shape
    C = math.gcd(T, GLA_CHUNK)
    n = T // C

    def to_chunks(a):
        return a.astype(jnp.float32).reshape(B, n, C, H, a.shape[-1]).transpose(1, 0, 3, 2, 4)

    causal = jnp.tril(jnp.ones((C, C), bool))

    def step(S, xs):
        qc, kc, vc, lc = xs
        b = jnp.cumsum(lc, axis=2)
        o_inter = jnp.einsum('bhtk,bhkv->bhtv', qc * jnp.exp(b), S)
        diff = jnp.where(causal[None, None, :, :, None], b[:, :, :, None, :] - b[:, :, None, :, :], -jnp.inf)
        att = jnp.einsum('bhtk,bhsk,bhtsk->bhts', qc, kc, jnp.exp(diff))
        o = o_inter + jnp.einsum('bhts,bhsv->bhtv', att, vc)
        b_last = b[:, :, -1:, :]
        S_new = jnp.exp(b_last[:, :, 0, :])[..., None] * S + jnp.einsum('bhsk,bhsv->bhkv', kc * jnp.exp(b_last - b), vc)
        return S_new, o

    S, o = lax.scan(step, s0.astype(jnp.float32), (to_chunks(q), to_chunks(k), to_chunks(v), to_chunks(logf)))
    o = o.transpose(1, 0, 3, 2, 4).reshape(B, T, H, VAL_DIM_B)
    return o, S


def finish_layer(x, oa, ob, gb, ga, gg, conv0, l, hgrn_norm_g, w_branch_a, w_branch_b, w_out,
                 ln1_g, ln1_b, w_up, conv_w, conv_b, w_down, ln2_g, ln2_b):
    B, T, _ = x.shape
    ob = rms_norm(ob.astype(x.dtype), hgrn_norm_g[l]) * jax.nn.silu(gb)
    merged = (jax.nn.sigmoid(ga) * (oa.reshape(B, T, -1) @ w_branch_a[l])
              + jax.nn.sigmoid(gg) * (ob.reshape(B, T, -1) @ w_branch_b[l]))
    h = layer_norm(ALPHA * x + merged @ w_out[l], ln1_g[l], ln1_b[l])
    a, g = jnp.split(h @ w_up[l], 2, axis=-1)
    ap = jnp.concatenate([conv0.astype(a.dtype), a], axis=1)
    c = conv_b[l] + sum(ap[:, j:j + T] * conv_w[l, j] for j in range(CONV_W))
    y = (jax.nn.gelu(c, approximate=False) * g) @ w_down[l]
    out = layer_norm(ALPHA * h + y, ln2_g[l], ln2_b[l])
    return out, ap[:, -(CONV_W - 1):]


def setup_inputs(seed: int = 0) -> dict:
    key = jax.random.key(seed)
    ks = jax.random.split(key, 32)
    f32 = jnp.float32

    def nrm(k, shape, s):
        return jax.random.normal(k, shape, f32) * s

    n_pages = PAST_LEN // PAGE_SIZE
    n_used = DEC_BATCH * n_pages
    n_phys = n_used + n_used // 4
    page_table = jax.random.permutation(ks[4], n_phys)[:n_used].reshape(DEC_BATCH, n_pages).astype(jnp.int32)
    return {
        'x_prompt': nrm(ks[0], (BATCH, SEQ, D_MODEL), 1.0),
        'x_sample': nrm(ks[1], (DEC_BATCH, DEC_SEQ, D_MODEL), 1.0),
        'cache_k': nrm(ks[2], (n_phys, DEPTH, PAGE_SIZE, N_HEADS_A, 2, HEAD_DIM_A), 1.0),
        'cache_v': nrm(ks[3], (n_phys, DEPTH, PAGE_SIZE, N_HEADS_A, 2 * HEAD_DIM_A), 1.0),
        'state_hgrn': nrm(ks[5], (DEC_BATCH, DEPTH, N_HEADS_B, KEY_DIM_B, VAL_DIM_B), 0.5),
        'state_conv': nrm(ks[6], (DEC_BATCH, DEPTH, CONV_W - 1, D_FF), 1.0),
        'page_table': page_table,
        'w_in': nrm(ks[7], (DEPTH, D_MODEL, IN_COLS), D_MODEL ** -0.5),
        'lambda_q1': nrm(ks[8], (DEPTH, HEAD_DIM_A), 0.1),
        'lambda_k1': nrm(ks[9], (DEPTH, HEAD_DIM_A), 0.1),
        'lambda_q2': nrm(ks[10], (DEPTH, HEAD_DIM_A), 0.1),
        'lambda_k2': nrm(ks[11], (DEPTH, HEAD_DIM_A), 0.1),
        'subln_g': 1.0 + nrm(ks[12], (DEPTH, 2 * HEAD_DIM_A), 0.02),
        'lb_logits': nrm(ks[13], (DEPTH + 1, N_HEADS_B * KEY_DIM_B), 0.1),
        'hgrn_norm_g': 1.0 + nrm(ks[14], (DEPTH, VAL_DIM_B), 0.02),
        'w_branch_a': nrm(ks[15], (DEPTH, VA_DIM, D_MODEL), VA_DIM ** -0.5),
        'w_branch_b': nrm(ks[16], (DEPTH, IB_DIM, D_MODEL), IB_DIM ** -0.5),
        'w_out': nrm(ks[17], (DEPTH, D_MODEL, D_MODEL), BETA * D_MODEL ** -0.5),
        'ln1_g': 1.0 + nrm(ks[18], (DEPTH, D_MODEL), 0.02),
        'ln1_b': nrm(ks[19], (DEPTH, D_MODEL), 0.02),
        'w_up': nrm(ks[20], (DEPTH, D_MODEL, 2 * D_FF), D_MODEL ** -0.5),
        'conv_w': nrm(ks[21], (DEPTH, CONV_W, D_FF), CONV_W ** -0.5),
        'conv_b': nrm(ks[22], (DEPTH, D_FF), 0.02),
        'w_down': nrm(ks[23], (DEPTH, D_FF, D_MODEL), BETA * D_FF ** -0.5),
        'ln2_g': 1.0 + nrm(ks[24], (DEPTH, D_MODEL), 0.02),
        'ln2_b': nrm(ks[25], (DEPTH, D_MODEL), 0.02),
    }


def reference(x_prompt, x_sample, cache_k, cache_v, state_hgrn, state_conv, page_table,
              w_in, lambda_q1, lambda_k1, lambda_q2, lambda_k2, subln_g, lb_logits, hgrn_norm_g,
              w_branch_a, w_branch_b, w_out, ln1_g, ln1_b, w_up, conv_w, conv_b, w_down, ln2_g, ln2_b):
    f32 = jnp.float32
    Bp, Tp = x_prompt.shape[0], x_prompt.shape[1]
    Bs, Ts = x_sample.shape[0], x_sample.shape[1]
    past_len = page_table.shape[1] * cache_k.shape[2]
    pos_p = jnp.arange(Tp)
    pos_s = past_len + jnp.arange(Ts)
    lb_all = jnp.cumsum(jax.nn.softmax(lb_logits.astype(f32), axis=0), axis=0)
    sample_masks = (jnp.ones((Ts, past_len), bool), jnp.tril(jnp.ones((Ts, Ts), bool)))
    ffn_args = (hgrn_norm_g, w_branch_a, w_branch_b, w_out, ln1_g, ln1_b, w_up, conv_w, conv_b, w_down, ln2_g, ln2_b)

    xp, xs = x_prompt, x_sample
    kp_l, vp_l, sp_l, cp_l, ks_l, vs_l, ss_l, cs_l = [], [], [], [], [], [], [], []
    for l in range(DEPTH):
        lam_init = 0.8 - 0.6 * math.exp(-0.3 * l)
        lam = (jnp.exp(jnp.sum(lambda_q1[l].astype(f32) * lambda_k1[l].astype(f32)))
               - jnp.exp(jnp.sum(lambda_q2[l].astype(f32) * lambda_k2[l].astype(f32))) + lam_init)

        qa, ka, va, qb, kb, logf, ib, gb, ga, gg = mixer_inputs(xp, pos_p, w_in[l], lb_all[l])
        oa = prompt_diff_attention(qa, ka, va, lam)
        oa = rms_norm(oa, subln_g[l]) * (1.0 - lam_init)
        s0 = jnp.zeros((Bp, N_HEADS_B, KEY_DIM_B, VAL_DIM_B), f32)
        ob, S_p = hgrn2_scan(qb, kb, ib, logf, s0)
        conv0 = jnp.zeros((Bp, CONV_W - 1, D_FF), xp.dtype)
        xp_new, conv_p = finish_layer(xp, oa, ob, gb, ga, gg, conv0, l, *ffn_args)
        kp_l.append(ka); vp_l.append(va); sp_l.append(S_p.astype(xp.dtype)); cp_l.append(conv_p)

        qa, ka, va, qb, kb, logf, ib, gb, ga, gg = mixer_inputs(xs, pos_s, w_in[l], lb_all[l])
        k_past = cache_k[page_table, l].reshape(Bs, past_len, N_HEADS_A, 2, HEAD_DIM_A)
        v_past = cache_v[page_table, l].reshape(Bs, past_len, N_HEADS_A, 2 * HEAD_DIM_A)
        oa = diff_attention(qa, (k_past.astype(ka.dtype), ka), (v_past.astype(va.dtype), va), sample_masks, lam)
        oa = rms_norm(oa, subln_g[l]) * (1.0 - lam_init)
        ob, S_s = hgrn2_scan(qb, kb, ib, logf, state_hgrn[:, l])
        xs_new, conv_s = finish_layer(xs, oa, ob, gb, ga, gg, state_conv[:, l], l, *ffn_args)
        ks_l.append(ka); vs_l.append(va); ss_l.append(S_s.astype(xs.dtype)); cs_l.append(conv_s)

        xp, xs = xp_new, xs_new

    k_prompt = jnp.stack(kp_l, axis=1)
    v_prompt = jnp.stack(vp_l, axis=1)
    hgrn_prompt = jnp.stack(sp_l, axis=1)
    conv_prompt = jnp.stack(cp_l, axis=1)
    k_sample = jnp.stack(ks_l, axis=1)
    v_sample = jnp.stack(vs_l, axis=1)
    hgrn_sample = jnp.stack(ss_l, axis=1)
    conv_sample = jnp.stack(cs_l, axis=1)
    return (xp, xs, k_prompt, v_prompt, hgrn_prompt, conv_prompt, k_sample, v_sample, hgrn_sample, conv_sample)
```

```python
import functools
import math

import jax
import jax.numpy as jnp
from jax import lax
from jax.experimental import pallas as pl
from jax.experimental.pallas import tpu as pltpu

F32 = jnp.float32
BF16 = jnp.bfloat16

N_HEADS_A = 4
HEAD_DIM_A = 64
N_HEADS_B = 4
KEY_DIM_B = 128
CONV_W = 3
ROPE_THETA = 10000.0
LN_EPS = 1e-5
RMS_EPS = 1e-5

LANES = 128
HEAD_W = 2 * HEAD_DIM_A
MIX_W = N_HEADS_A * HEAD_W
NEG_BIG = -1e30

VMEM_LIMIT = 56 * 1024 * 1024


def _dot(a, b):
    return jnp.dot(a, b, preferred_element_type=F32)


def _dot_nt(a, b):
    return lax.dot_general(a, b, (((1,), (1,)), ((), ())), preferred_element_type=F32)


def _dot_tn(a, b):
    return lax.dot_general(a, b, (((0,), (0,)), ((), ())), preferred_element_type=F32)


def _layer_norm(x, g, b):
    mu = jnp.mean(x, axis=-1, keepdims=True)
    xc = x - mu
    var = jnp.mean(xc * xc, axis=-1, keepdims=True)
    return xc * lax.rsqrt(var + LN_EPS) * g + b


def _lambda(lam_ref, lam_init):
    lv = lam_ref[...]
    e1 = jnp.exp(jnp.sum(lv[0:1] * lv[1:2], axis=-1, keepdims=True))
    e2 = jnp.exp(jnp.sum(lv[2:3] * lv[3:4], axis=-1, keepdims=True))
    return e1 - e2 + lam_init


def _rope(x, cos, sin_signed):
    lane = lax.broadcasted_iota(jnp.int32, x.shape, 1)
    first_half = (lane % HEAD_DIM_A) < (HEAD_DIM_A // 2)
    partner = jnp.where(first_half, pltpu.roll(x, LANES - HEAD_DIM_A // 2, 1),
                        pltpu.roll(x, HEAD_DIM_A // 2, 1))
    return x * cos + partner * sin_signed


def _mixer_kernel(layer, transposed, x_ref, w_ref, cos_ref, sin_ref, lbl_ref, *out_refs):
    if transposed:
        k_ref, v_ref, qt_ref, kb_ref, vt_ref, qb_ref, kk_ref, lf_ref, ib_ref = out_refs
    else:
        k_ref, v_ref, q_ref, qb_ref, kk_ref, lf_ref, ib_ref = out_refs
    xb = x_ref[...].astype(BF16)
    cos = cos_ref[...]
    sin = sin_ref[...]

    def section(i):
        return _dot(xb, w_ref[:, i * MIX_W:(i + 1) * MIX_W])

    scale = HEAD_DIM_A ** -0.5
    qa = section(0)
    qa = jnp.concatenate([_rope(qa[:, h * HEAD_W:(h + 1) * HEAD_W], cos, sin) * scale
                          for h in range(N_HEADS_A)], axis=1)
    ka = section(1)
    ka = jnp.concatenate([_rope(ka[:, h * HEAD_W:(h + 1) * HEAD_W], cos, sin)
                          for h in range(N_HEADS_A)], axis=1)
    va = section(2)
    k_ref[...] = ka
    v_ref[...] = va
    if transposed:
        qt_ref[...] = qa.T.astype(BF16)
        kb_ref[...] = ka.astype(BF16)
        vt_ref[...] = va.T.astype(BF16)
    else:
        q_ref[...] = qa

    qb_ref[...] = section(3)
    fr = section(4)
    ib_ref[...] = section(5)

    lg = lbl_ref[...]
    e = jnp.exp(lg - jnp.max(lg, axis=0, keepdims=True))
    lb = jnp.sum(e[:layer + 1], axis=0, keepdims=True) / jnp.sum(e, axis=0, keepdims=True)
    log_lb = jnp.log(lb)
    log_1mlb = jnp.log1p(-lb)
    log_sig = jnp.minimum(fr, 0.0) - jnp.log1p(jnp.exp(-jnp.abs(fr)))
    t = log_1mlb + log_sig
    hi = jnp.maximum(log_lb, t)
    lo = jnp.minimum(log_lb, t)
    lf_ref[...] = hi + jnp.log1p(jnp.exp(lo - hi))
    kk_ref[...] = (1.0 - lb) * (1.0 / (1.0 + jnp.exp(fr)))


def _mixer_proj(x, w_mix, cos, sin, lb_logits, layer, transposed, tm):
    n = x.shape[0]
    d = x.shape[1]
    n_tiles = n // tm
    pos_tiles = cos.shape[0] // tm
    row = lambda i: (i, 0)
    f32_out = jax.ShapeDtypeStruct((n, MIX_W), F32)
    row_spec = pl.BlockSpec((tm, MIX_W), row)
    if transposed:
        t_shape = jax.ShapeDtypeStruct((n_tiles, MIX_W, tm), BF16)
        t_spec = pl.BlockSpec((None, MIX_W, tm), lambda i: (i, 0, 0))
        out_shape = [f32_out, f32_out, t_shape, jax.ShapeDtypeStruct((n, MIX_W), BF16), t_shape,
                     f32_out, f32_out, f32_out, f32_out]
        out_specs = [row_spec, row_spec, t_spec, row_spec, t_spec,
                     row_spec, row_spec, row_spec, row_spec]
    else:
        out_shape = [f32_out] * 7
        out_specs = [row_spec] * 7
    return pl.pallas_call(
        functools.partial(_mixer_kernel, layer, transposed),
        grid=(n_tiles,),
        in_specs=[
            pl.BlockSpec((tm, d), row),
            pl.BlockSpec(w_mix.shape, lambda i: (0, 0)),
            pl.BlockSpec((tm, LANES), lambda i: (i % pos_tiles, 0)),
            pl.BlockSpec((tm, LANES), lambda i: (i % pos_tiles, 0)),
            pl.BlockSpec(lb_logits.shape, lambda i: (0, 0)),
        ],
        out_specs=out_specs,
        out_shape=out_shape,
        compiler_params=pltpu.CompilerParams(
            dimension_semantics=("parallel",), vmem_limit_bytes=VMEM_LIMIT),
        name="mixer_proj",
    )(x, w_mix, cos, sin, lb_logits)


def _prompt_attn_kernel(lam_init, tq, qt_ref, k_ref, vt_ref, lam_ref, g_ref, o_ref,
                        m_ref, l_ref, acc_ref):
    qi = pl.program_id(2)
    qt = qt_ref[...]
    qrow = lax.broadcasted_iota(jnp.int32, qt.shape, 0)
    zero = jnp.zeros_like(qt)
    qmaps = (jnp.where(qrow < HEAD_DIM_A, qt, zero), jnp.where(qrow >= HEAD_DIM_A, qt, zero))

    m_ref[...] = jnp.full(m_ref.shape, NEG_BIG, F32)
    l_ref[...] = jnp.zeros(l_ref.shape, F32)
    acc_ref[...] = jnp.zeros(acc_ref.shape, F32)

    def block(j, masked):
        kj = k_ref[pl.ds(pl.multiple_of(j * tq, tq), tq), :]
        vj = vt_ref[j]
        for mp in range(2):
            s = _dot(kj, qmaps[mp])
            if masked:
                krow = lax.broadcasted_iota(jnp.int32, s.shape, 0)
                qcol = lax.broadcasted_iota(jnp.int32, s.shape, 1)
                s = jnp.where(krow <= qcol, s, NEG_BIG)
            m_old = m_ref[mp]
            m_new = jnp.maximum(m_old, jnp.max(s, axis=0, keepdims=True))
            alpha = jnp.exp(m_old - m_new)
            p = jnp.exp(s - m_new)
            l_ref[mp] = alpha * l_ref[mp] + jnp.sum(p, axis=0, keepdims=True)
            acc_ref[mp] = alpha * acc_ref[mp] + _dot(vj, p.astype(BF16))
            m_ref[mp] = m_new

    def body(j, carry):
        block(j, False)
        return carry

    lax.fori_loop(0, qi, body, 0)
    block(qi, True)

    lam = _lambda(lam_ref, lam_init)
    o = acc_ref[0] / l_ref[0] - lam * (acc_ref[1] / l_ref[1])
    ms = jnp.mean(o * o, axis=0, keepdims=True)
    o = o * lax.rsqrt(ms + RMS_EPS) * g_ref[...] * (1.0 - lam_init)
    o_ref[...] = o.T.astype(o_ref.dtype)


def _prompt_attention(qt3, k_bf, vt3, lam_vecs, g_col, lam_init, batch, seq, tq):
    nq = seq // tq
    n = batch * seq
    return pl.pallas_call(
        functools.partial(_prompt_attn_kernel, lam_init, tq),
        grid=(batch, N_HEADS_A, nq),
        in_specs=[
            pl.BlockSpec((None, HEAD_W, tq), lambda b, h, i: (b * nq + i, h, 0)),
            pl.BlockSpec((seq, HEAD_W), lambda b, h, i: (b, h)),
            pl.BlockSpec((nq, HEAD_W, tq), lambda b, h, i: (b, h, 0)),
            pl.BlockSpec(lam_vecs.shape, lambda b, h, i: (0, 0)),
            pl.BlockSpec(g_col.shape, lambda b, h, i: (0, 0)),
        ],
        out_specs=pl.BlockSpec((tq, HEAD_W), lambda b, h, i: (b * nq + i, h)),
        out_shape=jax.ShapeDtypeStruct((n, MIX_W), BF16),
        scratch_shapes=[
            pltpu.VMEM((2, 1, tq), F32),
            pltpu.VMEM((2, 1, tq), F32),
            pltpu.VMEM((2, HEAD_W, tq), F32),
        ],
        compiler_params=pltpu.CompilerParams(
            dimension_semantics=("parallel", "parallel", "arbitrary"),
            vmem_limit_bytes=VMEM_LIMIT),
        name="prompt_diff_attention",
    )(qt3, k_bf, vt3, lam_vecs, g_col)


def _sample_attn_kernel(lam_init, pages_per_step, dec_seq, pt_ref, q_ref, kn_ref, vn_ref,
                        lam_ref, g_ref, *rest):
    k_refs = rest[:pages_per_step]
    v_refs = rest[pages_per_step:2 * pages_per_step]
    o_ref = rest[2 * pages_per_step]
    m_ref, l_ref, acc_ref = rest[2 * pages_per_step + 1:]
    j = pl.program_id(1)
    n_rows = q_ref.shape[0]

    qrep = q_ref[...]
    rrow = lax.broadcasted_iota(jnp.int32, qrep.shape, 0)
    rcol = lax.broadcasted_iota(jnp.int32, qrep.shape, 1)
    qbd = jnp.where(rcol // HEAD_DIM_A == rrow // dec_seq, qrep, 0.0).astype(BF16)

    @pl.when(j == 0)
    def _():
        m_ref[...] = jnp.full(m_ref.shape, NEG_BIG, F32)
        l_ref[...] = jnp.zeros(l_ref.shape, F32)
        acc_ref[...] = jnp.zeros(acc_ref.shape, F32)

    def update(s, v_bf):
        m_old = m_ref[...]
        m_new = jnp.maximum(m_old, jnp.max(s, axis=-1, keepdims=True))
        alpha = jnp.exp(m_old - m_new)
        p = jnp.exp(s - m_new)
        l_ref[...] = alpha * l_ref[...] + jnp.sum(p, axis=-1, keepdims=True)
        acc_ref[...] = alpha * acc_ref[...] + _dot(p.astype(BF16), v_bf)
        m_ref[...] = m_new

    for pg in range(pages_per_step):
        kp = k_refs[pg][...].astype(BF16)
        vp = v_refs[pg][...].astype(BF16)
        update(_dot_nt(qbd, kp), vp)

    @pl.when(j == pl.num_programs(1) - 1)
    def _():
        kn = kn_ref[...].astype(BF16)
        vn = vn_ref[...].astype(BF16)
        s = _dot_nt(qbd, kn)
        key = lax.broadcasted_iota(jnp.int32, s.shape, 1)
        tok = lax.broadcasted_iota(jnp.int32, s.shape, 0) % dec_seq
        s = jnp.where(key <= tok, s, NEG_BIG)
        update(s, vn)
        lam = _lambda(lam_ref, lam_init)
        acc = acc_ref[...] / l_ref[...]
        rows_per_head = 2 * dec_seq
        outs = []
        for h in range(N_HEADS_A):
            blk = acc[h * rows_per_head:(h + 1) * rows_per_head, h * HEAD_W:(h + 1) * HEAD_W]
            o = blk[:dec_seq] - lam * blk[dec_seq:]
            ms = jnp.mean(o * o, axis=-1, keepdims=True)
            outs.append(o * lax.rsqrt(ms + RMS_EPS) * g_ref[...] * (1.0 - lam_init))
        o_ref[...] = jnp.concatenate(outs, axis=1).astype(o_ref.dtype)


def _sample_attention(page_table, q_rep, k_new, v_new, cache_k2, cache_v2, lam_vecs, g_row,
                      lam_init, layer, dec_seq, pages_per_step):
    bs, n_pages = page_table.shape
    page = cache_k2.shape[2]
    n_rows = q_rep.shape[1]
    steps = n_pages // pages_per_step

    def page_spec(pg):
        return pl.BlockSpec((None, None, page, MIX_W),
                            lambda b, j, pt: (pt[b, j * pages_per_step + pg], layer, 0, 0))

    per_b = lambda b, j, pt: (b, 0, 0)
    const = lambda b, j, pt: (0, 0)
    grid_spec = pltpu.PrefetchScalarGridSpec(
        num_scalar_prefetch=1,
        grid=(bs, steps),
        in_specs=[
            pl.BlockSpec((None, n_rows, MIX_W), per_b),
            pl.BlockSpec((None, k_new.shape[1], MIX_W), per_b),
            pl.BlockSpec((None, k_new.shape[1], MIX_W), per_b),
            pl.BlockSpec(lam_vecs.shape, const),
            pl.BlockSpec(g_row.shape, const),
        ] + [page_spec(pg) for pg in range(pages_per_step)] * 2,
        out_specs=pl.BlockSpec((None, dec_seq, MIX_W), per_b),
        scratch_shapes=[
            pltpu.VMEM((n_rows, 1), F32),
            pltpu.VMEM((n_rows, 1), F32),
            pltpu.VMEM((n_rows, MIX_W), F32),
        ],
    )
    return pl.pallas_call(
        functools.partial(_sample_attn_kernel, lam_init, pages_per_step, dec_seq),
        grid_spec=grid_spec,
        out_shape=jax.ShapeDtypeStruct((bs, dec_seq, MIX_W), BF16),
        compiler_params=pltpu.CompilerParams(
            dimension_semantics=("parallel", "arbitrary"), vmem_limit_bytes=VMEM_LIMIT),
        name="sample_diff_attention",
    )(page_table, q_rep, k_new, v_new, lam_vecs, g_row,
      *([cache_k2] * pages_per_step), *([cache_v2] * pages_per_step))


def _hgrn_kernel(step, n_steps, q_ref, k_ref, lf_ref, v_ref, s0_ref, o_ref, sout_ref, st_ref):
    ti = pl.program_id(1)

    @pl.when(ti == 0)
    def _():
        for h in range(N_HEADS_B):
            st_ref[h] = s0_ref[h].T

    ones = jnp.ones((KEY_DIM_B, LANES), BF16)
    rowi = lax.broadcasted_iota(jnp.int32, (step, KEY_DIM_B), 0)

    def body(i, carry):
        r0 = pl.multiple_of(i * step, step)
        for h in range(N_HEADS_B):
            cols = slice(h * KEY_DIM_B, (h + 1) * KEY_DIM_B)
            q = q_ref[pl.ds(r0, step), cols]
            k = k_ref[pl.ds(r0, step), cols]
            v = v_ref[pl.ds(r0, step), cols]
            b = lf_ref[pl.ds(r0, step), cols]
            d = 1
            while d < step:
                b = b + jnp.where(rowi >= d, pltpu.roll(b, d, 0), 0.0)
                d *= 2
            st = st_ref[h]
            o = _dot_nt((q * jnp.exp(b)).astype(BF16), st.astype(BF16))
            xs = []
            for s in range(step):
                e = jnp.exp(jnp.minimum(b - b[s:s + 1], 0.0))
                xs.append(jnp.where(rowi >= s, q * k[s:s + 1] * e, 0.0))
            att = _dot(jnp.concatenate(xs, axis=0).astype(BF16), ones)
            for s in range(step):
                o = o + att[s * step:(s + 1) * step] * v[s:s + 1]
            o_ref[pl.ds(r0, step), cols] = o
            b_last = b[step - 1:step]
            kd = k * jnp.exp(b_last - b)
            st_ref[h] = st * jnp.exp(b_last) + _dot_tn(v.astype(BF16), kd.astype(BF16))
        return carry

    lax.fori_loop(0, n_steps, body, 0)

    @pl.when(ti == pl.num_programs(1) - 1)
    def _():
        for h in range(N_HEADS_B):
            sout_ref[h] = st_ref[h].T


def _hgrn_scan(q, k, lf, v, s0, n_seq, seq_rows, tile_rows, step):
    n = q.shape[0]
    tiles = seq_rows // tile_rows
    row = lambda s, t: (s * tiles + t, 0)
    spec = pl.BlockSpec((tile_rows, MIX_W), row)
    st_spec = pl.BlockSpec((None, N_HEADS_B, KEY_DIM_B, KEY_DIM_B), lambda s, t: (s, 0, 0, 0))
    return pl.pallas_call(
        functools.partial(_hgrn_kernel, step, tile_rows // step),
        grid=(n_seq, tiles),
        in_specs=[spec, spec, spec, spec, st_spec],
        out_specs=[spec, st_spec],
        out_shape=[jax.ShapeDtypeStruct((n, MIX_W), F32),
                   jax.ShapeDtypeStruct(s0.shape, F32)],
        scratch_shapes=[pltpu.VMEM((N_HEADS_B, KEY_DIM_B, KEY_DIM_B), F32)],
        compiler_params=pltpu.CompilerParams(
            dimension_semantics=("parallel", "arbitrary"), vmem_limit_bytes=VMEM_LIMIT),
        name="hgrn2_scan",
    )(q, k, lf, v, s0)


def _finish_kernel(alpha, d_ff, seq_rows, x_ref, oa_ref, ob_ref, c0_ref, c1_ref,
                   wg_ref, ng_ref, wa_ref, wb_ref, wo_ref, l1g_ref, l1b_ref, wu_ref,
                   cw_ref, cb_ref, wd_ref, l2g_ref, l2b_ref, out_ref, conv_ref, *scratch):
    tm = x_ref.shape[0]
    x = x_ref[...]
    gates = _dot(x.astype(BF16), wg_ref[...])
    d_model = x.shape[1]
    gb = gates[:, :MIX_W]
    ga = gates[:, MIX_W:MIX_W + d_model]
    gg = gates[:, MIX_W + d_model:]

    ob = ob_ref[...]
    normed = []
    for h in range(N_HEADS_B):
        oh = ob[:, h * KEY_DIM_B:(h + 1) * KEY_DIM_B]
        ms = jnp.mean(oh * oh, axis=-1, keepdims=True)
        normed.append(oh * lax.rsqrt(ms + RMS_EPS) * ng_ref[...])
    obn = jnp.concatenate(normed, axis=1) * (gb * jax.nn.sigmoid(gb))

    merged = (jax.nn.sigmoid(ga) * _dot(oa_ref[...], wa_ref[...])
              + jax.nn.sigmoid(gg) * _dot(obn.astype(BF16), wb_ref[...]))
    h1 = _layer_norm(alpha * x + _dot(merged.astype(BF16), wo_ref[...]), l1g_ref[...], l1b_ref[...])

    up = _dot(h1.astype(BF16), wu_ref[...])
    a = up[:, :d_ff]
    g = up[:, d_ff:]
    cw = cw_ref[...]
    if seq_rows is None:
        (a_ref,) = scratch
        ti = pl.program_id(1)

        @pl.when(ti == 0)
        def _():
            a_ref[6:8, :] = c0_ref[...]

        a_ref[8:8 + tm, :] = a
        a1 = a_ref[7:7 + tm, :]
        a2 = a_ref[6:6 + tm, :]
        tail = a_ref[6 + tm:8 + tm, :]
        a_ref[6:8, :] = tail
        conv_ref[...] = tail
    else:
        pos = lax.broadcasted_iota(jnp.int32, a.shape, 0) % seq_rows
        a1 = jnp.where(pos >= 1, pltpu.roll(a, 1, 0), 0.0) + c1_ref[...]
        a2 = jnp.where(pos >= 2, pltpu.roll(a, 2, 0), 0.0) + c0_ref[...]
        conv_ref[...] = a
    c = cb_ref[...] + a2 * cw[0:1] + a1 * cw[1:2] + a * cw[2:3]
    act = 0.5 * c * (1.0 + lax.erf(c * (2.0 ** -0.5))) * g
    y = _dot(act.astype(BF16), wd_ref[...])
    out_ref[...] = _layer_norm(alpha * h1 + y, l2g_ref[...], l2b_ref[...])


def _finish(x, oa, ob, c0, c1, weights, alpha, n_seq, seq_tiles, tm, seq_rows):
    n, d_model = x.shape
    d_ff = weights[-3].shape[0]
    row = lambda s, t: (s * seq_tiles + t, 0)
    const = lambda s, t: (0, 0)
    once = pl.Buffered(1)

    def wspec(w):
        return pl.BlockSpec(w.shape, const, pipeline_mode=once)

    if seq_rows is None:
        c_specs = [pl.BlockSpec((None, CONV_W - 1, d_ff), lambda s, t: (s, 0, 0))] * 2
        conv_shape = jax.ShapeDtypeStruct((n_seq, CONV_W - 1, d_ff), F32)
        conv_spec = pl.BlockSpec((None, CONV_W - 1, d_ff), lambda s, t: (s, 0, 0))
        scratch = [pltpu.VMEM((tm + 8, d_ff), F32)]
    else:
        c_specs = [pl.BlockSpec((tm, d_ff), row)] * 2
        conv_shape = jax.ShapeDtypeStruct((n, d_ff), F32)
        conv_spec = pl.BlockSpec((tm, d_ff), row)
        scratch = []
    return pl.pallas_call(
        functools.partial(_finish_kernel, alpha, d_ff, seq_rows),
        grid=(n_seq, seq_tiles),
        in_specs=[pl.BlockSpec((tm, d_model), row),
                  pl.BlockSpec((tm, MIX_W), row),
                  pl.BlockSpec((tm, MIX_W), row)] + c_specs + [wspec(w) for w in weights],
        out_specs=[pl.BlockSpec((tm, d_model), row), conv_spec],
        out_shape=[jax.ShapeDtypeStruct((n, d_model), F32), conv_shape],
        scratch_shapes=scratch,
        compiler_params=pltpu.CompilerParams(
            dimension_semantics=("parallel", "arbitrary"), vmem_limit_bytes=VMEM_LIMIT),
        name="finish_layer",
    )(x, oa, ob, c0, c1, *weights)


MIXER_TILE = 512
HGRN_TILE = 1024
HGRN_STEP = 16
FINISH_TILE = 256
PAGES_PER_STEP = 4
NEW_KEY_PAD = 16


def _rope_tables(pos):
    half = HEAD_DIM_A // 2
    inv = ROPE_THETA ** (-jnp.arange(half, dtype=F32) * 2.0 / HEAD_DIM_A)
    ang = pos.astype(F32)[:, None] * inv[None, :]
    cos = jnp.cos(ang)
    sin = jnp.sin(ang)
    reps = LANES // HEAD_DIM_A
    return (jnp.tile(jnp.concatenate([cos, cos], axis=1), (1, reps)),
            jnp.tile(jnp.concatenate([-sin, sin], axis=1), (1, reps)))


def kernel(x_prompt, x_sample, cache_k, cache_v, state_hgrn, state_conv, page_table, w_in, lambda_q1, lambda_k1, lambda_q2, lambda_k2, subln_g, lb_logits, hgrn_norm_g, w_branch_a, w_branch_b, w_out, ln1_g, ln1_b, w_up, conv_w, conv_b, w_down, ln2_g, ln2_b):
    bp, tp, d_model = x_prompt.shape
    bs, ts, _ = x_sample.shape
    depth = w_in.shape[0]
    n_phys, _, page, _, _, _ = cache_k.shape
    past_len = page_table.shape[1] * page
    d_ff = w_down.shape[1]
    alpha = (2 * depth) ** 0.25
    n_mix = 6 * MIX_W
    ts_pad = 8
    lb_logits = lb_logits.astype(F32)

    cos_p, sin_p = _rope_tables(jnp.arange(tp))
    cos_s, sin_s = _rope_tables(past_len + jnp.arange(bs * ts) % ts)
    cache_k2 = cache_k.reshape(n_phys, depth, page, MIX_W)
    cache_v2 = cache_v.reshape(n_phys, depth, page, MIX_W)

    xp = x_prompt.reshape(bp * tp, d_model)
    xs = x_sample.reshape(bs * ts, d_model)
    outs = {name: [] for name in ("kp", "vp", "sp", "cp", "ks", "vs", "ss", "cs")}
    for l in range(depth):
        lam_init = 0.8 - 0.6 * math.exp(-0.3 * l)
        w_mix = w_in[l, :, :n_mix].astype(BF16)
        lam_vecs = jnp.stack([lambda_q1[l], lambda_k1[l], lambda_q2[l], lambda_k2[l]]).astype(F32)
        g_row = subln_g[l].reshape(1, HEAD_W).astype(F32)
        weights = (
            w_in[l, :, n_mix:].astype(BF16),
            hgrn_norm_g[l].reshape(1, KEY_DIM_B).astype(F32),
            w_branch_a[l].astype(BF16), w_branch_b[l].astype(BF16), w_out[l].astype(BF16),
            ln1_g[l].reshape(1, d_model), ln1_b[l].reshape(1, d_model),
            w_up[l].astype(BF16), conv_w[l], conv_b[l].reshape(1, d_ff),
            w_down[l].astype(BF16), ln2_g[l].reshape(1, d_model), ln2_b[l].reshape(1, d_model))

        k_p, v_p, qt3, k_bf, vt3, qb, kk, lf, ib = _mixer_proj(
            xp, w_mix, cos_p, sin_p, lb_logits, l, True, MIXER_TILE)
        oa = _prompt_attention(qt3, k_bf, vt3, lam_vecs, g_row.reshape(HEAD_W, 1), lam_init,
                               bp, tp, MIXER_TILE)
        s0 = jnp.zeros((bp, N_HEADS_B, KEY_DIM_B, KEY_DIM_B), F32)
        ob, s_p = _hgrn_scan(qb, kk, lf, ib, s0, bp, tp, HGRN_TILE, HGRN_STEP)
        conv0 = jnp.zeros((bp, CONV_W - 1, d_ff), F32)
        xp_new, conv_p = _finish(xp, oa, ob, conv0, conv0, weights, alpha,
                                 bp, tp // FINISH_TILE, FINISH_TILE, None)
        outs["kp"].append(k_p.reshape(bp, tp, N_HEADS_A, 2, HEAD_DIM_A))
        outs["vp"].append(v_p.reshape(bp, tp, N_HEADS_A, HEAD_W))
        outs["sp"].append(s_p)
        outs["cp"].append(conv_p)

        k_s, v_s, q_s, qb, kk, lf, ib = _mixer_proj(
            xs, w_mix, cos_s, sin_s, lb_logits, l, False, bs * ts)
        pad_t = lambda a, rows: jnp.pad(a.reshape(bs, ts, MIX_W), ((0, 0), (0, rows - ts), (0, 0)))
        q_rep = jnp.tile(q_s.reshape(bs, ts, MIX_W), (1, 2 * N_HEADS_A, 1))
        oa = _sample_attention(page_table, q_rep, pad_t(k_s, NEW_KEY_PAD), pad_t(v_s, NEW_KEY_PAD),
                               cache_k2, cache_v2, lam_vecs, g_row, lam_init, l, ts, PAGES_PER_STEP)
        flat = lambda a: pad_t(a, ts_pad).reshape(bs * ts_pad, MIX_W)
        ob, s_s = _hgrn_scan(flat(qb), flat(kk), flat(lf), flat(ib), state_hgrn[:, l],
                             bs, ts_pad, ts_pad, ts_pad)
        ob = ob.reshape(bs, ts_pad, MIX_W)[:, :ts].reshape(bs * ts, MIX_W)
        st = state_conv[:, l]
        c0 = jnp.concatenate([st, jnp.zeros((bs, ts - 2, d_ff), F32)], axis=1)
        c1 = jnp.concatenate([st[:, 1:], jnp.zeros((bs, ts - 1, d_ff), F32)], axis=1)
        xs_new, a_s = _finish(xs, oa.reshape(bs * ts, MIX_W), ob,
                              c0.reshape(bs * ts, d_ff), c1.reshape(bs * ts, d_ff),
                              weights, alpha, 1, 1, bs * ts, ts)
        conv_s = jnp.concatenate([st, a_s.reshape(bs, ts, d_ff)], axis=1)[:, -(CONV_W - 1):]
        outs["ks"].append(k_s.reshape(bs, ts, N_HEADS_A, 2, HEAD_DIM_A))
        outs["vs"].append(v_s.reshape(bs, ts, N_HEADS_A, HEAD_W))
        outs["ss"].append(s_s)
        outs["cs"].append(conv_s)
        xp, xs = xp_new, xs_new

    stack = lambda name: jnp.stack(outs[name], axis=1)
    return (xp.reshape(bp, tp, d_model), xs.reshape(bs, ts, d_model),
            stack("kp"), stack("vp"), stack("sp"), stack("cp"),
            stack("ks"), stack("vs"), stack("ss"), stack("cs"))
```

```python
import functools
import math

import jax
import jax.numpy as jnp
from jax import lax
from jax.experimental import pallas as pl
from jax.experimental.pallas import tpu as pltpu

F32 = jnp.float32
BF16 = jnp.bfloat16

N_HEADS_A = 4
HEAD_DIM_A = 64
N_HEADS_B = 4
KEY_DIM_B = 128
CONV_W = 3
ROPE_THETA = 10000.0
LN_EPS = 1e-5
RMS_EPS = 1e-5

LANES = 128
HEAD_W = 2 * HEAD_DIM_A
MIX_W = N_HEADS_A * HEAD_W
NEG_BIG = -1e30

VMEM_LIMIT = 56 * 1024 * 1024


def _dot(a, b):
    return jnp.dot(a, b, preferred_element_type=F32)


def _dot_nt(a, b):
    return lax.dot_general(a, b, (((1,), (1,)), ((), ())), preferred_element_type=F32)


def _dot_tn(a, b):
    return lax.dot_general(a, b, (((0,), (0,)), ((), ())), preferred_element_type=F32)


def _layer_norm(x, g, b):
    mu = jnp.mean(x, axis=-1, keepdims=True)
    xc = x - mu
    var = jnp.mean(xc * xc, axis=-1, keepdims=True)
    return xc * lax.rsqrt(var + LN_EPS) * g + b


def _lambda(lam_ref, lam_init):
    lv = lam_ref[...]
    e1 = jnp.exp(jnp.sum(lv[0:1] * lv[1:2], axis=-1, keepdims=True))
    e2 = jnp.exp(jnp.sum(lv[2:3] * lv[3:4], axis=-1, keepdims=True))
    return e1 - e2 + lam_init


def _rope(x, cos, sin_signed):
    lane = lax.broadcasted_iota(jnp.int32, x.shape, 1)
    first_half = (lane % HEAD_DIM_A) < (HEAD_DIM_A // 2)
    partner = jnp.where(first_half, pltpu.roll(x, LANES - HEAD_DIM_A // 2, 1),
                        pltpu.roll(x, HEAD_DIM_A // 2, 1))
    return x * cos + partner * sin_signed


def _mixer_kernel(layer, transposed, x_ref, w_ref, cos_ref, sin_ref, lbl_ref, *out_refs):
    if transposed:
        k_ref, v_ref, qt_ref, kb_ref, vt_ref, qb_ref, kk_ref, lf_ref, ib_ref = out_refs
    else:
        k_ref, v_ref, q_ref, qb_ref, kk_ref, lf_ref, ib_ref = out_refs
    xb = x_ref[...].astype(BF16)
    cos = cos_ref[...]
    sin = sin_ref[...]

    def section(i):
        return _dot(xb, w_ref[:, i * MIX_W:(i + 1) * MIX_W])

    scale = HEAD_DIM_A ** -0.5
    qa = section(0)
    qa = jnp.concatenate([_rope(qa[:, h * HEAD_W:(h + 1) * HEAD_W], cos, sin) * scale
                          for h in range(N_HEADS_A)], axis=1)
    ka = section(1)
    ka = jnp.concatenate([_rope(ka[:, h * HEAD_W:(h + 1) * HEAD_W], cos, sin)
                          for h in range(N_HEADS_A)], axis=1)
    va = section(2)
    k_ref[...] = ka
    v_ref[...] = va
    if transposed:
        qt_ref[...] = qa.T.astype(BF16)
        kb_ref[...] = ka.astype(BF16)
        vt_ref[...] = va.T.astype(BF16)
    else:
        q_ref[...] = qa

    qb_ref[...] = section(3)
    fr = section(4)
    ib_ref[...] = section(5)

    lg = lbl_ref[...]
    e = jnp.exp(lg - jnp.max(lg, axis=0, keepdims=True))
    lb = jnp.sum(e[:layer + 1], axis=0, keepdims=True) / jnp.sum(e, axis=0, keepdims=True)
    log_lb = jnp.log(lb)
    log_1mlb = jnp.log1p(-lb)
    log_sig = jnp.minimum(fr, 0.0) - jnp.log1p(jnp.exp(-jnp.abs(fr)))
    t = log_1mlb + log_sig
    hi = jnp.maximum(log_lb, t)
    lo = jnp.minimum(log_lb, t)
    lf_ref[...] = hi + jnp.log1p(jnp.exp(lo - hi))
    kk_ref[...] = (1.0 - lb) * (1.0 / (1.0 + jnp.exp(fr)))


def _mixer_proj(x, w_mix, cos, sin, lb_logits, layer, transposed, tm):
    n = x.shape[0]
    d = x.shape[1]
    n_tiles = n // tm
    pos_tiles = cos.shape[0] // tm
    row = lambda i: (i, 0)
    f32_out = jax.ShapeDtypeStruct((n, MIX_W), F32)
    row_spec = pl.BlockSpec((tm, MIX_W), row)
    if transposed:
        t_shape = jax.ShapeDtypeStruct((n_tiles, MIX_W, tm), BF16)
        t_spec = pl.BlockSpec((None, MIX_W, tm), lambda i: (i, 0, 0))
        out_shape = [f32_out, f32_out, t_shape, jax.ShapeDtypeStruct((n, MIX_W), BF16), t_shape,
                     f32_out, f32_out, f32_out, f32_out]
        out_specs = [row_spec, row_spec, t_spec, row_spec, t_spec,
                     row_spec, row_spec, row_spec, row_spec]
    else:
        out_shape = [f32_out] * 7
        out_specs = [row_spec] * 7
    return pl.pallas_call(
        functools.partial(_mixer_kernel, layer, transposed),
        grid=(n_tiles,),
        in_specs=[
            pl.BlockSpec((tm, d), row),
            pl.BlockSpec(w_mix.shape, lambda i: (0, 0)),
            pl.BlockSpec((tm, LANES), lambda i: (i % pos_tiles, 0)),
            pl.BlockSpec((tm, LANES), lambda i: (i % pos_tiles, 0)),
            pl.BlockSpec(lb_logits.shape, lambda i: (0, 0)),
        ],
        out_specs=out_specs,
        out_shape=out_shape,
        compiler_params=pltpu.CompilerParams(
            dimension_semantics=("parallel",), vmem_limit_bytes=VMEM_LIMIT),
        name="mixer_proj",
    )(x, w_mix, cos, sin, lb_logits)


def _prompt_attn_kernel(lam_init, tq, qt_ref, k_ref, vt_ref, lam_ref, g_ref, o_ref,
                        m_ref, l_ref, acc_ref):
    qi = pl.program_id(2)
    qt = qt_ref[...]
    qrow = lax.broadcasted_iota(jnp.int32, qt.shape, 0)
    zero = jnp.zeros_like(qt)
    qmaps = (jnp.where(qrow < HEAD_DIM_A, qt, zero), jnp.where(qrow >= HEAD_DIM_A, qt, zero))

    m_ref[...] = jnp.full(m_ref.shape, NEG_BIG, F32)
    l_ref[...] = jnp.zeros(l_ref.shape, F32)
    acc_ref[...] = jnp.zeros(acc_ref.shape, F32)

    def block(j, masked):
        kj = k_ref[pl.ds(pl.multiple_of(j * tq, tq), tq), :]
        vj = vt_ref[j]
        for mp in range(2):
            s = _dot(kj, qmaps[mp])
            if masked:
                krow = lax.broadcasted_iota(jnp.int32, s.shape, 0)
                qcol = lax.broadcasted_iota(jnp.int32, s.shape, 1)
                s = jnp.where(krow <= qcol, s, NEG_BIG)
            m_old = m_ref[mp]
            m_new = jnp.maximum(m_old, jnp.max(s, axis=0, keepdims=True))
            alpha = jnp.exp(m_old - m_new)
            p = jnp.exp(s - m_new)
            l_ref[mp] = alpha * l_ref[mp] + jnp.sum(p, axis=0, keepdims=True)
            acc_ref[mp] = alpha * acc_ref[mp] + _dot(vj, p.astype(BF16))
            m_ref[mp] = m_new

    def body(j, carry):
        block(j, False)
        return carry

    lax.fori_loop(0, qi, body, 0)
    block(qi, True)

    lam = _lambda(lam_ref, lam_init)
    o = acc_ref[0] / l_ref[0] - lam * (acc_ref[1] / l_ref[1])
    ms = jnp.mean(o * o, axis=0, keepdims=True)
    o = o * lax.rsqrt(ms + RMS_EPS) * g_ref[...] * (1.0 - lam_init)
    o_ref[...] = o.T.astype(o_ref.dtype)


def _prompt_attention(qt3, k_bf, vt3, lam_vecs, g_col, lam_init, batch, seq, tq):
    nq = seq // tq
    n = batch * seq
    return pl.pallas_call(
        functools.partial(_prompt_attn_kernel, lam_init, tq),
        grid=(batch, N_HEADS_A, nq),
        in_specs=[
            pl.BlockSpec((None, HEAD_W, tq), lambda b, h, i: (b * nq + i, h, 0)),
            pl.BlockSpec((seq, HEAD_W), lambda b, h, i: (b, h)),
            pl.BlockSpec((nq, HEAD_W, tq), lambda b, h, i: (b, h, 0)),
            pl.BlockSpec(lam_vecs.shape, lambda b, h, i: (0, 0)),
            pl.BlockSpec(g_col.shape, lambda b, h, i: (0, 0)),
        ],
        out_specs=pl.BlockSpec((tq, HEAD_W), lambda b, h, i: (b * nq + i, h)),
        out_shape=jax.ShapeDtypeStruct((n, MIX_W), BF16),
        scratch_shapes=[
            pltpu.VMEM((2, 1, tq), F32),
            pltpu.VMEM((2, 1, tq), F32),
            pltpu.VMEM((2, HEAD_W, tq), F32),
        ],
        compiler_params=pltpu.CompilerParams(
            dimension_semantics=("parallel", "parallel", "arbitrary"),
            vmem_limit_bytes=VMEM_LIMIT),
        name="prompt_diff_attention",
    )(qt3, k_bf, vt3, lam_vecs, g_col)


def _sample_attn_kernel(lam_init, pages_per_step, dec_seq, pt_ref, q_ref, kn_ref, vn_ref,
                        lam_ref, g_ref, *rest):
    k_refs = rest[:pages_per_step]
    v_refs = rest[pages_per_step:2 * pages_per_step]
    o_ref = rest[2 * pages_per_step]
    m_ref, l_ref, acc_ref = rest[2 * pages_per_step + 1:]
    j = pl.program_id(1)
    n_rows = q_ref.shape[0]

    qrep = q_ref[...]
    rrow = lax.broadcasted_iota(jnp.int32, qrep.shape, 0)
    rcol = lax.broadcasted_iota(jnp.int32, qrep.shape, 1)
    qbd = jnp.where(rcol // HEAD_DIM_A == rrow // dec_seq, qrep, 0.0).astype(BF16)

    @pl.when(j == 0)
    def _():
        m_ref[...] = jnp.full(m_ref.shape, NEG_BIG, F32)
        l_ref[...] = jnp.zeros(l_ref.shape, F32)
        acc_ref[...] = jnp.zeros(acc_ref.shape, F32)

    def update(s, v_bf):
        m_old = m_ref[...]
        m_new = jnp.maximum(m_old, jnp.max(s, axis=-1, keepdims=True))
        alpha = jnp.exp(m_old - m_new)
        p = jnp.exp(s - m_new)
        l_ref[...] = alpha * l_ref[...] + jnp.sum(p, axis=-1, keepdims=True)
        acc_ref[...] = alpha * acc_ref[...] + _dot(p.astype(BF16), v_bf)
        m_ref[...] = m_new

    page = k_refs[0].shape[1]
    kt = jnp.concatenate([r[...] for r in k_refs], axis=1).astype(BF16)
    vp = jnp.concatenate(
        [jnp.concatenate([r[pl.ds(h, page, stride=N_HEADS_A), :] for h in range(N_HEADS_A)], axis=1)
         for r in v_refs], axis=0).astype(BF16)
    update(_dot(qbd, kt), vp)

    @pl.when(j == pl.num_programs(1) - 1)
    def _():
        kn = kn_ref[...].astype(BF16)
        vn = vn_ref[...].astype(BF16)
        s = _dot_nt(qbd, kn)
        key = lax.broadcasted_iota(jnp.int32, s.shape, 1)
        tok = lax.broadcasted_iota(jnp.int32, s.shape, 0) % dec_seq
        s = jnp.where(key <= tok, s, NEG_BIG)
        update(s, vn)
        lam = _lambda(lam_ref, lam_init)
        acc = acc_ref[...] / l_ref[...]
        rows_per_head = 2 * dec_seq
        outs = []
        for h in range(N_HEADS_A):
            blk = acc[h * rows_per_head:(h + 1) * rows_per_head, h * HEAD_W:(h + 1) * HEAD_W]
            o = blk[:dec_seq] - lam * blk[dec_seq:]
            ms = jnp.mean(o * o, axis=-1, keepdims=True)
            outs.append(o * lax.rsqrt(ms + RMS_EPS) * g_ref[...] * (1.0 - lam_init))
        o_ref[...] = jnp.concatenate(outs, axis=1).astype(o_ref.dtype)


def _sample_attention(page_table, q_rep, k_new, v_new, cache_kt, cache_v, lam_vecs, g_row,
                      lam_init, layer, dec_seq, pages_per_step):
    bs, n_pages = page_table.shape
    page = cache_kt.shape[3]
    n_rows = q_rep.shape[1]
    steps = n_pages // pages_per_step

    def k_spec(pg):
        return pl.BlockSpec((None, None, MIX_W, page),
                            lambda b, j, pt: (pt[b, j * pages_per_step + pg], layer, 0, 0))

    def v_spec(pg):
        return pl.BlockSpec((None, None, page * N_HEADS_A, HEAD_W),
                            lambda b, j, pt: (pt[b, j * pages_per_step + pg], layer, 0, 0))

    per_b = lambda b, j, pt: (b, 0, 0)
    const = lambda b, j, pt: (0, 0)
    grid_spec = pltpu.PrefetchScalarGridSpec(
        num_scalar_prefetch=1,
        grid=(bs, steps),
        in_specs=[
            pl.BlockSpec((None, n_rows, MIX_W), per_b),
            pl.BlockSpec((None, k_new.shape[1], MIX_W), per_b),
            pl.BlockSpec((None, k_new.shape[1], MIX_W), per_b),
            pl.BlockSpec(lam_vecs.shape, const),
            pl.BlockSpec(g_row.shape, const),
        ] + [k_spec(pg) for pg in range(pages_per_step)]
          + [v_spec(pg) for pg in range(pages_per_step)],
        out_specs=pl.BlockSpec((None, dec_seq, MIX_W), per_b),
        scratch_shapes=[
            pltpu.VMEM((n_rows, 1), F32),
            pltpu.VMEM((n_rows, 1), F32),
            pltpu.VMEM((n_rows, MIX_W), F32),
        ],
    )
    return pl.pallas_call(
        functools.partial(_sample_attn_kernel, lam_init, pages_per_step, dec_seq),
        grid_spec=grid_spec,
        out_shape=jax.ShapeDtypeStruct((bs, dec_seq, MIX_W), BF16),
        compiler_params=pltpu.CompilerParams(
            dimension_semantics=("parallel", "arbitrary"), vmem_limit_bytes=VMEM_LIMIT),
        name="sample_diff_attention",
    )(page_table, q_rep, k_new, v_new, lam_vecs, g_row,
      *([cache_kt] * pages_per_step), *([cache_v] * pages_per_step))


def _hgrn_kernel(step, n_steps, q_ref, k_ref, lf_ref, v_ref, s0_ref, o_ref, sout_ref, st_ref):
    ti = pl.program_id(1)

    @pl.when(ti == 0)
    def _():
        for h in range(N_HEADS_B):
            st_ref[h] = s0_ref[h].T

    ones = jnp.ones((KEY_DIM_B, LANES), BF16)
    rowi = lax.broadcasted_iota(jnp.int32, (step, KEY_DIM_B), 0)

    def body(i, carry):
        r0 = pl.multiple_of(i * step, step)
        for h in range(N_HEADS_B):
            cols = slice(h * KEY_DIM_B, (h + 1) * KEY_DIM_B)
            q = q_ref[pl.ds(r0, step), cols]
            k = k_ref[pl.ds(r0, step), cols]
            v = v_ref[pl.ds(r0, step), cols]
            b = lf_ref[pl.ds(r0, step), cols]
            d = 1
            while d < step:
                b = b + jnp.where(rowi >= d, pltpu.roll(b, d, 0), 0.0)
                d *= 2
            st = st_ref[h]
            o = _dot_nt((q * jnp.exp(b)).astype(BF16), st.astype(BF16))
            xs = []
            for s in range(step):
                e = jnp.exp(jnp.minimum(b - b[s:s + 1], 0.0))
                xs.append(jnp.where(rowi >= s, q * k[s:s + 1] * e, 0.0))
            att = _dot(jnp.concatenate(xs, axis=0).astype(BF16), ones)
            for s in range(step):
                o = o + att[s * step:(s + 1) * step] * v[s:s + 1]
            o_ref[pl.ds(r0, step), cols] = o
            b_last = b[step - 1:step]
            kd = k * jnp.exp(b_last - b)
            st_ref[h] = st * jnp.exp(b_last) + _dot_tn(v.astype(BF16), kd.astype(BF16))
        return carry

    lax.fori_loop(0, n_steps, body, 0)

    @pl.when(ti == pl.num_programs(1) - 1)
    def _():
        for h in range(N_HEADS_B):
            sout_ref[h] = st_ref[h].T


def _hgrn_scan(q, k, lf, v, s0, n_seq, seq_rows, tile_rows, step):
    n = q.shape[0]
    tiles = seq_rows // tile_rows
    row = lambda s, t: (s * tiles + t, 0)
    spec = pl.BlockSpec((tile_rows, MIX_W), row)
    st_spec = pl.BlockSpec((None, N_HEADS_B, KEY_DIM_B, KEY_DIM_B), lambda s, t: (s, 0, 0, 0))
    return pl.pallas_call(
        functools.partial(_hgrn_kernel, step, tile_rows // step),
        grid=(n_seq, tiles),
        in_specs=[spec, spec, spec, spec, st_spec],
        out_specs=[spec, st_spec],
        out_shape=[jax.ShapeDtypeStruct((n, MIX_W), F32),
                   jax.ShapeDtypeStruct(s0.shape, F32)],
        scratch_shapes=[pltpu.VMEM((N_HEADS_B, KEY_DIM_B, KEY_DIM_B), F32)],
        compiler_params=pltpu.CompilerParams(
            dimension_semantics=("parallel", "arbitrary"), vmem_limit_bytes=VMEM_LIMIT),
        name="hgrn2_scan",
    )(q, k, lf, v, s0)


def _finish_kernel(alpha, d_ff, seq_rows, x_ref, oa_ref, ob_ref, c0_ref, c1_ref,
                   wg_ref, ng_ref, wa_ref, wb_ref, wo_ref, l1g_ref, l1b_ref, wu_ref,
                   cw_ref, cb_ref, wd_ref, l2g_ref, l2b_ref, out_ref, conv_ref, *scratch):
    tm = x_ref.shape[0]
    x = x_ref[...]
    gates = _dot(x.astype(BF16), wg_ref[...])
    d_model = x.shape[1]
    gb = gates[:, :MIX_W]
    ga = gates[:, MIX_W:MIX_W + d_model]
    gg = gates[:, MIX_W + d_model:]

    ob = ob_ref[...]
    normed = []
    for h in range(N_HEADS_B):
        oh = ob[:, h * KEY_DIM_B:(h + 1) * KEY_DIM_B]
        ms = jnp.mean(oh * oh, axis=-1, keepdims=True)
        normed.append(oh * lax.rsqrt(ms + RMS_EPS) * ng_ref[...])
    obn = jnp.concatenate(normed, axis=1) * (gb * jax.nn.sigmoid(gb))

    merged = (jax.nn.sigmoid(ga) * _dot(oa_ref[...], wa_ref[...])
              + jax.nn.sigmoid(gg) * _dot(obn.astype(BF16), wb_ref[...]))
    h1 = _layer_norm(alpha * x + _dot(merged.astype(BF16), wo_ref[...]), l1g_ref[...], l1b_ref[...])

    up = _dot(h1.astype(BF16), wu_ref[...])
    a = up[:, :d_ff]
    g = up[:, d_ff:]
    cw = cw_ref[...]
    if seq_rows is None:
        (a_ref,) = scratch
        ti = pl.program_id(1)

        @pl.when(ti == 0)
        def _():
            a_ref[6:8, :] = c0_ref[...]

        a_ref[8:8 + tm, :] = a
        a1 = a_ref[7:7 + tm, :]
        a2 = a_ref[6:6 + tm, :]
        tail = a_ref[6 + tm:8 + tm, :]
        a_ref[6:8, :] = tail
        conv_ref[...] = tail
    else:
        pos = lax.broadcasted_iota(jnp.int32, a.shape, 0) % seq_rows
        a1 = jnp.where(pos >= 1, pltpu.roll(a, 1, 0), 0.0) + c1_ref[...]
        a2 = jnp.where(pos >= 2, pltpu.roll(a, 2, 0), 0.0) + c0_ref[...]
        conv_ref[...] = a
    c = cb_ref[...] + a2 * cw[0:1] + a1 * cw[1:2] + a * cw[2:3]
    act = 0.5 * c * (1.0 + lax.erf(c * (2.0 ** -0.5))) * g
    y = _dot(act.astype(BF16), wd_ref[...])
    out_ref[...] = _layer_norm(alpha * h1 + y, l2g_ref[...], l2b_ref[...])


def _finish(x, oa, ob, c0, c1, weights, alpha, n_seq, seq_tiles, tm, seq_rows):
    n, d_model = x.shape
    d_ff = weights[-3].shape[0]
    row = lambda s, t: (s * seq_tiles + t, 0)
    const = lambda s, t: (0, 0)
    once = pl.Buffered(1)

    def wspec(w):
        return pl.BlockSpec(w.shape, const, pipeline_mode=once)

    if seq_rows is None:
        c_specs = [pl.BlockSpec((None, CONV_W - 1, d_ff), lambda s, t: (s, 0, 0))] * 2
        conv_shape = jax.ShapeDtypeStruct((n_seq, CONV_W - 1, d_ff), F32)
        conv_spec = pl.BlockSpec((None, CONV_W - 1, d_ff), lambda s, t: (s, 0, 0))
        scratch = [pltpu.VMEM((tm + 8, d_ff), F32)]
    else:
        c_specs = [pl.BlockSpec((tm, d_ff), row)] * 2
        conv_shape = jax.ShapeDtypeStruct((n, d_ff), F32)
        conv_spec = pl.BlockSpec((tm, d_ff), row)
        scratch = []
    return pl.pallas_call(
        functools.partial(_finish_kernel, alpha, d_ff, seq_rows),
        grid=(n_seq, seq_tiles),
        in_specs=[pl.BlockSpec((tm, d_model), row),
                  pl.BlockSpec((tm, MIX_W), row),
                  pl.BlockSpec((tm, MIX_W), row)] + c_specs + [wspec(w) for w in weights],
        out_specs=[pl.BlockSpec((tm, d_model), row), conv_spec],
        out_shape=[jax.ShapeDtypeStruct((n, d_model), F32), conv_shape],
        scratch_shapes=scratch,
        compiler_params=pltpu.CompilerParams(
            dimension_semantics=("parallel", "arbitrary"), vmem_limit_bytes=VMEM_LIMIT),
        name="finish_layer",
    )(x, oa, ob, c0, c1, *weights)


MIXER_TILE = 512
HGRN_TILE = 1024
HGRN_STEP = 16
FINISH_TILE = 256
PAGES_PER_STEP = 8
NEW_KEY_PAD = 16


def _rope_tables(pos):
    half = HEAD_DIM_A // 2
    inv = ROPE_THETA ** (-jnp.arange(half, dtype=F32) * 2.0 / HEAD_DIM_A)
    ang = pos.astype(F32)[:, None] * inv[None, :]
    cos = jnp.cos(ang)
    sin = jnp.sin(ang)
    reps = LANES // HEAD_DIM_A
    return (jnp.tile(jnp.concatenate([cos, cos], axis=1), (1, reps)),
            jnp.tile(jnp.concatenate([-sin, sin], axis=1), (1, reps)))


def kernel(x_prompt, x_sample, cache_k, cache_v, state_hgrn, state_conv, page_table, w_in, lambda_q1, lambda_k1, lambda_q2, lambda_k2, subln_g, lb_logits, hgrn_norm_g, w_branch_a, w_branch_b, w_out, ln1_g, ln1_b, w_up, conv_w, conv_b, w_down, ln2_g, ln2_b):
    bp, tp, d_model = x_prompt.shape
    bs, ts, _ = x_sample.shape
    depth = w_in.shape[0]
    n_phys, _, page, _, _, _ = cache_k.shape
    past_len = page_table.shape[1] * page
    d_ff = w_down.shape[1]
    alpha = (2 * depth) ** 0.25
    n_mix = 6 * MIX_W
    ts_pad = 8
    lb_logits = lb_logits.astype(F32)

    cos_p, sin_p = _rope_tables(jnp.arange(tp))
    cos_s, sin_s = _rope_tables(past_len + jnp.arange(bs * ts) % ts)
    cache_kt = jnp.transpose(cache_k, (0, 1, 3, 4, 5, 2)).reshape(n_phys, depth, MIX_W, page)
    cache_vr = cache_v.reshape(n_phys, depth, page * N_HEADS_A, HEAD_W)

    xp = x_prompt.reshape(bp * tp, d_model)
    xs = x_sample.reshape(bs * ts, d_model)
    outs = {name: [] for name in ("kp", "vp", "sp", "cp", "ks", "vs", "ss", "cs")}
    for l in range(depth):
        lam_init = 0.8 - 0.6 * math.exp(-0.3 * l)
        w_mix = w_in[l, :, :n_mix].astype(BF16)
        lam_vecs = jnp.stack([lambda_q1[l], lambda_k1[l], lambda_q2[l], lambda_k2[l]]).astype(F32)
        g_row = subln_g[l].reshape(1, HEAD_W).astype(F32)
        weights = (
            w_in[l, :, n_mix:].astype(BF16),
            hgrn_norm_g[l].reshape(1, KEY_DIM_B).astype(F32),
            w_branch_a[l].astype(BF16), w_branch_b[l].astype(BF16), w_out[l].astype(BF16),
            ln1_g[l].reshape(1, d_model), ln1_b[l].reshape(1, d_model),
            w_up[l].astype(BF16), conv_w[l], conv_b[l].reshape(1, d_ff),
            w_down[l].astype(BF16), ln2_g[l].reshape(1, d_model), ln2_b[l].reshape(1, d_model))

        k_p, v_p, qt3, k_bf, vt3, qb, kk, lf, ib = _mixer_proj(
            xp, w_mix, cos_p, sin_p, lb_logits, l, True, MIXER_TILE)
        oa = _prompt_attention(qt3, k_bf, vt3, lam_vecs, g_row.reshape(HEAD_W, 1), lam_init,
                               bp, tp, MIXER_TILE)
        s0 = jnp.zeros((bp, N_HEADS_B, KEY_DIM_B, KEY_DIM_B), F32)
        ob, s_p = _hgrn_scan(qb, kk, lf, ib, s0, bp, tp, HGRN_TILE, HGRN_STEP)
        conv0 = jnp.zeros((bp, CONV_W - 1, d_ff), F32)
        xp_new, conv_p = _finish(xp, oa, ob, conv0, conv0, weights, alpha,
                                 bp, tp // FINISH_TILE, FINISH_TILE, None)
        outs["kp"].append(k_p.reshape(bp, tp, N_HEADS_A, 2, HEAD_DIM_A))
        outs["vp"].append(v_p.reshape(bp, tp, N_HEADS_A, HEAD_W))
        outs["sp"].append(s_p)
        outs["cp"].append(conv_p)

        k_s, v_s, q_s, qb, kk, lf, ib = _mixer_proj(
            xs, w_mix, cos_s, sin_s, lb_logits, l, False, bs * ts)
        pad_t = lambda a, rows: jnp.pad(a.reshape(bs, ts, MIX_W), ((0, 0), (0, rows - ts), (0, 0)))
        q_rep = jnp.tile(q_s.reshape(bs, ts, MIX_W), (1, 2 * N_HEADS_A, 1))
        oa = _sample_attention(page_table, q_rep, pad_t(k_s, NEW_KEY_PAD), pad_t(v_s, NEW_KEY_PAD),
                               cache_kt, cache_vr, lam_vecs, g_row, lam_init, l, ts, PAGES_PER_STEP)
        flat = lambda a: pad_t(a, ts_pad).reshape(bs * ts_pad, MIX_W)
        ob, s_s = _hgrn_scan(flat(qb), flat(kk), flat(lf), flat(ib), state_hgrn[:, l],
                             bs, ts_pad, ts_pad, ts_pad)
        ob = ob.reshape(bs, ts_pad, MIX_W)[:, :ts].reshape(bs * ts, MIX_W)
        st = state_conv[:, l]
        c0 = jnp.concatenate([st, jnp.zeros((bs, ts - 2, d_ff), F32)], axis=1)
        c1 = jnp.concatenate([st[:, 1:], jnp.zeros((bs, ts - 1, d_ff), F32)], axis=1)
        xs_new, a_s = _finish(xs, oa.reshape(bs * ts, MIX_W), ob,
                              c0.reshape(bs * ts, d_ff), c1.reshape(bs * ts, d_ff),
                              weights, alpha, 1, 1, bs * ts, ts)
        conv_s = jnp.concatenate([st, a_s.reshape(bs, ts, d_ff)], axis=1)[:, -(CONV_W - 1):]
        outs["ks"].append(k_s.reshape(bs, ts, N_HEADS_A, 2, HEAD_DIM_A))
        outs["vs"].append(v_s.reshape(bs, ts, N_HEADS_A, HEAD_W))
        outs["ss"].append(s_s)
        outs["cs"].append(conv_s)
        xp, xs = xp_new, xs_new

    stack = lambda name: jnp.stack(outs[name], axis=1)
    return (xp.reshape(bp, tp, d_model), xs.reshape(bs, ts, d_model),
            stack("kp"), stack("vp"), stack("sp"), stack("cp"),
            stack("ks"), stack("vs"), stack("ss"), stack("cs"))
```

```python
import functools
import math

import jax
import jax.numpy as jnp
from jax import lax
from jax.experimental import pallas as pl
from jax.experimental.pallas import tpu as pltpu

F32 = jnp.float32
BF16 = jnp.bfloat16

N_HEADS_A = 4
HEAD_DIM_A = 64
N_HEADS_B = 4
KEY_DIM_B = 128
CONV_W = 3
ROPE_THETA = 10000.0
LN_EPS = 1e-5
RMS_EPS = 1e-5

LANES = 128
HEAD_W = 2 * HEAD_DIM_A
MIX_W = N_HEADS_A * HEAD_W
NEG_BIG = -1e30
SUM_ROWS = 16

VMEM_LIMIT = 56 * 1024 * 1024


def _dot(a, b):
    return jnp.dot(a, b, preferred_element_type=F32)


def _dot_nt(a, b):
    return lax.dot_general(a, b, (((1,), (1,)), ((), ())), preferred_element_type=F32)


def _dot_tn(a, b):
    return lax.dot_general(a, b, (((0,), (0,)), ((), ())), preferred_element_type=F32)


def _layer_norm(x, g, b):
    mu = jnp.mean(x, axis=-1, keepdims=True)
    xc = x - mu
    var = jnp.mean(xc * xc, axis=-1, keepdims=True)
    return xc * lax.rsqrt(var + LN_EPS) * g + b


def _lambda(lam_ref, lam_init):
    lv = lam_ref[...]
    e1 = jnp.exp(jnp.sum(lv[0:1] * lv[1:2], axis=-1, keepdims=True))
    e2 = jnp.exp(jnp.sum(lv[2:3] * lv[3:4], axis=-1, keepdims=True))
    return e1 - e2 + lam_init


def _rope(x, cos, sin_signed):
    lane = lax.broadcasted_iota(jnp.int32, x.shape, 1)
    first_half = (lane % HEAD_DIM_A) < (HEAD_DIM_A // 2)
    partner = jnp.where(first_half, pltpu.roll(x, LANES - HEAD_DIM_A // 2, 1),
                        pltpu.roll(x, HEAD_DIM_A // 2, 1))
    return x * cos + partner * sin_signed


def _mixer_kernel(layer, transposed, x_ref, w_ref, cos_ref, sin_ref, lbl_ref, *out_refs):
    if transposed:
        k_ref, v_ref, qt_ref, kb_ref, vt_ref, qb_ref, kk_ref, lf_ref, ib_ref = out_refs
    else:
        k_ref, v_ref, q_ref, qb_ref, kk_ref, lf_ref, ib_ref = out_refs
    xb = x_ref[...].astype(BF16)
    cos = cos_ref[...]
    sin = sin_ref[...]

    def section(i):
        return _dot(xb, w_ref[:, i * MIX_W:(i + 1) * MIX_W])

    scale = HEAD_DIM_A ** -0.5 * math.log2(math.e)
    qa = section(0)
    qa = jnp.concatenate([_rope(qa[:, h * HEAD_W:(h + 1) * HEAD_W], cos, sin) * scale
                          for h in range(N_HEADS_A)], axis=1)
    ka = section(1)
    ka = jnp.concatenate([_rope(ka[:, h * HEAD_W:(h + 1) * HEAD_W], cos, sin)
                          for h in range(N_HEADS_A)], axis=1)
    va = section(2)
    if transposed:
        k_ref[...] = ka.T
        for h in range(N_HEADS_A):
            v_ref[pl.ds(h, va.shape[0], stride=N_HEADS_A), :] = va[:, h * HEAD_W:(h + 1) * HEAD_W]
        qt_ref[...] = qa.T.astype(BF16)
        kb_ref[...] = ka.astype(BF16)
        vt_ref[...] = va.T.astype(BF16)
    else:
        k_ref[...] = ka
        v_ref[...] = va
        q_ref[...] = qa

    qb_ref[...] = section(3)
    fr = section(4)
    ib_ref[...] = section(5)

    lg = lbl_ref[...]
    e = jnp.exp(lg - jnp.max(lg, axis=0, keepdims=True))
    lb = jnp.sum(e[:layer + 1], axis=0, keepdims=True) / jnp.sum(e, axis=0, keepdims=True)
    log_lb = jnp.log(lb)
    log_1mlb = jnp.log1p(-lb)
    log_sig = jnp.minimum(fr, 0.0) - jnp.log1p(jnp.exp(-jnp.abs(fr)))
    t = log_1mlb + log_sig
    hi = jnp.maximum(log_lb, t)
    lo = jnp.minimum(log_lb, t)
    lf_ref[...] = hi + jnp.log1p(jnp.exp(lo - hi))
    kk_ref[...] = (1.0 - lb) * (1.0 / (1.0 + jnp.exp(fr)))


def _mixer_proj(x, w_mix, cos, sin, lb_logits, layer, transposed, tm):
    n = x.shape[0]
    d = x.shape[1]
    n_tiles = n // tm
    pos_tiles = cos.shape[0] // tm
    row = lambda i: (i, 0)
    f32_out = jax.ShapeDtypeStruct((n, MIX_W), F32)
    row_spec = pl.BlockSpec((tm, MIX_W), row)
    if transposed:
        t_shape = jax.ShapeDtypeStruct((n_tiles, MIX_W, tm), BF16)
        t_spec = pl.BlockSpec((None, MIX_W, tm), lambda i: (i, 0, 0))
        kt_shape = jax.ShapeDtypeStruct((n // cos.shape[0], MIX_W, cos.shape[0]), F32)
        kt_spec = pl.BlockSpec((None, MIX_W, tm), lambda i: (i // pos_tiles, 0, i % pos_tiles))
        v_shape = jax.ShapeDtypeStruct((n * N_HEADS_A, HEAD_W), F32)
        v_spec = pl.BlockSpec((tm * N_HEADS_A, HEAD_W), row)
        out_shape = [kt_shape, v_shape, t_shape, jax.ShapeDtypeStruct((n, MIX_W), BF16), t_shape,
                     f32_out, f32_out, f32_out, f32_out]
        out_specs = [kt_spec, v_spec, t_spec, row_spec, t_spec,
                     row_spec, row_spec, row_spec, row_spec]
    else:
        out_shape = [f32_out] * 7
        out_specs = [row_spec] * 7
    return pl.pallas_call(
        functools.partial(_mixer_kernel, layer, transposed),
        grid=(n_tiles,),
        in_specs=[
            pl.BlockSpec((tm, d), row),
            pl.BlockSpec(w_mix.shape, lambda i: (0, 0)),
            pl.BlockSpec((tm, LANES), lambda i: (i % pos_tiles, 0)),
            pl.BlockSpec((tm, LANES), lambda i: (i % pos_tiles, 0)),
            pl.BlockSpec(lb_logits.shape, lambda i: (0, 0)),
        ],
        out_specs=out_specs,
        out_shape=out_shape,
        compiler_params=pltpu.CompilerParams(
            dimension_semantics=("parallel",), vmem_limit_bytes=VMEM_LIMIT),
        name="mixer_proj",
    )(x, w_mix, cos, sin, lb_logits)


def _prompt_attn_kernel(lam_init, tq, qt_ref, k_ref, vt_ref, lam_ref, g_ref, o_ref,
                        m_ref, acc_ref, sa_ref, sb_ref):
    qi = pl.program_id(2)
    qt = qt_ref[...]
    qrow = lax.broadcasted_iota(jnp.int32, qt.shape, 0)
    zero = jnp.zeros_like(qt)
    qmaps = (jnp.where(qrow < HEAD_DIM_A, qt, zero), jnp.where(qrow >= HEAD_DIM_A, qt, zero))

    m_ref[...] = jnp.full(m_ref.shape, NEG_BIG, F32)
    acc_ref[...] = jnp.zeros(acc_ref.shape, F32)
    ones = jnp.ones((SUM_ROWS, tq), BF16)

    def scores(j, s_ref):
        kj = k_ref[pl.ds(pl.multiple_of(j * tq, tq), tq), :]
        for mp in range(2):
            s_ref[mp] = _dot(kj, qmaps[mp])

    def consume(j, s_ref, masked):
        vj = jnp.concatenate([vt_ref[j], ones], axis=0)
        for mp in range(2):
            s = s_ref[mp]
            if masked:
                krow = lax.broadcasted_iota(jnp.int32, s.shape, 0)
                qcol = lax.broadcasted_iota(jnp.int32, s.shape, 1)
                s = jnp.where(krow <= qcol, s, NEG_BIG)
            m_old = m_ref[mp]
            m_new = jnp.maximum(m_old, jnp.max(s, axis=0, keepdims=True))
            alpha = jnp.exp2(m_old - m_new)
            p = jnp.exp2(s - m_new).astype(BF16)
            acc_ref[mp] = alpha * acc_ref[mp] + _dot(vj, p)
            m_ref[mp] = m_new

    def body(jj, carry):
        j = 2 * jj
        scores(j + 1, sb_ref)
        consume(j, sa_ref, False)
        scores(j + 2, sa_ref)
        consume(j + 1, sb_ref, False)
        return carry

    scores(0, sa_ref)
    lax.fori_loop(0, qi // 2, body, 0)

    @pl.when(qi % 2 == 0)
    def _():
        consume(qi, sa_ref, True)

    @pl.when(qi % 2 == 1)
    def _():
        scores(qi, sb_ref)
        consume(qi - 1, sa_ref, False)
        consume(qi, sb_ref, True)

    lam = _lambda(lam_ref, lam_init)
    acc1 = acc_ref[0]
    acc2 = acc_ref[1]
    o = (acc1[:HEAD_W] / acc1[HEAD_W:HEAD_W + 1]
         - lam * (acc2[:HEAD_W] / acc2[HEAD_W:HEAD_W + 1]))
    ms = jnp.mean(o * o, axis=0, keepdims=True)
    o = o * lax.rsqrt(ms + RMS_EPS) * g_ref[...] * (1.0 - lam_init)
    o_ref[...] = o.T.astype(o_ref.dtype)


def _prompt_attention(qt3, k_bf, vt3, lam_vecs, g_col, lam_init, batch, seq, tq):
    nq = seq // tq
    n = batch * seq
    return pl.pallas_call(
        functools.partial(_prompt_attn_kernel, lam_init, tq),
        grid=(batch, N_HEADS_A, nq),
        in_specs=[
            pl.BlockSpec((None, HEAD_W, tq), lambda b, h, i: (b * nq + i, h, 0)),
            pl.BlockSpec((seq, HEAD_W), lambda b, h, i: (b, h)),
            pl.BlockSpec((nq, HEAD_W, tq), lambda b, h, i: (b, h, 0)),
            pl.BlockSpec(lam_vecs.shape, lambda b, h, i: (0, 0)),
            pl.BlockSpec(g_col.shape, lambda b, h, i: (0, 0)),
        ],
        out_specs=pl.BlockSpec((tq, HEAD_W), lambda b, h, i: (b * nq + i, h)),
        out_shape=jax.ShapeDtypeStruct((n, MIX_W), BF16),
        scratch_shapes=[
            pltpu.VMEM((2, 1, tq), F32),
            pltpu.VMEM((2, HEAD_W + SUM_ROWS, tq), F32),
            pltpu.VMEM((2, tq, tq), F32),
            pltpu.VMEM((2, tq, tq), F32),
        ],
        compiler_params=pltpu.CompilerParams(
            dimension_semantics=("parallel", "parallel", "arbitrary"),
            vmem_limit_bytes=VMEM_LIMIT),
        name="prompt_diff_attention",
    )(qt3, k_bf, vt3, lam_vecs, g_col)


def _sample_attn_kernel(lam_init, pages_per_step, dec_seq, pt_ref, q_ref, kn_ref, vn_ref,
                        lam_ref, g_ref, *rest):
    k_refs = rest[:pages_per_step]
    v_refs = rest[pages_per_step:2 * pages_per_step]
    o_ref = rest[2 * pages_per_step]
    m_ref, l_ref, acc_ref = rest[2 * pages_per_step + 1:]
    j = pl.program_id(1)
    n_rows = q_ref.shape[0]

    qrep = q_ref[...]
    rrow = lax.broadcasted_iota(jnp.int32, qrep.shape, 0)
    rcol = lax.broadcasted_iota(jnp.int32, qrep.shape, 1)
    qbd = jnp.where(rcol // HEAD_DIM_A == rrow // dec_seq, qrep, 0.0).astype(BF16)

    @pl.when(j == 0)
    def _():
        m_ref[...] = jnp.full(m_ref.shape, NEG_BIG, F32)
        l_ref[...] = jnp.zeros(l_ref.shape, F32)
        acc_ref[...] = jnp.zeros(acc_ref.shape, F32)

    def update(s, v_bf):
        m_old = m_ref[...]
        m_new = jnp.maximum(m_old, jnp.max(s, axis=-1, keepdims=True))
        alpha = jnp.exp2(m_old - m_new)
        p = jnp.exp2(s - m_new)
        l_ref[...] = alpha * l_ref[...] + jnp.sum(p, axis=-1, keepdims=True)
        acc_ref[...] = alpha * acc_ref[...] + _dot(p.astype(BF16), v_bf)
        m_ref[...] = m_new

    page = k_refs[0].shape[1]
    kt = jnp.concatenate([r[...] for r in k_refs], axis=1).astype(BF16)
    vp = jnp.concatenate(
        [jnp.concatenate([r[pl.ds(h, page, stride=N_HEADS_A), :] for h in range(N_HEADS_A)], axis=1)
         for r in v_refs], axis=0).astype(BF16)
    update(_dot(qbd, kt), vp)

    @pl.when(j == pl.num_programs(1) - 1)
    def _():
        kn = kn_ref[...].astype(BF16)
        vn = vn_ref[...].astype(BF16)
        s = _dot_nt(qbd, kn)
        key = lax.broadcasted_iota(jnp.int32, s.shape, 1)
        tok = lax.broadcasted_iota(jnp.int32, s.shape, 0) % dec_seq
        s = jnp.where(key <= tok, s, NEG_BIG)
        update(s, vn)
        lam = _lambda(lam_ref, lam_init)
        acc = acc_ref[...] / l_ref[...]
        rows_per_head = 2 * dec_seq
        outs = []
        for h in range(N_HEADS_A):
            blk = acc[h * rows_per_head:(h + 1) * rows_per_head, h * HEAD_W:(h + 1) * HEAD_W]
            o = blk[:dec_seq] - lam * blk[dec_seq:]
            ms = jnp.mean(o * o, axis=-1, keepdims=True)
            outs.append(o * lax.rsqrt(ms + RMS_EPS) * g_ref[...] * (1.0 - lam_init))
        o_ref[...] = jnp.concatenate(outs, axis=1).astype(o_ref.dtype)


def _sample_attention(page_table, q_rep, k_new, v_new, cache_kt, cache_v, lam_vecs, g_row,
                      lam_init, layer, dec_seq, pages_per_step):
    bs, n_pages = page_table.shape
    page = cache_kt.shape[3]
    n_rows = q_rep.shape[1]
    steps = n_pages // pages_per_step

    def k_spec(pg):
        return pl.BlockSpec((None, None, MIX_W, page),
                            lambda b, j, pt: (pt[b, j * pages_per_step + pg], layer, 0, 0))

    def v_spec(pg):
        return pl.BlockSpec((None, None, page * N_HEADS_A, HEAD_W),
                            lambda b, j, pt: (pt[b, j * pages_per_step + pg], layer, 0, 0))

    per_b = lambda b, j, pt: (b, 0, 0)
    const = lambda b, j, pt: (0, 0)
    grid_spec = pltpu.PrefetchScalarGridSpec(
        num_scalar_prefetch=1,
        grid=(bs, steps),
        in_specs=[
            pl.BlockSpec((None, n_rows, MIX_W), per_b),
            pl.BlockSpec((None, k_new.shape[1], MIX_W), per_b),
            pl.BlockSpec((None, k_new.shape[1], MIX_W), per_b),
            pl.BlockSpec(lam_vecs.shape, const),
            pl.BlockSpec(g_row.shape, const),
        ] + [k_spec(pg) for pg in range(pages_per_step)]
          + [v_spec(pg) for pg in range(pages_per_step)],
        out_specs=pl.BlockSpec((None, dec_seq, MIX_W), per_b),
        scratch_shapes=[
            pltpu.VMEM((n_rows, 1), F32),
            pltpu.VMEM((n_rows, 1), F32),
            pltpu.VMEM((n_rows, MIX_W), F32),
        ],
    )
    return pl.pallas_call(
        functools.partial(_sample_attn_kernel, lam_init, pages_per_step, dec_seq),
        grid_spec=grid_spec,
        out_shape=jax.ShapeDtypeStruct((bs, dec_seq, MIX_W), BF16),
        compiler_params=pltpu.CompilerParams(
            dimension_semantics=("parallel", "arbitrary"), vmem_limit_bytes=VMEM_LIMIT),
        name="sample_diff_attention",
    )(page_table, q_rep, k_new, v_new, lam_vecs, g_row,
      *([cache_kt] * pages_per_step), *([cache_v] * pages_per_step))


def _hgrn_kernel(step, n_steps, q_ref, k_ref, lf_ref, v_ref, s0_ref, o_ref, sout_ref, st_ref):
    ti = pl.program_id(1)

    @pl.when(ti == 0)
    def _():
        for h in range(N_HEADS_B):
            st_ref[h] = s0_ref[h].T

    ones = jnp.ones((KEY_DIM_B, LANES), BF16)
    rowi = lax.broadcasted_iota(jnp.int32, (step, KEY_DIM_B), 0)

    def body(i, carry):
        r0 = pl.multiple_of(i * step, step)
        for h in range(N_HEADS_B):
            cols = slice(h * KEY_DIM_B, (h + 1) * KEY_DIM_B)
            q = q_ref[pl.ds(r0, step), cols]
            k = k_ref[pl.ds(r0, step), cols]
            v = v_ref[pl.ds(r0, step), cols]
            b = lf_ref[pl.ds(r0, step), cols]
            d = 1
            while d < step:
                b = b + jnp.where(rowi >= d, pltpu.roll(b, d, 0), 0.0)
                d *= 2
            st = st_ref[h]
            o = _dot_nt((q * jnp.exp(b)).astype(BF16), st.astype(BF16))
            xs = []
            for s in range(step):
                e = jnp.exp(jnp.minimum(b - b[s:s + 1], 0.0))
                xs.append(jnp.where(rowi >= s, q * k[s:s + 1] * e, 0.0))
            att = _dot(jnp.concatenate(xs, axis=0).astype(BF16), ones)
            for s in range(step):
                o = o + att[s * step:(s + 1) * step] * v[s:s + 1]
            o_ref[pl.ds(r0, step), cols] = o
            b_last = b[step - 1:step]
            kd = k * jnp.exp(b_last - b)
            st_ref[h] = st * jnp.exp(b_last) + _dot_tn(v.astype(BF16), kd.astype(BF16))
        return carry

    lax.fori_loop(0, n_steps, body, 0)

    @pl.when(ti == pl.num_programs(1) - 1)
    def _():
        for h in range(N_HEADS_B):
            sout_ref[h] = st_ref[h].T


def _hgrn_scan(q, k, lf, v, s0, n_seq, seq_rows, tile_rows, step):
    n = q.shape[0]
    tiles = seq_rows // tile_rows
    row = lambda s, t: (s * tiles + t, 0)
    spec = pl.BlockSpec((tile_rows, MIX_W), row)
    st_spec = pl.BlockSpec((None, N_HEADS_B, KEY_DIM_B, KEY_DIM_B), lambda s, t: (s, 0, 0, 0))
    return pl.pallas_call(
        functools.partial(_hgrn_kernel, step, tile_rows // step),
        grid=(n_seq, tiles),
        in_specs=[spec, spec, spec, spec, st_spec],
        out_specs=[spec, st_spec],
        out_shape=[jax.ShapeDtypeStruct((n, MIX_W), F32),
                   jax.ShapeDtypeStruct(s0.shape, F32)],
        scratch_shapes=[pltpu.VMEM((N_HEADS_B, KEY_DIM_B, KEY_DIM_B), F32)],
        compiler_params=pltpu.CompilerParams(
            dimension_semantics=("parallel", "arbitrary"), vmem_limit_bytes=VMEM_LIMIT),
        name="hgrn2_scan",
    )(q, k, lf, v, s0)


def _finish_kernel(alpha, d_ff, seq_rows, x_ref, oa_ref, ob_ref, c0_ref, c1_ref,
                   wg_ref, ng_ref, wa_ref, wb_ref, wo_ref, l1g_ref, l1b_ref, wu_ref,
                   cw_ref, cb_ref, wd_ref, l2g_ref, l2b_ref, out_ref, conv_ref, *scratch):
    tm = x_ref.shape[0]
    x = x_ref[...]
    gates = _dot(x.astype(BF16), wg_ref[...])
    d_model = x.shape[1]
    gb = gates[:, :MIX_W]
    ga = gates[:, MIX_W:MIX_W + d_model]
    gg = gates[:, MIX_W + d_model:]

    ob = ob_ref[...]
    normed = []
    for h in range(N_HEADS_B):
        oh = ob[:, h * KEY_DIM_B:(h + 1) * KEY_DIM_B]
        ms = jnp.mean(oh * oh, axis=-1, keepdims=True)
        normed.append(oh * lax.rsqrt(ms + RMS_EPS) * ng_ref[...])
    obn = jnp.concatenate(normed, axis=1) * (gb * jax.nn.sigmoid(gb))

    merged = (jax.nn.sigmoid(ga) * _dot(oa_ref[...], wa_ref[...])
              + jax.nn.sigmoid(gg) * _dot(obn.astype(BF16), wb_ref[...]))
    h1 = _layer_norm(alpha * x + _dot(merged.astype(BF16), wo_ref[...]), l1g_ref[...], l1b_ref[...])

    up = _dot(h1.astype(BF16), wu_ref[...])
    a = up[:, :d_ff]
    g = up[:, d_ff:]
    cw = cw_ref[...]
    if seq_rows is None:
        (a_ref,) = scratch
        ti = pl.program_id(1)

        @pl.when(ti == 0)
        def _():
            a_ref[6:8, :] = c0_ref[...]

        a_ref[8:8 + tm, :] = a
        a1 = a_ref[7:7 + tm, :]
        a2 = a_ref[6:6 + tm, :]
        tail = a_ref[6 + tm:8 + tm, :]
        a_ref[6:8, :] = tail
        conv_ref[...] = tail
    else:
        pos = lax.broadcasted_iota(jnp.int32, a.shape, 0) % seq_rows
        a1 = jnp.where(pos >= 1, pltpu.roll(a, 1, 0), 0.0) + c1_ref[...]
        a2 = jnp.where(pos >= 2, pltpu.roll(a, 2, 0), 0.0) + c0_ref[...]
        conv_ref[...] = a
    c = cb_ref[...] + a2 * cw[0:1] + a1 * cw[1:2] + a * cw[2:3]
    act = 0.5 * c * (1.0 + lax.erf(c * (2.0 ** -0.5))) * g
    y = _dot(act.astype(BF16), wd_ref[...])
    out_ref[...] = _layer_norm(alpha * h1 + y, l2g_ref[...], l2b_ref[...])


def _finish(x, oa, ob, c0, c1, weights, alpha, n_seq, seq_tiles, tm, seq_rows):
    n, d_model = x.shape
    d_ff = weights[-3].shape[0]
    row = lambda s, t: (s * seq_tiles + t, 0)
    const = lambda s, t: (0, 0)
    once = pl.Buffered(1)

    def wspec(w):
        return pl.BlockSpec(w.shape, const, pipeline_mode=once)

    if seq_rows is None:
        c_specs = [pl.BlockSpec((None, CONV_W - 1, d_ff), lambda s, t: (s, 0, 0))] * 2
        conv_shape = jax.ShapeDtypeStruct((n_seq, CONV_W - 1, d_ff), F32)
        conv_spec = pl.BlockSpec((None, CONV_W - 1, d_ff), lambda s, t: (s, 0, 0))
        scratch = [pltpu.VMEM((tm + 8, d_ff), F32)]
    else:
        c_specs = [pl.BlockSpec((tm, d_ff), row)] * 2
        conv_shape = jax.ShapeDtypeStruct((n, d_ff), F32)
        conv_spec = pl.BlockSpec((tm, d_ff), row)
        scratch = []
    return pl.pallas_call(
        functools.partial(_finish_kernel, alpha, d_ff, seq_rows),
        grid=(n_seq, seq_tiles),
        in_specs=[pl.BlockSpec((tm, d_model), row),
                  pl.BlockSpec((tm, MIX_W), row),
                  pl.BlockSpec((tm, MIX_W), row)] + c_specs + [wspec(w) for w in weights],
        out_specs=[pl.BlockSpec((tm, d_model), row), conv_spec],
        out_shape=[jax.ShapeDtypeStruct((n, d_model), F32), conv_shape],
        scratch_shapes=scratch,
        compiler_params=pltpu.CompilerParams(
            dimension_semantics=("parallel", "arbitrary"), vmem_limit_bytes=VMEM_LIMIT),
        name="finish_layer",
    )(x, oa, ob, c0, c1, *weights)


MIXER_TILE = 512
HGRN_TILE = 1024
HGRN_STEP = 16
FINISH_TILE = 256
PAGES_PER_STEP = 8
NEW_KEY_PAD = 16


def _rope_tables(pos):
    half = HEAD_DIM_A // 2
    inv = ROPE_THETA ** (-jnp.arange(half, dtype=F32) * 2.0 / HEAD_DIM_A)
    ang = pos.astype(F32)[:, None] * inv[None, :]
    cos = jnp.cos(ang)
    sin = jnp.sin(ang)
    reps = LANES // HEAD_DIM_A
    return (jnp.tile(jnp.concatenate([cos, cos], axis=1), (1, reps)),
            jnp.tile(jnp.concatenate([-sin, sin], axis=1), (1, reps)))


def kernel(x_prompt, x_sample, cache_k, cache_v, state_hgrn, state_conv, page_table, w_in, lambda_q1, lambda_k1, lambda_q2, lambda_k2, subln_g, lb_logits, hgrn_norm_g, w_branch_a, w_branch_b, w_out, ln1_g, ln1_b, w_up, conv_w, conv_b, w_down, ln2_g, ln2_b):
    bp, tp, d_model = x_prompt.shape
    bs, ts, _ = x_sample.shape
    depth = w_in.shape[0]
    n_phys, _, page, _, _, _ = cache_k.shape
    past_len = page_table.shape[1] * page
    d_ff = w_down.shape[1]
    alpha = (2 * depth) ** 0.25
    n_mix = 6 * MIX_W
    ts_pad = 8
    lb_logits = lb_logits.astype(F32)

    cos_p, sin_p = _rope_tables(jnp.arange(tp))
    cos_s, sin_s = _rope_tables(past_len + jnp.arange(bs * ts) % ts)
    cache_kt = jnp.transpose(cache_k, (0, 1, 3, 4, 5, 2)).reshape(n_phys, depth, MIX_W, page)
    cache_vr = cache_v.reshape(n_phys, depth, page * N_HEADS_A, HEAD_W)

    xp = x_prompt.reshape(bp * tp, d_model)
    xs = x_sample.reshape(bs * ts, d_model)
    outs = {name: [] for name in ("kp", "vp", "sp", "cp", "ks", "vs", "ss", "cs")}
    for l in range(depth):
        lam_init = 0.8 - 0.6 * math.exp(-0.3 * l)
        w_mix = w_in[l, :, :n_mix].astype(BF16)
        lam_vecs = jnp.stack([lambda_q1[l], lambda_k1[l], lambda_q2[l], lambda_k2[l]]).astype(F32)
        g_row = subln_g[l].reshape(1, HEAD_W).astype(F32)
        weights = (
            w_in[l, :, n_mix:].astype(BF16),
            hgrn_norm_g[l].reshape(1, KEY_DIM_B).astype(F32),
            w_branch_a[l].astype(BF16), w_branch_b[l].astype(BF16), w_out[l].astype(BF16),
            ln1_g[l].reshape(1, d_model), ln1_b[l].reshape(1, d_model),
            w_up[l].astype(BF16), conv_w[l], conv_b[l].reshape(1, d_ff),
            w_down[l].astype(BF16), ln2_g[l].reshape(1, d_model), ln2_b[l].reshape(1, d_model))

        k_p, v_p, qt3, k_bf, vt3, qb, kk, lf, ib = _mixer_proj(
            xp, w_mix, cos_p, sin_p, lb_logits, l, True, MIXER_TILE)
        oa = _prompt_attention(qt3, k_bf, vt3, lam_vecs, g_row.reshape(HEAD_W, 1), lam_init,
                               bp, tp, MIXER_TILE)
        s0 = jnp.zeros((bp, N_HEADS_B, KEY_DIM_B, KEY_DIM_B), F32)
        ob, s_p = _hgrn_scan(qb, kk, lf, ib, s0, bp, tp, HGRN_TILE, HGRN_STEP)
        conv0 = jnp.zeros((bp, CONV_W - 1, d_ff), F32)
        xp_new, conv_p = _finish(xp, oa, ob, conv0, conv0, weights, alpha,
                                 bp, tp // FINISH_TILE, FINISH_TILE, None)
        outs["kp"].append(jnp.transpose(k_p.reshape(bp, N_HEADS_A, 2, HEAD_DIM_A, tp), (0, 4, 1, 2, 3)))
        outs["vp"].append(v_p.reshape(bp, tp, N_HEADS_A, HEAD_W))
        outs["sp"].append(s_p)
        outs["cp"].append(conv_p)

        k_s, v_s, q_s, qb, kk, lf, ib = _mixer_proj(
            xs, w_mix, cos_s, sin_s, lb_logits, l, False, bs * ts)
        pad_t = lambda a, rows: jnp.pad(a.reshape(bs, ts, MIX_W), ((0, 0), (0, rows - ts), (0, 0)))
        q_rep = jnp.tile(q_s.reshape(bs, ts, MIX_W), (1, 2 * N_HEADS_A, 1))
        oa = _sample_attention(page_table, q_rep, pad_t(k_s, NEW_KEY_PAD), pad_t(v_s, NEW_KEY_PAD),
                               cache_kt, cache_vr, lam_vecs, g_row, lam_init, l, ts, PAGES_PER_STEP)
        flat = lambda a: pad_t(a, ts_pad).reshape(bs * ts_pad, MIX_W)
        ob, s_s = _hgrn_scan(flat(qb), flat(kk), flat(lf), flat(ib), state_hgrn[:, l],
                             bs, ts_pad, ts_pad, ts_pad)
        ob = ob.reshape(bs, ts_pad, MIX_W)[:, :ts].reshape(bs * ts, MIX_W)
        st = state_conv[:, l]
        c0 = jnp.concatenate([st, jnp.zeros((bs, ts - 2, d_ff), F32)], axis=1)
        c1 = jnp.concatenate([st[:, 1:], jnp.zeros((bs, ts - 1, d_ff), F32)], axis=1)
        xs_new, a_s = _finish(xs, oa.reshape(bs * ts, MIX_W), ob,
                              c0.reshape(bs * ts, d_ff), c1.reshape(bs * ts, d_ff),
                              weights, alpha, 1, 1, bs * ts, ts)
        conv_s = jnp.concatenate([st, a_s.reshape(bs, ts, d_ff)], axis=1)[:, -(CONV_W - 1):]
        outs["ks"].append(k_s.reshape(bs, ts, N_HEADS_A, 2, HEAD_DIM_A))
        outs["vs"].append(v_s.reshape(bs, ts, N_HEADS_A, HEAD_W))
        outs["ss"].append(s_s)
        outs["cs"].append(conv_s)
        xp, xs = xp_new, xs_new

    stack = lambda name: jnp.stack(outs[name], axis=1)
    return (xp.reshape(bp, tp, d_model), xs.reshape(bs, ts, d_model),
            stack("kp"), stack("vp"), stack("sp"), stack("cp"),
            stack("ks"), stack("vs"), stack("ss"), stack("cs"))
```

```python
import functools
import math

import jax
import jax.numpy as jnp
from jax import lax
from jax.experimental import pallas as pl
from jax.experimental.pallas import tpu as pltpu

F32 = jnp.float32
BF16 = jnp.bfloat16

N_HEADS_A = 4
HEAD_DIM_A = 64
N_HEADS_B = 4
KEY_DIM_B = 128
CONV_W = 3
ROPE_THETA = 10000.0
LN_EPS = 1e-5
RMS_EPS = 1e-5

LANES = 128
HEAD_W = 2 * HEAD_DIM_A
MIX_W = N_HEADS_A * HEAD_W
NEG_BIG = -1e30
SUM_ROWS = 16

VMEM_LIMIT = 56 * 1024 * 1024


def _dot(a, b):
    return jnp.dot(a, b, preferred_element_type=F32)


def _dot_nt(a, b):
    return lax.dot_general(a, b, (((1,), (1,)), ((), ())), preferred_element_type=F32)


def _dot_tn(a, b):
    return lax.dot_general(a, b, (((0,), (0,)), ((), ())), preferred_element_type=F32)


def _layer_norm(x, g, b):
    mu = jnp.mean(x, axis=-1, keepdims=True)
    xc = x - mu
    var = jnp.mean(xc * xc, axis=-1, keepdims=True)
    return xc * lax.rsqrt(var + LN_EPS) * g + b


def _lambda(lam_ref, lam_init):
    lv = lam_ref[...]
    e1 = jnp.exp(jnp.sum(lv[0:1] * lv[1:2], axis=-1, keepdims=True))
    e2 = jnp.exp(jnp.sum(lv[2:3] * lv[3:4], axis=-1, keepdims=True))
    return e1 - e2 + lam_init


def _rope(x, cos, sin_signed):
    lane = lax.broadcasted_iota(jnp.int32, x.shape, 1)
    first_half = (lane % HEAD_DIM_A) < (HEAD_DIM_A // 2)
    partner = jnp.where(first_half, pltpu.roll(x, LANES - HEAD_DIM_A // 2, 1),
                        pltpu.roll(x, HEAD_DIM_A // 2, 1))
    return x * cos + partner * sin_signed


def _mixer_kernel(layer, transposed, x_ref, w_ref, cos_ref, sin_ref, lbl_ref, *out_refs):
    if transposed:
        k_ref, v_ref, qt_ref, kb_ref, vt_ref, qb_ref, kk_ref, lf_ref, ib_ref = out_refs
    else:
        k_ref, v_ref, q_ref, qb_ref, kk_ref, lf_ref, ib_ref = out_refs
    xb = x_ref[...].astype(BF16)
    cos = cos_ref[...]
    sin = sin_ref[...]

    def section(i):
        return _dot(xb, w_ref[:, i * MIX_W:(i + 1) * MIX_W])

    scale = HEAD_DIM_A ** -0.5 * math.log2(math.e)
    qa = section(0)
    qa = jnp.concatenate([_rope(qa[:, h * HEAD_W:(h + 1) * HEAD_W], cos, sin) * scale
                          for h in range(N_HEADS_A)], axis=1)
    ka = section(1)
    ka = jnp.concatenate([_rope(ka[:, h * HEAD_W:(h + 1) * HEAD_W], cos, sin)
                          for h in range(N_HEADS_A)], axis=1)
    va = section(2)
    if transposed:
        k_ref[...] = ka.T
        for h in range(N_HEADS_A):
            v_ref[pl.ds(h, va.shape[0], stride=N_HEADS_A), :] = va[:, h * HEAD_W:(h + 1) * HEAD_W]
        qt_ref[...] = qa.T.astype(BF16)
        kb_ref[...] = ka.astype(BF16)
        vt_ref[...] = va.T.astype(BF16)
    else:
        k_ref[...] = ka
        v_ref[...] = va
        q_ref[...] = qa

    qb_ref[...] = section(3)
    fr = section(4)
    ib_ref[...] = section(5)

    lg = lbl_ref[...]
    e = jnp.exp(lg - jnp.max(lg, axis=0, keepdims=True))
    lb = jnp.sum(e[:layer + 1], axis=0, keepdims=True) / jnp.sum(e, axis=0, keepdims=True)
    log_lb = jnp.log(lb)
    log_1mlb = jnp.log1p(-lb)
    log_sig = jnp.minimum(fr, 0.0) - jnp.log1p(jnp.exp(-jnp.abs(fr)))
    t = log_1mlb + log_sig
    hi = jnp.maximum(log_lb, t)
    lo = jnp.minimum(log_lb, t)
    lf_ref[...] = (hi + jnp.log1p(jnp.exp(lo - hi))) * math.log2(math.e)
    kk_ref[...] = (1.0 - lb) * (1.0 / (1.0 + jnp.exp(fr)))


def _mixer_proj(x, w_mix, cos, sin, lb_logits, layer, transposed, tm):
    n = x.shape[0]
    d = x.shape[1]
    n_tiles = n // tm
    pos_tiles = cos.shape[0] // tm
    row = lambda i: (i, 0)
    f32_out = jax.ShapeDtypeStruct((n, MIX_W), F32)
    row_spec = pl.BlockSpec((tm, MIX_W), row)
    if transposed:
        t_shape = jax.ShapeDtypeStruct((n_tiles, MIX_W, tm), BF16)
        t_spec = pl.BlockSpec((None, MIX_W, tm), lambda i: (i, 0, 0))
        kt_shape = jax.ShapeDtypeStruct((n // cos.shape[0], MIX_W, cos.shape[0]), F32)
        kt_spec = pl.BlockSpec((None, MIX_W, tm), lambda i: (i // pos_tiles, 0, i % pos_tiles))
        v_shape = jax.ShapeDtypeStruct((n * N_HEADS_A, HEAD_W), F32)
        v_spec = pl.BlockSpec((tm * N_HEADS_A, HEAD_W), row)
        out_shape = [kt_shape, v_shape, t_shape, jax.ShapeDtypeStruct((n, MIX_W), BF16), t_shape,
                     f32_out, f32_out, f32_out, f32_out]
        out_specs = [kt_spec, v_spec, t_spec, row_spec, t_spec,
                     row_spec, row_spec, row_spec, row_spec]
    else:
        out_shape = [f32_out] * 7
        out_specs = [row_spec] * 7
    return pl.pallas_call(
        functools.partial(_mixer_kernel, layer, transposed),
        grid=(n_tiles,),
        in_specs=[
            pl.BlockSpec((tm, d), row),
            pl.BlockSpec(w_mix.shape, lambda i: (0, 0)),
            pl.BlockSpec((tm, LANES), lambda i: (i % pos_tiles, 0)),
            pl.BlockSpec((tm, LANES), lambda i: (i % pos_tiles, 0)),
            pl.BlockSpec(lb_logits.shape, lambda i: (0, 0)),
        ],
        out_specs=out_specs,
        out_shape=out_shape,
        compiler_params=pltpu.CompilerParams(
            dimension_semantics=("parallel",), vmem_limit_bytes=VMEM_LIMIT),
        name="mixer_proj",
    )(x, w_mix, cos, sin, lb_logits)


def _prompt_attn_kernel(lam_init, tq, qt_ref, k_ref, vt_ref, lam_ref, g_ref, o_ref,
                        m_ref, acc_ref, sa_ref, sb_ref):
    qi = pl.program_id(2)
    qt = qt_ref[...]
    qrow = lax.broadcasted_iota(jnp.int32, qt.shape, 0)
    zero = jnp.zeros_like(qt)
    qmaps = (jnp.where(qrow < HEAD_DIM_A, qt, zero), jnp.where(qrow >= HEAD_DIM_A, qt, zero))

    m_ref[...] = jnp.full(m_ref.shape, NEG_BIG, F32)
    acc_ref[...] = jnp.zeros(acc_ref.shape, F32)
    ones = jnp.ones((SUM_ROWS, tq), BF16)

    def scores(j, s_ref):
        kj = k_ref[pl.ds(pl.multiple_of(j * tq, tq), tq), :]
        for mp in range(2):
            s_ref[mp] = _dot(kj, qmaps[mp])

    def consume(j, s_ref, masked):
        vj = jnp.concatenate([vt_ref[j], ones], axis=0)
        for mp in range(2):
            s = s_ref[mp]
            if masked:
                krow = lax.broadcasted_iota(jnp.int32, s.shape, 0)
                qcol = lax.broadcasted_iota(jnp.int32, s.shape, 1)
                s = jnp.where(krow <= qcol, s, NEG_BIG)
            m_old = m_ref[mp]
            m_new = jnp.maximum(m_old, jnp.max(s, axis=0, keepdims=True))
            alpha = jnp.exp2(m_old - m_new)
            p = jnp.exp2(s - m_new).astype(BF16)
            acc_ref[mp] = alpha * acc_ref[mp] + _dot(vj, p)
            m_ref[mp] = m_new

    def body(jj, carry):
        j = 2 * jj
        scores(j + 1, sb_ref)
        consume(j, sa_ref, False)
        scores(j + 2, sa_ref)
        consume(j + 1, sb_ref, False)
        return carry

    scores(0, sa_ref)
    lax.fori_loop(0, qi // 2, body, 0)

    @pl.when(qi % 2 == 0)
    def _():
        consume(qi, sa_ref, True)

    @pl.when(qi % 2 == 1)
    def _():
        scores(qi, sb_ref)
        consume(qi - 1, sa_ref, False)
        consume(qi, sb_ref, True)

    lam = _lambda(lam_ref, lam_init)
    acc1 = acc_ref[0]
    acc2 = acc_ref[1]
    o = (acc1[:HEAD_W] / acc1[HEAD_W:HEAD_W + 1]
         - lam * (acc2[:HEAD_W] / acc2[HEAD_W:HEAD_W + 1]))
    ms = jnp.mean(o * o, axis=0, keepdims=True)
    o = o * lax.rsqrt(ms + RMS_EPS) * g_ref[...] * (1.0 - lam_init)
    o_ref[...] = o.T.astype(o_ref.dtype)


def _prompt_attention(qt3, k_bf, vt3, lam_vecs, g_col, lam_init, batch, seq, tq):
    nq = seq // tq
    n = batch * seq
    return pl.pallas_call(
        functools.partial(_prompt_attn_kernel, lam_init, tq),
        grid=(batch, N_HEADS_A, nq),
        in_specs=[
            pl.BlockSpec((None, HEAD_W, tq), lambda b, h, i: (b * nq + i, h, 0)),
            pl.BlockSpec((seq, HEAD_W), lambda b, h, i: (b, h)),
            pl.BlockSpec((nq, HEAD_W, tq), lambda b, h, i: (b, h, 0)),
            pl.BlockSpec(lam_vecs.shape, lambda b, h, i: (0, 0)),
            pl.BlockSpec(g_col.shape, lambda b, h, i: (0, 0)),
        ],
        out_specs=pl.BlockSpec((tq, HEAD_W), lambda b, h, i: (b * nq + i, h)),
        out_shape=jax.ShapeDtypeStruct((n, MIX_W), BF16),
        scratch_shapes=[
            pltpu.VMEM((2, 1, tq), F32),
            pltpu.VMEM((2, HEAD_W + SUM_ROWS, tq), F32),
            pltpu.VMEM((2, tq, tq), F32),
            pltpu.VMEM((2, tq, tq), F32),
        ],
        compiler_params=pltpu.CompilerParams(
            dimension_semantics=("parallel", "parallel", "arbitrary"),
            vmem_limit_bytes=VMEM_LIMIT),
        name="prompt_diff_attention",
    )(qt3, k_bf, vt3, lam_vecs, g_col)


def _sample_attn_kernel(lam_init, pages_per_step, dec_seq, pt_ref, q_ref, kn_ref, vn_ref,
                        lam_ref, g_ref, *rest):
    k_refs = rest[:pages_per_step]
    v_refs = rest[pages_per_step:2 * pages_per_step]
    o_ref = rest[2 * pages_per_step]
    m_ref, l_ref, acc_ref = rest[2 * pages_per_step + 1:]
    j = pl.program_id(1)
    n_rows = q_ref.shape[0]

    qrep = q_ref[...]
    rrow = lax.broadcasted_iota(jnp.int32, qrep.shape, 0)
    rcol = lax.broadcasted_iota(jnp.int32, qrep.shape, 1)
    qbd = jnp.where(rcol // HEAD_DIM_A == rrow // dec_seq, qrep, 0.0).astype(BF16)

    @pl.when(j == 0)
    def _():
        m_ref[...] = jnp.full(m_ref.shape, NEG_BIG, F32)
        l_ref[...] = jnp.zeros(l_ref.shape, F32)
        acc_ref[...] = jnp.zeros(acc_ref.shape, F32)

    def update(s, v_bf):
        m_old = m_ref[...]
        m_new = jnp.maximum(m_old, jnp.max(s, axis=-1, keepdims=True))
        alpha = jnp.exp2(m_old - m_new)
        p = jnp.exp2(s - m_new)
        l_ref[...] = alpha * l_ref[...] + jnp.sum(p, axis=-1, keepdims=True)
        acc_ref[...] = alpha * acc_ref[...] + _dot(p.astype(BF16), v_bf)
        m_ref[...] = m_new

    page = k_refs[0].shape[1]
    kt = jnp.concatenate([r[...] for r in k_refs], axis=1).astype(BF16)
    vp = jnp.concatenate(
        [jnp.concatenate([r[pl.ds(h, page, stride=N_HEADS_A), :] for h in range(N_HEADS_A)], axis=1)
         for r in v_refs], axis=0).astype(BF16)
    update(_dot(qbd, kt), vp)

    @pl.when(j == pl.num_programs(1) - 1)
    def _():
        kn = kn_ref[...].astype(BF16)
        vn = vn_ref[...].astype(BF16)
        s = _dot_nt(qbd, kn)
        key = lax.broadcasted_iota(jnp.int32, s.shape, 1)
        tok = lax.broadcasted_iota(jnp.int32, s.shape, 0) % dec_seq
        s = jnp.where(key <= tok, s, NEG_BIG)
        update(s, vn)
        lam = _lambda(lam_ref, lam_init)
        acc = acc_ref[...] / l_ref[...]
        rows_per_head = 2 * dec_seq
        outs = []
        for h in range(N_HEADS_A):
            blk = acc[h * rows_per_head:(h + 1) * rows_per_head, h * HEAD_W:(h + 1) * HEAD_W]
            o = blk[:dec_seq] - lam * blk[dec_seq:]
            ms = jnp.mean(o * o, axis=-1, keepdims=True)
            outs.append(o * lax.rsqrt(ms + RMS_EPS) * g_ref[...] * (1.0 - lam_init))
        o_ref[...] = jnp.concatenate(outs, axis=1).astype(o_ref.dtype)


def _sample_attention(page_table, q_rep, k_new, v_new, cache_kt, cache_v, lam_vecs, g_row,
                      lam_init, layer, dec_seq, pages_per_step):
    bs, n_pages = page_table.shape
    page = cache_kt.shape[3]
    n_rows = q_rep.shape[1]
    steps = n_pages // pages_per_step

    def k_spec(pg):
        return pl.BlockSpec((None, None, MIX_W, page),
                            lambda b, j, pt: (pt[b, j * pages_per_step + pg], layer, 0, 0))

    def v_spec(pg):
        return pl.BlockSpec((None, None, page * N_HEADS_A, HEAD_W),
                            lambda b, j, pt: (pt[b, j * pages_per_step + pg], layer, 0, 0))

    per_b = lambda b, j, pt: (b, 0, 0)
    const = lambda b, j, pt: (0, 0)
    grid_spec = pltpu.PrefetchScalarGridSpec(
        num_scalar_prefetch=1,
        grid=(bs, steps),
        in_specs=[
            pl.BlockSpec((None, n_rows, MIX_W), per_b),
            pl.BlockSpec((None, k_new.shape[1], MIX_W), per_b),
            pl.BlockSpec((None, k_new.shape[1], MIX_W), per_b),
            pl.BlockSpec(lam_vecs.shape, const),
            pl.BlockSpec(g_row.shape, const),
        ] + [k_spec(pg) for pg in range(pages_per_step)]
          + [v_spec(pg) for pg in range(pages_per_step)],
        out_specs=pl.BlockSpec((None, dec_seq, MIX_W), per_b),
        scratch_shapes=[
            pltpu.VMEM((n_rows, 1), F32),
            pltpu.VMEM((n_rows, 1), F32),
            pltpu.VMEM((n_rows, MIX_W), F32),
        ],
    )
    return pl.pallas_call(
        functools.partial(_sample_attn_kernel, lam_init, pages_per_step, dec_seq),
        grid_spec=grid_spec,
        out_shape=jax.ShapeDtypeStruct((bs, dec_seq, MIX_W), BF16),
        compiler_params=pltpu.CompilerParams(
            dimension_semantics=("parallel", "arbitrary"), vmem_limit_bytes=VMEM_LIMIT),
        name="sample_diff_attention",
    )(page_table, q_rep, k_new, v_new, lam_vecs, g_row,
      *([cache_kt] * pages_per_step), *([cache_v] * pages_per_step))


def _hgrn_kernel(step, n_steps, q_ref, k_ref, lf_ref, v_ref, s0_ref, o_ref, sout_ref, st_ref):
    ti = pl.program_id(1)

    @pl.when(ti == 0)
    def _():
        for h in range(N_HEADS_B):
            st_ref[h] = s0_ref[h].T

    ones = jnp.ones((KEY_DIM_B, LANES), BF16)
    rowi = lax.broadcasted_iota(jnp.int32, (step, KEY_DIM_B), 0)

    def body(i, carry):
        r0 = pl.multiple_of(i * step, step)
        for h in range(N_HEADS_B):
            cols = slice(h * KEY_DIM_B, (h + 1) * KEY_DIM_B)
            q = q_ref[pl.ds(r0, step), cols]
            k = k_ref[pl.ds(r0, step), cols]
            v = v_ref[pl.ds(r0, step), cols]
            b = lf_ref[pl.ds(r0, step), cols]
            d = 1
            while d < step:
                b = b + jnp.where(rowi >= d, pltpu.roll(b, d, 0), 0.0)
                d *= 2
            st = st_ref[h]
            o = _dot_nt((q * jnp.exp2(b)).astype(BF16), st.astype(BF16))
            xs = []
            for s in range(step):
                lo = (s // 8) * 8
                e = jnp.exp2(jnp.where(rowi[lo:] >= s, b[lo:] - b[s:s + 1], NEG_BIG))
                xs.append(q[lo:] * k[s:s + 1] * e)
            att = _dot(jnp.concatenate(xs, axis=0).astype(BF16), ones)
            parts = [o[lo:lo + 8] for lo in range(0, step, 8)]
            off = 0
            for s in range(step):
                lo = (s // 8) * 8
                for g in range(lo // 8, step // 8):
                    parts[g] = parts[g] + att[off:off + 8] * v[s:s + 1]
                    off += 8
            o_ref[pl.ds(r0, step), cols] = jnp.concatenate(parts, axis=0)
            b_last = b[step - 1:step]
            kd = k * jnp.exp2(b_last - b)
            st_ref[h] = st * jnp.exp2(b_last) + _dot_tn(v.astype(BF16), kd.astype(BF16))
        return carry

    lax.fori_loop(0, n_steps, body, 0, unroll=math.gcd(n_steps, HGRN_UNROLL))

    @pl.when(ti == pl.num_programs(1) - 1)
    def _():
        for h in range(N_HEADS_B):
            sout_ref[h] = st_ref[h].T


def _hgrn_scan(q, k, lf, v, s0, n_seq, seq_rows, tile_rows, step):
    n = q.shape[0]
    tiles = seq_rows // tile_rows
    row = lambda s, t: (s * tiles + t, 0)
    spec = pl.BlockSpec((tile_rows, MIX_W), row)
    st_spec = pl.BlockSpec((None, N_HEADS_B, KEY_DIM_B, KEY_DIM_B), lambda s, t: (s, 0, 0, 0))
    return pl.pallas_call(
        functools.partial(_hgrn_kernel, step, tile_rows // step),
        grid=(n_seq, tiles),
        in_specs=[spec, spec, spec, spec, st_spec],
        out_specs=[spec, st_spec],
        out_shape=[jax.ShapeDtypeStruct((n, MIX_W), F32),
                   jax.ShapeDtypeStruct(s0.shape, F32)],
        scratch_shapes=[pltpu.VMEM((N_HEADS_B, KEY_DIM_B, KEY_DIM_B), F32)],
        compiler_params=pltpu.CompilerParams(
            dimension_semantics=("parallel", "arbitrary"), vmem_limit_bytes=VMEM_LIMIT),
        name="hgrn2_scan",
    )(q, k, lf, v, s0)


def _finish_kernel(alpha, d_ff, seq_rows, x_ref, oa_ref, ob_ref, c0_ref, c1_ref,
                   wg_ref, ng_ref, wa_ref, wb_ref, wo_ref, l1g_ref, l1b_ref, wu_ref,
                   cw_ref, cb_ref, wd_ref, l2g_ref, l2b_ref, out_ref, conv_ref, *scratch):
    tm = x_ref.shape[0]
    x = x_ref[...]
    gates = _dot(x.astype(BF16), wg_ref[...])
    d_model = x.shape[1]
    gb = gates[:, :MIX_W]
    ga = gates[:, MIX_W:MIX_W + d_model]
    gg = gates[:, MIX_W + d_model:]

    ob = ob_ref[...]
    normed = []
    for h in range(N_HEADS_B):
        oh = ob[:, h * KEY_DIM_B:(h + 1) * KEY_DIM_B]
        ms = jnp.mean(oh * oh, axis=-1, keepdims=True)
        normed.append(oh * lax.rsqrt(ms + RMS_EPS) * ng_ref[...])
    obn = jnp.concatenate(normed, axis=1) * (gb * jax.nn.sigmoid(gb))

    merged = (jax.nn.sigmoid(ga) * _dot(oa_ref[...], wa_ref[...])
              + jax.nn.sigmoid(gg) * _dot(obn.astype(BF16), wb_ref[...]))
    h1 = _layer_norm(alpha * x + _dot(merged.astype(BF16), wo_ref[...]), l1g_ref[...], l1b_ref[...])

    hb = h1.astype(BF16)
    chunks = [(c, min(c + FFN_CHUNK, d_ff)) for c in range(0, d_ff, FFN_CHUNK)]

    def up(lo, hi):
        return _dot(hb, wu_ref[:, lo:hi]), _dot(hb, wu_ref[:, d_ff + lo:d_ff + hi])

    if seq_rows is None:
        (a_ref,) = scratch

        @pl.when(pl.program_id(1) == 0)
        def _():
            a_ref[6:8, :] = c0_ref[...]

    y = None
    nxt = up(*chunks[0])
    for i, (lo, hi) in enumerate(chunks):
        a, g = nxt
        if i + 1 < len(chunks):
            nxt = up(*chunks[i + 1])
        if seq_rows is None:
            a_ref[8:8 + tm, lo:hi] = a
            a1 = a_ref[7:7 + tm, lo:hi]
            a2 = a_ref[6:6 + tm, lo:hi]
            tail = a_ref[6 + tm:8 + tm, lo:hi]
            a_ref[6:8, lo:hi] = tail
            conv_ref[:, lo:hi] = tail
        else:
            pos = lax.broadcasted_iota(jnp.int32, a.shape, 0) % seq_rows
            a1 = jnp.where(pos >= 1, pltpu.roll(a, 1, 0), 0.0) + c1_ref[:, lo:hi]
            a2 = jnp.where(pos >= 2, pltpu.roll(a, 2, 0), 0.0) + c0_ref[:, lo:hi]
            conv_ref[:, lo:hi] = a
        c = (cb_ref[:, lo:hi] + a2 * cw_ref[0:1, lo:hi] + a1 * cw_ref[1:2, lo:hi]
             + a * cw_ref[2:3, lo:hi])
        act = 0.5 * c * (1.0 + lax.erf(c * (2.0 ** -0.5))) * g
        yi = _dot(act.astype(BF16), wd_ref[lo:hi, :])
        y = yi if y is None else y + yi
    out_ref[...] = _layer_norm(alpha * h1 + y, l2g_ref[...], l2b_ref[...])


def _finish(x, oa, ob, c0, c1, weights, alpha, n_seq, seq_tiles, tm, seq_rows):
    n, d_model = x.shape
    d_ff = weights[-3].shape[0]
    row = lambda s, t: (s * seq_tiles + t, 0)
    const = lambda s, t: (0, 0)
    once = pl.Buffered(1)

    def wspec(w):
        return pl.BlockSpec(w.shape, const, pipeline_mode=once)

    if seq_rows is None:
        c_specs = [pl.BlockSpec((None, CONV_W - 1, d_ff), lambda s, t: (s, 0, 0))] * 2
        conv_shape = jax.ShapeDtypeStruct((n_seq, CONV_W - 1, d_ff), F32)
        conv_spec = pl.BlockSpec((None, CONV_W - 1, d_ff), lambda s, t: (s, 0, 0))
        scratch = [pltpu.VMEM((tm + 8, d_ff), F32)]
    else:
        c_specs = [pl.BlockSpec((tm, d_ff), row)] * 2
        conv_shape = jax.ShapeDtypeStruct((n, d_ff), F32)
        conv_spec = pl.BlockSpec((tm, d_ff), row)
        scratch = []
    return pl.pallas_call(
        functools.partial(_finish_kernel, alpha, d_ff, seq_rows),
        grid=(n_seq, seq_tiles),
        in_specs=[pl.BlockSpec((tm, d_model), row),
                  pl.BlockSpec((tm, MIX_W), row),
                  pl.BlockSpec((tm, MIX_W), row)] + c_specs + [wspec(w) for w in weights],
        out_specs=[pl.BlockSpec((tm, d_model), row), conv_spec],
        out_shape=[jax.ShapeDtypeStruct((n, d_model), F32), conv_shape],
        scratch_shapes=scratch,
        compiler_params=pltpu.CompilerParams(
            dimension_semantics=("parallel", "arbitrary"), vmem_limit_bytes=VMEM_LIMIT),
        name="finish_layer",
    )(x, oa, ob, c0, c1, *weights)


MIXER_TILE = 512
HGRN_TILE = 1024
HGRN_STEP = 16
HGRN_UNROLL = 4
FINISH_TILE = 256
FFN_CHUNK = 512
PAGES_PER_STEP = 16
NEW_KEY_PAD = 16


def _rope_tables(pos):
    half = HEAD_DIM_A // 2
    inv = ROPE_THETA ** (-jnp.arange(half, dtype=F32) * 2.0 / HEAD_DIM_A)
    ang = pos.astype(F32)[:, None] * inv[None, :]
    cos = jnp.cos(ang)
    sin = jnp.sin(ang)
    reps = LANES // HEAD_DIM_A
    return (jnp.tile(jnp.concatenate([cos, cos], axis=1), (1, reps)),
            jnp.tile(jnp.concatenate([-sin, sin], axis=1), (1, reps)))


def kernel(x_prompt, x_sample, cache_k, cache_v, state_hgrn, state_conv, page_table, w_in, lambda_q1, lambda_k1, lambda_q2, lambda_k2, subln_g, lb_logits, hgrn_norm_g, w_branch_a, w_branch_b, w_out, ln1_g, ln1_b, w_up, conv_w, conv_b, w_down, ln2_g, ln2_b):
    bp, tp, d_model = x_prompt.shape
    bs, ts, _ = x_sample.shape
    depth = w_in.shape[0]
    n_phys, _, page, _, _, _ = cache_k.shape
    past_len = page_table.shape[1] * page
    d_ff = w_down.shape[1]
    alpha = (2 * depth) ** 0.25
    n_mix = 6 * MIX_W
    ts_pad = 8
    lb_logits = lb_logits.astype(F32)

    cos_p, sin_p = _rope_tables(jnp.arange(tp))
    cos_s, sin_s = _rope_tables(past_len + jnp.arange(bs * ts) % ts)
    cache_kt = jnp.transpose(cache_k, (0, 1, 3, 4, 5, 2)).reshape(n_phys, depth, MIX_W, page)
    cache_vr = cache_v.reshape(n_phys, depth, page * N_HEADS_A, HEAD_W)

    xp = x_prompt.reshape(bp * tp, d_model)
    xs = x_sample.reshape(bs * ts, d_model)
    outs = {name: [] for name in ("kp", "vp", "sp", "cp", "ks", "vs", "ss", "cs")}
    for l in range(depth):
        lam_init = 0.8 - 0.6 * math.exp(-0.3 * l)
        w_mix = w_in[l, :, :n_mix].astype(BF16)
        lam_vecs = jnp.stack([lambda_q1[l], lambda_k1[l], lambda_q2[l], lambda_k2[l]]).astype(F32)
        g_row = subln_g[l].reshape(1, HEAD_W).astype(F32)
        weights = (
            w_in[l, :, n_mix:].astype(BF16),
            hgrn_norm_g[l].reshape(1, KEY_DIM_B).astype(F32),
            w_branch_a[l].astype(BF16), w_branch_b[l].astype(BF16), w_out[l].astype(BF16),
            ln1_g[l].reshape(1, d_model), ln1_b[l].reshape(1, d_model),
            w_up[l].astype(BF16), conv_w[l], conv_b[l].reshape(1, d_ff),
            w_down[l].astype(BF16), ln2_g[l].reshape(1, d_model), ln2_b[l].reshape(1, d_model))

        k_p, v_p, qt3, k_bf, vt3, qb, kk, lf, ib = _mixer_proj(
            xp, w_mix, cos_p, sin_p, lb_logits, l, True, MIXER_TILE)
        oa = _prompt_attention(qt3, k_bf, vt3, lam_vecs, g_row.reshape(HEAD_W, 1), lam_init,
                               bp, tp, MIXER_TILE)
        s0 = jnp.zeros((bp, N_HEADS_B, KEY_DIM_B, KEY_DIM_B), F32)
        ob, s_p = _hgrn_scan(qb, kk, lf, ib, s0, bp, tp, HGRN_TILE, HGRN_STEP)
        conv0 = jnp.zeros((bp, CONV_W - 1, d_ff), F32)
        xp_new, conv_p = _finish(xp, oa, ob, conv0, conv0, weights, alpha,
                                 bp, tp // FINISH_TILE, FINISH_TILE, None)
        outs["kp"].append(jnp.transpose(k_p.reshape(bp, N_HEADS_A, 2, HEAD_DIM_A, tp), (0, 4, 1, 2, 3)))
        outs["vp"].append(v_p.reshape(bp, tp, N_HEADS_A, HEAD_W))
        outs["sp"].append(s_p)
        outs["cp"].append(conv_p)

        k_s, v_s, q_s, qb, kk, lf, ib = _mixer_proj(
            xs, w_mix, cos_s, sin_s, lb_logits, l, False, bs * ts)
        pad_t = lambda a, rows: jnp.pad(a.reshape(bs, ts, MIX_W), ((0, 0), (0, rows - ts), (0, 0)))
        q_rep = jnp.tile(q_s.reshape(bs, ts, MIX_W), (1, 2 * N_HEADS_A, 1))
        oa = _sample_attention(page_table, q_rep, pad_t(k_s, NEW_KEY_PAD), pad_t(v_s, NEW_KEY_PAD),
                               cache_kt, cache_vr, lam_vecs, g_row, lam_init, l, ts, PAGES_PER_STEP)
        flat = lambda a: pad_t(a, ts_pad).reshape(bs * ts_pad, MIX_W)
        ob, s_s = _hgrn_scan(flat(qb), flat(kk), flat(lf), flat(ib), state_hgrn[:, l],
                             bs, ts_pad, ts_pad, ts_pad)
        ob = ob.reshape(bs, ts_pad, MIX_W)[:, :ts].reshape(bs * ts, MIX_W)
        st = state_conv[:, l]
        c0 = jnp.concatenate([st, jnp.zeros((bs, ts - 2, d_ff), F32)], axis=1)
        c1 = jnp.concatenate([st[:, 1:], jnp.zeros((bs, ts - 1, d_ff), F32)], axis=1)
        xs_new, a_s = _finish(xs, oa.reshape(bs * ts, MIX_W), ob,
                              c0.reshape(bs * ts, d_ff), c1.reshape(bs * ts, d_ff),
                              weights, alpha, 1, 1, bs * ts, ts)
        conv_s = jnp.concatenate([st, a_s.reshape(bs, ts, d_ff)], axis=1)[:, -(CONV_W - 1):]
        outs["ks"].append(k_s.reshape(bs, ts, N_HEADS_A, 2, HEAD_DIM_A))
        outs["vs"].append(v_s.reshape(bs, ts, N_HEADS_A, HEAD_W))
        outs["ss"].append(s_s)
        outs["cs"].append(conv_s)
        xp, xs = xp_new, xs_new

    stack = lambda name: jnp.stack(outs[name], axis=1)
    return (xp.reshape(bp, tp, d_model), xs.reshape(bs, ts, d_model),
            stack("kp"), stack("vp"), stack("sp"), stack("cp"),
            stack("ks"), stack("vs"), stack("ss"), stack("cs"))
```

```python
import functools
import math

import jax
import jax.numpy as jnp
from jax import lax
from jax.experimental import pallas as pl
from jax.experimental.pallas import tpu as pltpu

F32 = jnp.float32
BF16 = jnp.bfloat16

N_HEADS_A = 4
HEAD_DIM_A = 64
N_HEADS_B = 4
KEY_DIM_B = 128
CONV_W = 3
ROPE_THETA = 10000.0
LN_EPS = 1e-5
RMS_EPS = 1e-5

LANES = 128
HEAD_W = 2 * HEAD_DIM_A
MIX_W = N_HEADS_A * HEAD_W
NEG_BIG = -1e30
SUM_ROWS = 16

VMEM_LIMIT = 56 * 1024 * 1024


def _dot(a, b):
    return jnp.dot(a, b, preferred_element_type=F32)


def _dot_nt(a, b):
    return lax.dot_general(a, b, (((1,), (1,)), ((), ())), preferred_element_type=F32)


def _dot_tn(a, b):
    return lax.dot_general(a, b, (((0,), (0,)), ((), ())), preferred_element_type=F32)


def _layer_norm(x, g, b):
    mu = jnp.mean(x, axis=-1, keepdims=True)
    xc = x - mu
    var = jnp.mean(xc * xc, axis=-1, keepdims=True)
    return xc * lax.rsqrt(var + LN_EPS) * g + b


def _lambda(lam_ref, lam_init):
    lv = lam_ref[...]
    e1 = jnp.exp(jnp.sum(lv[0:1] * lv[1:2], axis=-1, keepdims=True))
    e2 = jnp.exp(jnp.sum(lv[2:3] * lv[3:4], axis=-1, keepdims=True))
    return e1 - e2 + lam_init


def _rope(x, cos, sin_signed):
    lane = lax.broadcasted_iota(jnp.int32, x.shape, 1)
    first_half = (lane % HEAD_DIM_A) < (HEAD_DIM_A // 2)
    partner = jnp.where(first_half, pltpu.roll(x, LANES - HEAD_DIM_A // 2, 1),
                        pltpu.roll(x, HEAD_DIM_A // 2, 1))
    return x * cos + partner * sin_signed


def _mixer_kernel(layer, transposed, x_ref, w_ref, cos_ref, sin_ref, lbl_ref, *out_refs):
    if transposed:
        k_ref, v_ref, qt_ref, kb_ref, vt_ref, qb_ref, kk_ref, lf_ref, ib_ref = out_refs
    else:
        k_ref, v_ref, q_ref, qb_ref, kk_ref, lf_ref, ib_ref = out_refs
    xb = x_ref[...].astype(BF16)
    cos = cos_ref[...]
    sin = sin_ref[...]

    def section(i):
        return _dot(xb, w_ref[:, i * MIX_W:(i + 1) * MIX_W])

    scale = HEAD_DIM_A ** -0.5 * math.log2(math.e)
    qa = section(0)
    qa = jnp.concatenate([_rope(qa[:, h * HEAD_W:(h + 1) * HEAD_W], cos, sin) * scale
                          for h in range(N_HEADS_A)], axis=1)
    ka = section(1)
    ka = jnp.concatenate([_rope(ka[:, h * HEAD_W:(h + 1) * HEAD_W], cos, sin)
                          for h in range(N_HEADS_A)], axis=1)
    va = section(2)
    if transposed:
        k_ref[...] = ka.T
        for h in range(N_HEADS_A):
            v_ref[pl.ds(h, va.shape[0], stride=N_HEADS_A), :] = va[:, h * HEAD_W:(h + 1) * HEAD_W]
        qt_ref[...] = qa.T.astype(BF16)
        kb_ref[...] = ka.astype(BF16)
        vt_ref[...] = va.T.astype(BF16)
    else:
        k_ref[...] = ka
        v_ref[...] = va
        q_ref[...] = qa

    qb_ref[...] = section(3)
    fr = section(4)
    ib_ref[...] = section(5)

    lg = lbl_ref[...]
    e = jnp.exp(lg - jnp.max(lg, axis=0, keepdims=True))
    lb = jnp.sum(e[:layer + 1], axis=0, keepdims=True) / jnp.sum(e, axis=0, keepdims=True)
    f = lb + (1.0 - lb) * (1.0 / (1.0 + jnp.exp(-fr)))
    lf_ref[...] = jnp.log2(f)
    kk_ref[...] = jnp.log2((1.0 - lb) * (1.0 / (1.0 + jnp.exp(fr))))


def _mixer_proj(x, w_mix, cos, sin, lb_logits, layer, transposed, tm):
    n = x.shape[0]
    d = x.shape[1]
    n_tiles = n // tm
    pos_tiles = cos.shape[0] // tm
    row = lambda i: (i, 0)
    f32_out = jax.ShapeDtypeStruct((n, MIX_W), F32)
    row_spec = pl.BlockSpec((tm, MIX_W), row)
    if transposed:
        t_shape = jax.ShapeDtypeStruct((n_tiles, MIX_W, tm), BF16)
        t_spec = pl.BlockSpec((None, MIX_W, tm), lambda i: (i, 0, 0))
        kt_shape = jax.ShapeDtypeStruct((n // cos.shape[0], MIX_W, cos.shape[0]), F32)
        kt_spec = pl.BlockSpec((None, MIX_W, tm), lambda i: (i // pos_tiles, 0, i % pos_tiles))
        v_shape = jax.ShapeDtypeStruct((n * N_HEADS_A, HEAD_W), F32)
        v_spec = pl.BlockSpec((tm * N_HEADS_A, HEAD_W), row)
        out_shape = [kt_shape, v_shape, t_shape, jax.ShapeDtypeStruct((n, MIX_W), BF16), t_shape,
                     f32_out, f32_out, f32_out, f32_out]
        out_specs = [kt_spec, v_spec, t_spec, row_spec, t_spec,
                     row_spec, row_spec, row_spec, row_spec]
    else:
        out_shape = [f32_out] * 7
        out_specs = [row_spec] * 7
    return pl.pallas_call(
        functools.partial(_mixer_kernel, layer, transposed),
        grid=(n_tiles,),
        in_specs=[
            pl.BlockSpec((tm, d), row),
            pl.BlockSpec(w_mix.shape, lambda i: (0, 0)),
            pl.BlockSpec((tm, LANES), lambda i: (i % pos_tiles, 0)),
            pl.BlockSpec((tm, LANES), lambda i: (i % pos_tiles, 0)),
            pl.BlockSpec(lb_logits.shape, lambda i: (0, 0)),
        ],
        out_specs=out_specs,
        out_shape=out_shape,
        compiler_params=pltpu.CompilerParams(
            dimension_semantics=("parallel",), vmem_limit_bytes=VMEM_LIMIT),
        name="mixer_proj",
    )(x, w_mix, cos, sin, lb_logits)


def _prompt_attn_kernel(lam_init, tq, qt_ref, k_ref, vt_ref, lam_ref, g_ref, o_ref,
                        m_ref, acc_ref, sa_ref, sb_ref):
    qi = pl.program_id(2)
    qt = qt_ref[...]
    qrow = lax.broadcasted_iota(jnp.int32, qt.shape, 0)
    zero = jnp.zeros_like(qt)
    qmaps = (jnp.where(qrow < HEAD_DIM_A, qt, zero), jnp.where(qrow >= HEAD_DIM_A, qt, zero))

    m_ref[...] = jnp.full(m_ref.shape, NEG_BIG, F32)
    acc_ref[...] = jnp.zeros(acc_ref.shape, F32)
    ones = jnp.ones((SUM_ROWS, tq), BF16)

    def scores(j, s_ref):
        kj = k_ref[pl.ds(pl.multiple_of(j * tq, tq), tq), :]
        for mp in range(2):
            s_ref[mp] = _dot(kj, qmaps[mp])

    def consume(j, s_ref, masked):
        vj = jnp.concatenate([vt_ref[j], ones], axis=0)
        for mp in range(2):
            s = s_ref[mp]
            if masked:
                krow = lax.broadcasted_iota(jnp.int32, s.shape, 0)
                qcol = lax.broadcasted_iota(jnp.int32, s.shape, 1)
                s = jnp.where(krow <= qcol, s, NEG_BIG)
            m_old = m_ref[mp]
            m_new = jnp.maximum(m_old, jnp.max(s, axis=0, keepdims=True))
            alpha = jnp.exp2(m_old - m_new)
            p = jnp.exp2(s - m_new).astype(BF16)
            acc_ref[mp] = alpha * acc_ref[mp] + _dot(vj, p)
            m_ref[mp] = m_new

    def body(jj, carry):
        j = 2 * jj
        scores(j + 1, sb_ref)
        consume(j, sa_ref, False)
        scores(j + 2, sa_ref)
        consume(j + 1, sb_ref, False)
        return carry

    scores(0, sa_ref)
    lax.fori_loop(0, qi // 2, body, 0)

    @pl.when(qi % 2 == 0)
    def _():
        consume(qi, sa_ref, True)

    @pl.when(qi % 2 == 1)
    def _():
        scores(qi, sb_ref)
        consume(qi - 1, sa_ref, False)
        consume(qi, sb_ref, True)

    lam = _lambda(lam_ref, lam_init)
    acc1 = acc_ref[0]
    acc2 = acc_ref[1]
    o = (acc1[:HEAD_W] / acc1[HEAD_W:HEAD_W + 1]
         - lam * (acc2[:HEAD_W] / acc2[HEAD_W:HEAD_W + 1]))
    ms = jnp.mean(o * o, axis=0, keepdims=True)
    o = o * lax.rsqrt(ms + RMS_EPS) * g_ref[...] * (1.0 - lam_init)
    o_ref[...] = o.T.astype(o_ref.dtype)


def _prompt_attention(qt3, k_bf, vt3, lam_vecs, g_col, lam_init, batch, seq, tq):
    nq = seq // tq
    n = batch * seq
    return pl.pallas_call(
        functools.partial(_prompt_attn_kernel, lam_init, tq),
        grid=(batch, N_HEADS_A, nq),
        in_specs=[
            pl.BlockSpec((None, HEAD_W, tq), lambda b, h, i: (b * nq + i, h, 0)),
            pl.BlockSpec((seq, HEAD_W), lambda b, h, i: (b, h)),
            pl.BlockSpec((nq, HEAD_W, tq), lambda b, h, i: (b, h, 0)),
            pl.BlockSpec(lam_vecs.shape, lambda b, h, i: (0, 0)),
            pl.BlockSpec(g_col.shape, lambda b, h, i: (0, 0)),
        ],
        out_specs=pl.BlockSpec((tq, HEAD_W), lambda b, h, i: (b * nq + i, h)),
        out_shape=jax.ShapeDtypeStruct((n, MIX_W), BF16),
        scratch_shapes=[
            pltpu.VMEM((2, 1, tq), F32),
            pltpu.VMEM((2, HEAD_W + SUM_ROWS, tq), F32),
            pltpu.VMEM((2, tq, tq), F32),
            pltpu.VMEM((2, tq, tq), F32),
        ],
        compiler_params=pltpu.CompilerParams(
            dimension_semantics=("parallel", "parallel", "arbitrary"),
            vmem_limit_bytes=VMEM_LIMIT),
        name="prompt_diff_attention",
    )(qt3, k_bf, vt3, lam_vecs, g_col)


def _sample_attn_kernel(lam_init, pages_per_step, dec_seq, pt_ref, q_ref, kn_ref, vn_ref,
                        lam_ref, g_ref, *rest):
    k_refs = rest[:pages_per_step]
    v_refs = rest[pages_per_step:2 * pages_per_step]
    o_ref = rest[2 * pages_per_step]
    m_ref, l_ref, acc_ref = rest[2 * pages_per_step + 1:]
    j = pl.program_id(1)
    n_rows = q_ref.shape[0]

    qrep = q_ref[...]
    rrow = lax.broadcasted_iota(jnp.int32, qrep.shape, 0)
    rcol = lax.broadcasted_iota(jnp.int32, qrep.shape, 1)
    qbd = jnp.where(rcol // HEAD_DIM_A == rrow // dec_seq, qrep, 0.0).astype(BF16)

    @pl.when(j == 0)
    def _():
        m_ref[...] = jnp.full(m_ref.shape, NEG_BIG, F32)
        l_ref[...] = jnp.zeros(l_ref.shape, F32)
        acc_ref[...] = jnp.zeros(acc_ref.shape, F32)

    def update(s, v_bf):
        m_old = m_ref[...]
        m_new = jnp.maximum(m_old, jnp.max(s, axis=-1, keepdims=True))
        alpha = jnp.exp2(m_old - m_new)
        p = jnp.exp2(s - m_new)
        l_ref[...] = alpha * l_ref[...] + jnp.sum(p, axis=-1, keepdims=True)
        acc_ref[...] = alpha * acc_ref[...] + _dot(p.astype(BF16), v_bf)
        m_ref[...] = m_new

    page = k_refs[0].shape[1]
    kt = jnp.concatenate([r[...] for r in k_refs], axis=1).astype(BF16)
    vp = jnp.concatenate(
        [jnp.concatenate([r[pl.ds(h, page, stride=N_HEADS_A), :] for h in range(N_HEADS_A)], axis=1)
         for r in v_refs], axis=0).astype(BF16)
    update(_dot(qbd, kt), vp)

    @pl.when(j == pl.num_programs(1) - 1)
    def _():
        kn = kn_ref[...].astype(BF16)
        vn = vn_ref[...].astype(BF16)
        s = _dot_nt(qbd, kn)
        key = lax.broadcasted_iota(jnp.int32, s.shape, 1)
        tok = lax.broadcasted_iota(jnp.int32, s.shape, 0) % dec_seq
        s = jnp.where(key <= tok, s, NEG_BIG)
        update(s, vn)
        lam = _lambda(lam_ref, lam_init)
        acc = acc_ref[...] / l_ref[...]
        rows_per_head = 2 * dec_seq
        outs = []
        for h in range(N_HEADS_A):
            blk = acc[h * rows_per_head:(h + 1) * rows_per_head, h * HEAD_W:(h + 1) * HEAD_W]
            o = blk[:dec_seq] - lam * blk[dec_seq:]
            ms = jnp.mean(o * o, axis=-1, keepdims=True)
            outs.append(o * lax.rsqrt(ms + RMS_EPS) * g_ref[...] * (1.0 - lam_init))
        o_ref[...] = jnp.concatenate(outs, axis=1).astype(o_ref.dtype)


def _sample_attention(page_table, q_rep, k_new, v_new, cache_kt, cache_v, lam_vecs, g_row,
                      lam_init, layer, dec_seq, pages_per_step):
    bs, n_pages = page_table.shape
    page = cache_kt.shape[3]
    n_rows = q_rep.shape[1]
    steps = n_pages // pages_per_step

    def k_spec(pg):
        return pl.BlockSpec((None, None, MIX_W, page),
                            lambda b, j, pt: (pt[b, j * pages_per_step + pg], layer, 0, 0))

    def v_spec(pg):
        return pl.BlockSpec((None, None, page * N_HEADS_A, HEAD_W),
                            lambda b, j, pt: (pt[b, j * pages_per_step + pg], layer, 0, 0))

    per_b = lambda b, j, pt: (b, 0, 0)
    const = lambda b, j, pt: (0, 0)
    grid_spec = pltpu.PrefetchScalarGridSpec(
        num_scalar_prefetch=1,
        grid=(bs, steps),
        in_specs=[
            pl.BlockSpec((None, n_rows, MIX_W), per_b),
            pl.BlockSpec((None, k_new.shape[1], MIX_W), per_b),
            pl.BlockSpec((None, k_new.shape[1], MIX_W), per_b),
            pl.BlockSpec(lam_vecs.shape, const),
            pl.BlockSpec(g_row.shape, const),
        ] + [k_spec(pg) for pg in range(pages_per_step)]
          + [v_spec(pg) for pg in range(pages_per_step)],
        out_specs=pl.BlockSpec((None, dec_seq, MIX_W), per_b),
        scratch_shapes=[
            pltpu.VMEM((n_rows, 1), F32),
            pltpu.VMEM((n_rows, 1), F32),
            pltpu.VMEM((n_rows, MIX_W), F32),
        ],
    )
    return pl.pallas_call(
        functools.partial(_sample_attn_kernel, lam_init, pages_per_step, dec_seq),
        grid_spec=grid_spec,
        out_shape=jax.ShapeDtypeStruct((bs, dec_seq, MIX_W), BF16),
        compiler_params=pltpu.CompilerParams(
            dimension_semantics=("parallel", "arbitrary"), vmem_limit_bytes=VMEM_LIMIT),
        name="sample_diff_attention",
    )(page_table, q_rep, k_new, v_new, lam_vecs, g_row,
      *([cache_kt] * pages_per_step), *([cache_v] * pages_per_step))


def _hgrn_kernel(step, n_steps, q_ref, k_ref, lf_ref, v_ref, s0_ref, o_ref, sout_ref, st_ref):
    ti = pl.program_id(1)

    @pl.when(ti == 0)
    def _():
        for h in range(N_HEADS_B):
            st_ref[h] = s0_ref[h].T

    ones = jnp.ones((KEY_DIM_B, LANES), BF16)
    rowi = lax.broadcasted_iota(jnp.int32, (step, KEY_DIM_B), 0)

    def body(i, carry):
        r0 = pl.multiple_of(i * step, step)
        for h in range(N_HEADS_B):
            cols = slice(h * KEY_DIM_B, (h + 1) * KEY_DIM_B)
            q = q_ref[pl.ds(r0, step), cols]
            lk = k_ref[pl.ds(r0, step), cols]
            v = v_ref[pl.ds(r0, step), cols]
            b = lf_ref[pl.ds(r0, step), cols]
            d = 1
            while d < step:
                b = b + jnp.where(rowi >= d, pltpu.roll(b, d, 0), 0.0)
                d *= 2
            st = st_ref[h]
            o = _dot_nt((q * jnp.exp2(b)).astype(BF16), st.astype(BF16))
            c = b - lk
            xs = []
            for s in range(step):
                lo = (s // 8) * 8
                e = jnp.exp2(jnp.where(rowi[lo:] >= s, b[lo:] - c[s:s + 1], NEG_BIG))
                xs.append(q[lo:] * e)
            att = _dot(jnp.concatenate(xs, axis=0).astype(BF16), ones)
            parts = [o[lo:lo + 8] for lo in range(0, step, 8)]
            off = 0
            for s in range(step):
                lo = (s // 8) * 8
                for g in range(lo // 8, step // 8):
                    parts[g] = parts[g] + att[off:off + 8] * v[s:s + 1]
                    off += 8
            o_ref[pl.ds(r0, step), cols] = jnp.concatenate(parts, axis=0)
            b_last = b[step - 1:step]
            kd = jnp.exp2(b_last - c)
            st_ref[h] = st * jnp.exp2(b_last) + _dot_tn(v.astype(BF16), kd.astype(BF16))
        return carry

    lax.fori_loop(0, n_steps, body, 0, unroll=math.gcd(n_steps, HGRN_UNROLL))

    @pl.when(ti == pl.num_programs(1) - 1)
    def _():
        for h in range(N_HEADS_B):
            sout_ref[h] = st_ref[h].T


def _hgrn_scan(q, k, lf, v, s0, n_seq, seq_rows, tile_rows, step):
    n = q.shape[0]
    tiles = seq_rows // tile_rows
    row = lambda s, t: (s * tiles + t, 0)
    spec = pl.BlockSpec((tile_rows, MIX_W), row)
    st_spec = pl.BlockSpec((None, N_HEADS_B, KEY_DIM_B, KEY_DIM_B), lambda s, t: (s, 0, 0, 0))
    return pl.pallas_call(
        functools.partial(_hgrn_kernel, step, tile_rows // step),
        grid=(n_seq, tiles),
        in_specs=[spec, spec, spec, spec, st_spec],
        out_specs=[spec, st_spec],
        out_shape=[jax.ShapeDtypeStruct((n, MIX_W), F32),
                   jax.ShapeDtypeStruct(s0.shape, F32)],
        scratch_shapes=[pltpu.VMEM((N_HEADS_B, KEY_DIM_B, KEY_DIM_B), F32)],
        compiler_params=pltpu.CompilerParams(
            dimension_semantics=("parallel", "arbitrary"), vmem_limit_bytes=VMEM_LIMIT),
        name="hgrn2_scan",
    )(q, k, lf, v, s0)


def _finish_kernel(alpha, d_ff, seq_rows, x_ref, oa_ref, ob_ref, c0_ref, c1_ref,
                   wg_ref, ng_ref, wa_ref, wb_ref, wo_ref, l1g_ref, l1b_ref, wu_ref,
                   cw_ref, cb_ref, wd_ref, l2g_ref, l2b_ref, out_ref, conv_ref, *scratch):
    tm, d_model = x_ref.shape
    n_grp = FINISH_ROW_GROUPS if tm % (FINISH_ROW_GROUPS * LANES) == 0 else 1
    grp = [slice(i * (tm // n_grp), (i + 1) * (tm // n_grp)) for i in range(n_grp)]
    xs = [x_ref[r, :] for r in grp]
    gates = [_dot(x.astype(BF16), wg_ref[...]) for x in xs]

    def hgrn_out(r, gt):
        ob = ob_ref[r, :]
        gb = gt[:, :MIX_W]
        normed = []
        for h in range(N_HEADS_B):
            oh = ob[:, h * KEY_DIM_B:(h + 1) * KEY_DIM_B]
            ms = jnp.mean(oh * oh, axis=-1, keepdims=True)
            normed.append(oh * lax.rsqrt(ms + RMS_EPS) * ng_ref[...])
        return (jnp.concatenate(normed, axis=1) * (gb * jax.nn.sigmoid(gb))).astype(BF16)

    obn = [hgrn_out(r, gt) for r, gt in zip(grp, gates)]
    br_a = [_dot(oa_ref[r, :], wa_ref[...]) for r in grp]
    br_b = [_dot(o, wb_ref[...]) for o in obn]
    merged = [(jax.nn.sigmoid(gt[:, MIX_W:MIX_W + d_model]) * a
               + jax.nn.sigmoid(gt[:, MIX_W + d_model:]) * b).astype(BF16)
              for gt, a, b in zip(gates, br_a, br_b)]
    res = [_dot(m, wo_ref[...]) for m in merged]
    h1 = jnp.concatenate([_layer_norm(alpha * x + r, l1g_ref[...], l1b_ref[...])
                          for x, r in zip(xs, res)], axis=0)

    hb = h1.astype(BF16)
    chunks = [(c, min(c + FFN_CHUNK, d_ff)) for c in range(0, d_ff, FFN_CHUNK)]

    def up(lo, hi):
        return _dot(hb, wu_ref[:, lo:hi]), _dot(hb, wu_ref[:, d_ff + lo:d_ff + hi])

    if seq_rows is None:
        (a_ref,) = scratch

        @pl.when(pl.program_id(1) == 0)
        def _():
            a_ref[6:8, :] = c0_ref[...]

    y = None
    nxt = up(*chunks[0])
    for i, (lo, hi) in enumerate(chunks):
        a, g = nxt
        if i + 1 < len(chunks):
            nxt = up(*chunks[i + 1])
        if seq_rows is None:
            a_ref[8:8 + tm, lo:hi] = a
            a1 = a_ref[7:7 + tm, lo:hi]
            a2 = a_ref[6:6 + tm, lo:hi]
            tail = a_ref[6 + tm:8 + tm, lo:hi]
            a_ref[6:8, lo:hi] = tail
            conv_ref[:, lo:hi] = tail
        else:
            pos = lax.broadcasted_iota(jnp.int32, a.shape, 0) % seq_rows
            a1 = jnp.where(pos >= 1, pltpu.roll(a, 1, 0), 0.0) + c1_ref[:, lo:hi]
            a2 = jnp.where(pos >= 2, pltpu.roll(a, 2, 0), 0.0) + c0_ref[:, lo:hi]
            conv_ref[:, lo:hi] = a
        c = (cb_ref[:, lo:hi] + a2 * cw_ref[0:1, lo:hi] + a1 * cw_ref[1:2, lo:hi]
             + a * cw_ref[2:3, lo:hi])
        act = 0.5 * c * (1.0 + lax.erf(c * (2.0 ** -0.5))) * g
        yi = _dot(act.astype(BF16), wd_ref[lo:hi, :])
        y = yi if y is None else y + yi
    out_ref[...] = _layer_norm(alpha * h1 + y, l2g_ref[...], l2b_ref[...])


def _finish(x, oa, ob, c0, c1, weights, alpha, n_seq, seq_tiles, tm, seq_rows):
    n, d_model = x.shape
    d_ff = weights[-3].shape[0]
    row = lambda s, t: (s * seq_tiles + t, 0)
    const = lambda s, t: (0, 0)
    once = pl.Buffered(1)

    def wspec(w):
        return pl.BlockSpec(w.shape, const, pipeline_mode=once)

    if seq_rows is None:
        c_specs = [pl.BlockSpec((None, CONV_W - 1, d_ff), lambda s, t: (s, 0, 0))] * 2
        conv_shape = jax.ShapeDtypeStruct((n_seq, CONV_W - 1, d_ff), F32)
        conv_spec = pl.BlockSpec((None, CONV_W - 1, d_ff), lambda s, t: (s, 0, 0))
        scratch = [pltpu.VMEM((tm + 8, d_ff), F32)]
    else:
        c_specs = [pl.BlockSpec((tm, d_ff), row)] * 2
        conv_shape = jax.ShapeDtypeStruct((n, d_ff), F32)
        conv_spec = pl.BlockSpec((tm, d_ff), row)
        scratch = []
    return pl.pallas_call(
        functools.partial(_finish_kernel, alpha, d_ff, seq_rows),
        grid=(n_seq, seq_tiles),
        in_specs=[pl.BlockSpec((tm, d_model), row),
                  pl.BlockSpec((tm, MIX_W), row),
                  pl.BlockSpec((tm, MIX_W), row)] + c_specs + [wspec(w) for w in weights],
        out_specs=[pl.BlockSpec((tm, d_model), row), conv_spec],
        out_shape=[jax.ShapeDtypeStruct((n, d_model), F32), conv_shape],
        scratch_shapes=scratch,
        compiler_params=pltpu.CompilerParams(
            dimension_semantics=("parallel", "arbitrary"), vmem_limit_bytes=VMEM_LIMIT),
        name="finish_layer",
    )(x, oa, ob, c0, c1, *weights)


MIXER_TILE = 512
HGRN_TILE = 1024
HGRN_STEP = 16
HGRN_UNROLL = 8
FINISH_TILE = 512
FINISH_ROW_GROUPS = 2
FFN_CHUNK = 512
PAGES_PER_STEP = 16
NEW_KEY_PAD = 16


def _rope_tables(pos):
    half = HEAD_DIM_A // 2
    inv = ROPE_THETA ** (-jnp.arange(half, dtype=F32) * 2.0 / HEAD_DIM_A)
    ang = pos.astype(F32)[:, None] * inv[None, :]
    cos = jnp.cos(ang)
    sin = jnp.sin(ang)
    reps = LANES // HEAD_DIM_A
    return (jnp.tile(jnp.concatenate([cos, cos], axis=1), (1, reps)),
            jnp.tile(jnp.concatenate([-sin, sin], axis=1), (1, reps)))


def kernel(x_prompt, x_sample, cache_k, cache_v, state_hgrn, state_conv, page_table, w_in, lambda_q1, lambda_k1, lambda_q2, lambda_k2, subln_g, lb_logits, hgrn_norm_g, w_branch_a, w_branch_b, w_out, ln1_g, ln1_b, w_up, conv_w, conv_b, w_down, ln2_g, ln2_b):
    bp, tp, d_model = x_prompt.shape
    bs, ts, _ = x_sample.shape
    depth = w_in.shape[0]
    n_phys, _, page, _, _, _ = cache_k.shape
    past_len = page_table.shape[1] * page
    d_ff = w_down.shape[1]
    alpha = (2 * depth) ** 0.25
    n_mix = 6 * MIX_W
    ts_pad = 8
    lb_logits = lb_logits.astype(F32)

    cos_p, sin_p = _rope_tables(jnp.arange(tp))
    cos_s, sin_s = _rope_tables(past_len + jnp.arange(bs * ts) % ts)
    cache_kt = jnp.transpose(cache_k, (0, 1, 3, 4, 5, 2)).reshape(n_phys, depth, MIX_W, page)
    cache_vr = cache_v.reshape(n_phys, depth, page * N_HEADS_A, HEAD_W)

    xp = x_prompt.reshape(bp * tp, d_model)
    xs = x_sample.reshape(bs * ts, d_model)
    outs = {name: [] for name in ("kp", "vp", "sp", "cp", "ks", "vs", "ss", "cs")}
    for l in range(depth):
        lam_init = 0.8 - 0.6 * math.exp(-0.3 * l)
        w_mix = w_in[l, :, :n_mix].astype(BF16)
        lam_vecs = jnp.stack([lambda_q1[l], lambda_k1[l], lambda_q2[l], lambda_k2[l]]).astype(F32)
        g_row = subln_g[l].reshape(1, HEAD_W).astype(F32)
        weights = (
            w_in[l, :, n_mix:].astype(BF16),
            hgrn_norm_g[l].reshape(1, KEY_DIM_B).astype(F32),
            w_branch_a[l].astype(BF16), w_branch_b[l].astype(BF16), w_out[l].astype(BF16),
            ln1_g[l].reshape(1, d_model), ln1_b[l].reshape(1, d_model),
            w_up[l].astype(BF16), conv_w[l], conv_b[l].reshape(1, d_ff),
            w_down[l].astype(BF16), ln2_g[l].reshape(1, d_model), ln2_b[l].reshape(1, d_model))

        k_p, v_p, qt3, k_bf, vt3, qb, kk, lf, ib = _mixer_proj(
            xp, w_mix, cos_p, sin_p, lb_logits, l, True, MIXER_TILE)
        oa = _prompt_attention(qt3, k_bf, vt3, lam_vecs, g_row.reshape(HEAD_W, 1), lam_init,
                               bp, tp, MIXER_TILE)
        s0 = jnp.zeros((bp, N_HEADS_B, KEY_DIM_B, KEY_DIM_B), F32)
        ob, s_p = _hgrn_scan(qb, kk, lf, ib, s0, bp, tp, HGRN_TILE, HGRN_STEP)
        conv0 = jnp.zeros((bp, CONV_W - 1, d_ff), F32)
        xp_new, conv_p = _finish(xp, oa, ob, conv0, conv0, weights, alpha,
                                 bp, tp // FINISH_TILE, FINISH_TILE, None)
        outs["kp"].append(jnp.transpose(k_p.reshape(bp, N_HEADS_A, 2, HEAD_DIM_A, tp), (0, 4, 1, 2, 3)))
        outs["vp"].append(v_p.reshape(bp, tp, N_HEADS_A, HEAD_W))
        outs["sp"].append(s_p)
        outs["cp"].append(conv_p)

        k_s, v_s, q_s, qb, kk, lf, ib = _mixer_proj(
            xs, w_mix, cos_s, sin_s, lb_logits, l, False, bs * ts)
        pad_t = lambda a, rows: jnp.pad(a.reshape(bs, ts, MIX_W), ((0, 0), (0, rows - ts), (0, 0)))
        q_rep = jnp.tile(q_s.reshape(bs, ts, MIX_W), (1, 2 * N_HEADS_A, 1))
        oa = _sample_attention(page_table, q_rep, pad_t(k_s, NEW_KEY_PAD), pad_t(v_s, NEW_KEY_PAD),
                               cache_kt, cache_vr, lam_vecs, g_row, lam_init, l, ts, PAGES_PER_STEP)
        flat = lambda a: pad_t(a, ts_pad).reshape(bs * ts_pad, MIX_W)
        ob, s_s = _hgrn_scan(flat(qb), flat(kk), flat(lf), flat(ib), state_hgrn[:, l],
                             bs, ts_pad, ts_pad, ts_pad)
        ob = ob.reshape(bs, ts_pad, MIX_W)[:, :ts].reshape(bs * ts, MIX_W)
        st = state_conv[:, l]
        c0 = jnp.concatenate([st, jnp.zeros((bs, ts - 2, d_ff), F32)], axis=1)
        c1 = jnp.concatenate([st[:, 1:], jnp.zeros((bs, ts - 1, d_ff), F32)], axis=1)
        xs_new, a_s = _finish(xs, oa.reshape(bs * ts, MIX_W), ob,
                              c0.reshape(bs * ts, d_ff), c1.reshape(bs * ts, d_ff),
                              weights, alpha, 1, 1, bs * ts, ts)
        conv_s = jnp.concatenate([st, a_s.reshape(bs, ts, d_ff)], axis=1)[:, -(CONV_W - 1):]
        outs["ks"].append(k_s.reshape(bs, ts, N_HEADS_A, 2, HEAD_DIM_A))
        outs["vs"].append(v_s.reshape(bs, ts, N_HEADS_A, HEAD_W))
        outs["ss"].append(s_s)
        outs["cs"].append(conv_s)
        xp, xs = xp_new, xs_new

    stack = lambda name: jnp.stack(outs[name], axis=1)
    return (xp.reshape(bp, tp, d_model), xs.reshape(bs, ts, d_model),
            stack("kp"), stack("vp"), stack("sp"), stack("cp"),
            stack("ks"), stack("vs"), stack("ss"), stack("cs"))
```

```python
import functools
import math

import jax
import jax.numpy as jnp
from jax import lax
from jax.experimental import pallas as pl
from jax.experimental.pallas import tpu as pltpu

F32 = jnp.float32
BF16 = jnp.bfloat16

N_HEADS_A = 4
HEAD_DIM_A = 64
N_HEADS_B = 4
KEY_DIM_B = 128
CONV_W = 3
ROPE_THETA = 10000.0
LN_EPS = 1e-5
RMS_EPS = 1e-5

LANES = 128
HEAD_W = 2 * HEAD_DIM_A
MIX_W = N_HEADS_A * HEAD_W
NEG_BIG = -1e30
SUM_ROWS = 16

VMEM_LIMIT = 56 * 1024 * 1024


def _dot(a, b):
    return jnp.dot(a, b, preferred_element_type=F32)


def _dot_nt(a, b):
    return lax.dot_general(a, b, (((1,), (1,)), ((), ())), preferred_element_type=F32)


def _dot_tn(a, b):
    return lax.dot_general(a, b, (((0,), (0,)), ((), ())), preferred_element_type=F32)


def _layer_norm(x, g, b):
    mu = jnp.mean(x, axis=-1, keepdims=True)
    xc = x - mu
    var = jnp.mean(xc * xc, axis=-1, keepdims=True)
    return xc * lax.rsqrt(var + LN_EPS) * g + b


def _lambda(lam_ref, lam_init):
    lv = lam_ref[...]
    e1 = jnp.exp(jnp.sum(lv[0:1] * lv[1:2], axis=-1, keepdims=True))
    e2 = jnp.exp(jnp.sum(lv[2:3] * lv[3:4], axis=-1, keepdims=True))
    return e1 - e2 + lam_init


def _rope(x, cos, sin_signed):
    lane = lax.broadcasted_iota(jnp.int32, x.shape, 1)
    first_half = (lane % HEAD_DIM_A) < (HEAD_DIM_A // 2)
    partner = jnp.where(first_half, pltpu.roll(x, LANES - HEAD_DIM_A // 2, 1),
                        pltpu.roll(x, HEAD_DIM_A // 2, 1))
    return x * cos + partner * sin_signed


def _mixer_kernel(layer, transposed, x_ref, w_ref, cos_ref, sin_ref, lbl_ref, *out_refs):
    if transposed:
        k_ref, v_ref, qt_ref, kb_ref, vt_ref, qb_ref, kk_ref, lf_ref, ib_ref = out_refs
    else:
        k_ref, v_ref, q_ref, qb_ref, kk_ref, lf_ref, ib_ref = out_refs
    xb = x_ref[...].astype(BF16)
    cos = cos_ref[...]
    sin = sin_ref[...]

    def section(i):
        return _dot(xb, w_ref[:, i * MIX_W:(i + 1) * MIX_W])

    scale = HEAD_DIM_A ** -0.5 * math.log2(math.e)
    qa = section(0)
    qa = jnp.concatenate([_rope(qa[:, h * HEAD_W:(h + 1) * HEAD_W], cos, sin) * scale
                          for h in range(N_HEADS_A)], axis=1)
    ka = section(1)
    ka = jnp.concatenate([_rope(ka[:, h * HEAD_W:(h + 1) * HEAD_W], cos, sin)
                          for h in range(N_HEADS_A)], axis=1)
    va = section(2)
    if transposed:
        k_ref[...] = ka.T
        for h in range(N_HEADS_A):
            v_ref[pl.ds(h, va.shape[0], stride=N_HEADS_A), :] = va[:, h * HEAD_W:(h + 1) * HEAD_W]
        qt_ref[...] = qa.T.astype(BF16)
        kb_ref[...] = ka.astype(BF16)
        vt_ref[...] = va.T.astype(BF16)
    else:
        k_ref[...] = ka
        v_ref[...] = va
        q_ref[...] = qa

    qb_ref[...] = section(3)
    fr = section(4)
    ib_ref[...] = section(5)

    lg = lbl_ref[...]
    e = jnp.exp(lg - jnp.max(lg, axis=0, keepdims=True))
    lb = jnp.sum(e[:layer + 1], axis=0, keepdims=True) / jnp.sum(e, axis=0, keepdims=True)
    f = lb + (1.0 - lb) * (1.0 / (1.0 + jnp.exp(-fr)))
    lf_ref[...] = jnp.log2(f)
    kk_ref[...] = jnp.log2((1.0 - lb) * (1.0 / (1.0 + jnp.exp(fr))))


def _mixer_proj(x, w_mix, cos, sin, lb_logits, layer, transposed, tm):
    n = x.shape[0]
    d = x.shape[1]
    n_tiles = n // tm
    pos_tiles = cos.shape[0] // tm
    row = lambda i: (i, 0)
    f32_out = jax.ShapeDtypeStruct((n, MIX_W), F32)
    row_spec = pl.BlockSpec((tm, MIX_W), row)
    if transposed:
        t_shape = jax.ShapeDtypeStruct((n_tiles, MIX_W, tm), BF16)
        t_spec = pl.BlockSpec((None, MIX_W, tm), lambda i: (i, 0, 0))
        kt_shape = jax.ShapeDtypeStruct((n // cos.shape[0], MIX_W, cos.shape[0]), F32)
        kt_spec = pl.BlockSpec((None, MIX_W, tm), lambda i: (i // pos_tiles, 0, i % pos_tiles))
        v_shape = jax.ShapeDtypeStruct((n * N_HEADS_A, HEAD_W), F32)
        v_spec = pl.BlockSpec((tm * N_HEADS_A, HEAD_W), row)
        out_shape = [kt_shape, v_shape, t_shape, jax.ShapeDtypeStruct((n, MIX_W), BF16), t_shape,
                     f32_out, f32_out, f32_out, f32_out]
        out_specs = [kt_spec, v_spec, t_spec, row_spec, t_spec,
                     row_spec, row_spec, row_spec, row_spec]
    else:
        out_shape = [f32_out] * 7
        out_specs = [row_spec] * 7
    return pl.pallas_call(
        functools.partial(_mixer_kernel, layer, transposed),
        grid=(n_tiles,),
        in_specs=[
            pl.BlockSpec((tm, d), row),
            pl.BlockSpec(w_mix.shape, lambda i: (0, 0)),
            pl.BlockSpec((tm, LANES), lambda i: (i % pos_tiles, 0)),
            pl.BlockSpec((tm, LANES), lambda i: (i % pos_tiles, 0)),
            pl.BlockSpec(lb_logits.shape, lambda i: (0, 0)),
        ],
        out_specs=out_specs,
        out_shape=out_shape,
        compiler_params=pltpu.CompilerParams(
            dimension_semantics=("parallel",), vmem_limit_bytes=VMEM_LIMIT),
        name="mixer_proj",
    )(x, w_mix, cos, sin, lb_logits)


def _prompt_attn_kernel(lam_init, tq, qt_ref, k_ref, vt_ref, lam_ref, g_ref, o_ref,
                        m_ref, acc_ref, sa_ref, sb_ref):
    qi = pl.program_id(2)
    qt = qt_ref[...]
    qrow = lax.broadcasted_iota(jnp.int32, qt.shape, 0)
    zero = jnp.zeros_like(qt)
    qmaps = (jnp.where(qrow < HEAD_DIM_A, qt, zero), jnp.where(qrow >= HEAD_DIM_A, qt, zero))

    m_ref[...] = jnp.full(m_ref.shape, NEG_BIG, F32)
    acc_ref[...] = jnp.zeros(acc_ref.shape, F32)
    ones = jnp.ones((SUM_ROWS, tq), BF16)

    def scores(j, s_ref):
        kj = k_ref[pl.ds(pl.multiple_of(j * tq, tq), tq), :]
        for mp in range(2):
            s_ref[mp] = _dot(kj, qmaps[mp])

    def consume(j, s_ref, masked):
        vj = jnp.concatenate([vt_ref[j], ones], axis=0)
        for mp in range(2):
            s = s_ref[mp]
            if masked:
                krow = lax.broadcasted_iota(jnp.int32, s.shape, 0)
                qcol = lax.broadcasted_iota(jnp.int32, s.shape, 1)
                s = jnp.where(krow <= qcol, s, NEG_BIG)
            m_old = m_ref[mp]
            m_new = jnp.maximum(m_old, jnp.max(s, axis=0, keepdims=True))
            alpha = jnp.exp2(m_old - m_new)
            p = jnp.exp2(s - m_new).astype(BF16)
            acc_ref[mp] = alpha * acc_ref[mp] + _dot(vj, p)
            m_ref[mp] = m_new

    def pair(j):
        scores(j + 1, sb_ref)
        consume(j, sa_ref, False)
        scores(j + 2, sa_ref)
        consume(j + 1, sb_ref, False)

    def body4(jj, carry):
        pair(4 * jj)
        pair(4 * jj + 2)
        return carry

    def body2(jj, carry):
        pair(4 * (qi // 4) + 2 * jj)
        return carry

    scores(0, sa_ref)
    lax.fori_loop(0, qi // 4, body4, 0)
    lax.fori_loop(0, (qi % 4) // 2, body2, 0)

    @pl.when(qi % 2 == 0)
    def _():
        consume(qi, sa_ref, True)

    @pl.when(qi % 2 == 1)
    def _():
        scores(qi, sb_ref)
        consume(qi - 1, sa_ref, False)
        consume(qi, sb_ref, True)

    lam = _lambda(lam_ref, lam_init)
    acc1 = acc_ref[0]
    acc2 = acc_ref[1]
    o = (acc1[:HEAD_W] / acc1[HEAD_W:HEAD_W + 1]
         - lam * (acc2[:HEAD_W] / acc2[HEAD_W:HEAD_W + 1]))
    ms = jnp.mean(o * o, axis=0, keepdims=True)
    o = o * lax.rsqrt(ms + RMS_EPS) * g_ref[...] * (1.0 - lam_init)
    o_ref[...] = o.T.astype(o_ref.dtype)


def _prompt_attention(qt3, k_bf, vt3, lam_vecs, g_col, lam_init, batch, seq, tq):
    nq = seq // tq
    n = batch * seq
    return pl.pallas_call(
        functools.partial(_prompt_attn_kernel, lam_init, tq),
        grid=(batch, N_HEADS_A, nq),
        in_specs=[
            pl.BlockSpec((None, HEAD_W, tq), lambda b, h, i: (b * nq + i, h, 0)),
            pl.BlockSpec((seq, HEAD_W), lambda b, h, i: (b, h)),
            pl.BlockSpec((nq, HEAD_W, tq), lambda b, h, i: (b, h, 0)),
            pl.BlockSpec(lam_vecs.shape, lambda b, h, i: (0, 0)),
            pl.BlockSpec(g_col.shape, lambda b, h, i: (0, 0)),
        ],
        out_specs=pl.BlockSpec((tq, HEAD_W), lambda b, h, i: (b * nq + i, h)),
        out_shape=jax.ShapeDtypeStruct((n, MIX_W), BF16),
        scratch_shapes=[
            pltpu.VMEM((2, 1, tq), F32),
            pltpu.VMEM((2, HEAD_W + SUM_ROWS, tq), F32),
            pltpu.VMEM((2, tq, tq), F32),
            pltpu.VMEM((2, tq, tq), F32),
        ],
        compiler_params=pltpu.CompilerParams(
            dimension_semantics=("parallel", "parallel", "arbitrary"),
            vmem_limit_bytes=VMEM_LIMIT),
        name="prompt_diff_attention",
    )(qt3, k_bf, vt3, lam_vecs, g_col)


class _DecodeState:
    def __init__(self, q_ref, m_ref, l_ref, acc_ref, dec_seq):
        self.m_ref, self.l_ref, self.acc_ref, self.dec_seq = m_ref, l_ref, acc_ref, dec_seq
        qrep = q_ref[...]
        rrow = lax.broadcasted_iota(jnp.int32, qrep.shape, 0)
        rcol = lax.broadcasted_iota(jnp.int32, qrep.shape, 1)
        self.qbd = jnp.where(rcol // HEAD_DIM_A == rrow // dec_seq, qrep, 0.0).astype(BF16)

    def init(self):
        self.m_ref[...] = jnp.full(self.m_ref.shape, NEG_BIG, F32)
        self.l_ref[...] = jnp.zeros(self.l_ref.shape, F32)
        self.acc_ref[...] = jnp.zeros(self.acc_ref.shape, F32)

    def update(self, s, v_bf):
        m_old = self.m_ref[...]
        m_new = jnp.maximum(m_old, jnp.max(s, axis=-1, keepdims=True))
        alpha = jnp.exp2(m_old - m_new)
        p = jnp.exp2(s - m_new)
        self.l_ref[...] = alpha * self.l_ref[...] + jnp.sum(p, axis=-1, keepdims=True)
        self.acc_ref[...] = alpha * self.acc_ref[...] + _dot(p.astype(BF16), v_bf)
        self.m_ref[...] = m_new

    def pages(self, k_refs, v_refs):
        page = k_refs[0].shape[1]
        kt = jnp.concatenate([r[...] for r in k_refs], axis=1).astype(BF16)
        vp = jnp.concatenate(
            [jnp.concatenate([r[pl.ds(h, page, stride=N_HEADS_A), :] for h in range(N_HEADS_A)],
                             axis=1) for r in v_refs], axis=0).astype(BF16)
        self.update(_dot(self.qbd, kt), vp)

    def finish(self, kn_ref, vn_ref, lam_ref, g_ref, o_ref, lam_init):
        dec_seq = self.dec_seq
        kn = kn_ref[...].astype(BF16)
        vn = vn_ref[...].astype(BF16)
        s = _dot_nt(self.qbd, kn)
        key = lax.broadcasted_iota(jnp.int32, s.shape, 1)
        tok = lax.broadcasted_iota(jnp.int32, s.shape, 0) % dec_seq
        self.update(jnp.where(key <= tok, s, NEG_BIG), vn)
        lam = _lambda(lam_ref, lam_init)
        acc = self.acc_ref[...] / self.l_ref[...]
        rows_per_head = 2 * dec_seq
        outs = []
        for h in range(N_HEADS_A):
            blk = acc[h * rows_per_head:(h + 1) * rows_per_head, h * HEAD_W:(h + 1) * HEAD_W]
            o = blk[:dec_seq] - lam * blk[dec_seq:]
            ms = jnp.mean(o * o, axis=-1, keepdims=True)
            outs.append(o * lax.rsqrt(ms + RMS_EPS) * g_ref[...] * (1.0 - lam_init))
        o_ref[...] = jnp.concatenate(outs, axis=1).astype(o_ref.dtype)


def _hgrn_steps(step, q_ref, k_ref, lf_ref, v_ref, o_ref, st_ref):
    ones = jnp.ones((KEY_DIM_B, LANES), BF16)
    rowi = lax.broadcasted_iota(jnp.int32, (step, KEY_DIM_B), 0)

    def advance(r0):
        for h in range(N_HEADS_B):
            cols = slice(h * KEY_DIM_B, (h + 1) * KEY_DIM_B)
            q = q_ref[pl.ds(r0, step), cols]
            lk = k_ref[pl.ds(r0, step), cols]
            v = v_ref[pl.ds(r0, step), cols]
            b = lf_ref[pl.ds(r0, step), cols]
            d = 1
            while d < step:
                b = b + jnp.where(rowi >= d, pltpu.roll(b, d, 0), 0.0)
                d *= 2
            st = st_ref[h]
            o = _dot_nt((q * jnp.exp2(b)).astype(BF16), st.astype(BF16))
            c = b - lk
            xs = []
            for s in range(step):
                lo = (s // 8) * 8
                e = jnp.exp2(jnp.where(rowi[lo:] >= s, b[lo:] - c[s:s + 1], NEG_BIG))
                xs.append(q[lo:] * e)
            att = _dot(jnp.concatenate(xs, axis=0).astype(BF16), ones)
            parts = [o[lo:lo + 8] for lo in range(0, step, 8)]
            off = 0
            for s in range(step):
                lo = (s // 8) * 8
                for g in range(lo // 8, step // 8):
                    parts[g] = parts[g] + att[off:off + 8] * v[s:s + 1]
                    off += 8
            o_ref[pl.ds(r0, step), cols] = jnp.concatenate(parts, axis=0)
            b_last = b[step - 1:step]
            kd = jnp.exp2(b_last - c)
            st_ref[h] = st * jnp.exp2(b_last) + _dot_tn(v.astype(BF16), kd.astype(BF16))

    return advance


def _hgrn_state_io(s0_ref, sout_ref, st_ref):
    ti = pl.program_id(1)

    def load():
        @pl.when(ti == 0)
        def _():
            for h in range(N_HEADS_B):
                st_ref[h] = s0_ref[h].T

    def store():
        @pl.when(ti == pl.num_programs(1) - 1)
        def _():
            for h in range(N_HEADS_B):
                sout_ref[h] = st_ref[h].T

    return load, store


def _hgrn_kernel(step, n_steps, q_ref, k_ref, lf_ref, v_ref, s0_ref, o_ref, sout_ref, st_ref):
    load, store = _hgrn_state_io(s0_ref, sout_ref, st_ref)
    advance = _hgrn_steps(step, q_ref, k_ref, lf_ref, v_ref, o_ref, st_ref)
    load()

    def body(i, carry):
        advance(pl.multiple_of(i * step, step))
        return carry

    lax.fori_loop(0, n_steps, body, 0, unroll=math.gcd(n_steps, HGRN_UNROLL))
    store()


def _hgrn_scan(q, k, lf, v, s0, n_seq, seq_rows, tile_rows, step):
    n = q.shape[0]
    tiles = seq_rows // tile_rows
    row = lambda s, t: (s * tiles + t, 0)
    spec = pl.BlockSpec((tile_rows, MIX_W), row)
    st_spec = pl.BlockSpec((None, N_HEADS_B, KEY_DIM_B, KEY_DIM_B), lambda s, t: (s, 0, 0, 0))
    return pl.pallas_call(
        functools.partial(_hgrn_kernel, step, tile_rows // step),
        grid=(n_seq, tiles),
        in_specs=[spec, spec, spec, spec, st_spec],
        out_specs=[spec, st_spec],
        out_shape=[jax.ShapeDtypeStruct((n, MIX_W), F32),
                   jax.ShapeDtypeStruct(s0.shape, F32)],
        scratch_shapes=[pltpu.VMEM((N_HEADS_B, KEY_DIM_B, KEY_DIM_B), F32)],
        compiler_params=pltpu.CompilerParams(
            dimension_semantics=("parallel", "arbitrary"), vmem_limit_bytes=VMEM_LIMIT),
        name="hgrn2_scan",
    )(q, k, lf, v, s0)


def _hgrn_decode_kernel(lam_init, step, n_steps, n_pages, steps_per_sample, dec_seq, pt_ref,
                        q_ref, k_ref, lf_ref, v_ref, s0_ref,
                        qs_ref, kn_ref, vn_ref, lam_ref, g_ref, *rest):
    k_pages = rest[:n_pages]
    v_pages = rest[n_pages:2 * n_pages]
    o_ref, sout_ref, oa_ref = rest[2 * n_pages:2 * n_pages + 3]
    st_ref, m_ref, l_ref, acc_ref = rest[2 * n_pages + 3:]
    phase = pl.program_id(1) % steps_per_sample

    load, store = _hgrn_state_io(s0_ref, sout_ref, st_ref)
    advance = _hgrn_steps(step, q_ref, k_ref, lf_ref, v_ref, o_ref, st_ref)
    dec = _DecodeState(qs_ref, m_ref, l_ref, acc_ref, dec_seq)
    load()
    pl.when(phase == 0)(dec.init)

    n_groups = math.gcd(n_steps, n_pages)
    for i in range(n_groups):
        lo, hi = i * n_pages // n_groups, (i + 1) * n_pages // n_groups
        dec.pages(k_pages[lo:hi], v_pages[lo:hi])
        for r in range(i * n_steps // n_groups, (i + 1) * n_steps // n_groups):
            advance(r * step)

    store()

    @pl.when(phase == steps_per_sample - 1)
    def _():
        dec.finish(kn_ref, vn_ref, lam_ref, g_ref, oa_ref, lam_init)


def _hgrn_scan_with_decode(q, k, lf, v, s0, n_seq, seq_rows, step,
                           page_table, q_rep, k_new, v_new, cache_kt, cache_v, lam_vecs, g_row,
                           lam_init, layer, dec_seq, pages_per_step):
    n = q.shape[0]
    bs, n_pages = page_table.shape
    page = cache_kt.shape[3]
    n_rows = q_rep.shape[1]
    steps_per_sample = n_pages // pages_per_step
    total_steps = bs * steps_per_sample
    tiles = total_steps // n_seq
    tile_rows = seq_rows // tiles
    assert tiles * n_seq == total_steps and tile_rows * tiles == seq_rows
    assert tile_rows % step == 0 and tiles % steps_per_sample == 0

    row = lambda s, t, pt: (s * tiles + t, 0)
    spec = pl.BlockSpec((tile_rows, MIX_W), row)
    st_spec = pl.BlockSpec((None, N_HEADS_B, KEY_DIM_B, KEY_DIM_B), lambda s, t, pt: (s, 0, 0, 0))
    sample = lambda s, t, pt: ((s * tiles + t) // steps_per_sample, 0, 0)
    const = lambda s, t, pt: (0, 0)

    def page_index(pg):
        return lambda s, t, pt: (pt[s * tiles + t, pg], layer, 0, 0)

    k_specs = [pl.BlockSpec((None, None, MIX_W, page), page_index(pg))
               for pg in range(pages_per_step)]
    v_specs = [pl.BlockSpec((None, None, page * N_HEADS_A, HEAD_W), page_index(pg))
               for pg in range(pages_per_step)]
    grid_spec = pltpu.PrefetchScalarGridSpec(
        num_scalar_prefetch=1,
        grid=(n_seq, tiles),
        in_specs=[spec, spec, spec, spec, st_spec,
                  pl.BlockSpec((None, n_rows, MIX_W), sample),
                  pl.BlockSpec((None, k_new.shape[1], MIX_W), sample),
                  pl.BlockSpec((None, k_new.shape[1], MIX_W), sample),
                  pl.BlockSpec(lam_vecs.shape, const),
                  pl.BlockSpec(g_row.shape, const)] + k_specs + v_specs,
        out_specs=[spec, st_spec, pl.BlockSpec((None, dec_seq, MIX_W), sample)],
        scratch_shapes=[
            pltpu.VMEM((N_HEADS_B, KEY_DIM_B, KEY_DIM_B), F32),
            pltpu.VMEM((n_rows, 1), F32),
            pltpu.VMEM((n_rows, 1), F32),
            pltpu.VMEM((n_rows, MIX_W), F32),
        ],
    )
    return pl.pallas_call(
        functools.partial(_hgrn_decode_kernel, lam_init, step, tile_rows // step,
                          pages_per_step, steps_per_sample, dec_seq),
        grid_spec=grid_spec,
        out_shape=[jax.ShapeDtypeStruct((n, MIX_W), F32),
                   jax.ShapeDtypeStruct(s0.shape, F32),
                   jax.ShapeDtypeStruct((bs, dec_seq, MIX_W), BF16)],
        compiler_params=pltpu.CompilerParams(
            dimension_semantics=("arbitrary", "arbitrary"), vmem_limit_bytes=VMEM_LIMIT),
        name="hgrn2_scan_with_decode_attention",
    )(page_table.reshape(total_steps, pages_per_step), q, k, lf, v, s0, q_rep, k_new, v_new,
      lam_vecs, g_row, *([cache_kt] * pages_per_step), *([cache_v] * pages_per_step))


def _finish_kernel(alpha, d_ff, seq_rows, x_ref, oa_ref, ob_ref, c0_ref, c1_ref,
                   wg_ref, ng_ref, wa_ref, wb_ref, wo_ref, l1g_ref, l1b_ref, wu_ref,
                   cw_ref, cb_ref, wd_ref, l2g_ref, l2b_ref, out_ref, conv_ref, *scratch):
    tm, d_model = x_ref.shape
    n_grp = FINISH_ROW_GROUPS if tm % (FINISH_ROW_GROUPS * LANES) == 0 else 1
    grp = [slice(i * (tm // n_grp), (i + 1) * (tm // n_grp)) for i in range(n_grp)]
    xs = [x_ref[r, :] for r in grp]
    gates = [_dot(x.astype(BF16), wg_ref[...]) for x in xs]

    def hgrn_out(r, gt):
        ob = ob_ref[r, :]
        gb = gt[:, :MIX_W]
        normed = []
        for h in range(N_HEADS_B):
            oh = ob[:, h * KEY_DIM_B:(h + 1) * KEY_DIM_B]
            ms = jnp.mean(oh * oh, axis=-1, keepdims=True)
            normed.append(oh * lax.rsqrt(ms + RMS_EPS) * ng_ref[...])
        return (jnp.concatenate(normed, axis=1) * (gb * jax.nn.sigmoid(gb))).astype(BF16)

    obn = [hgrn_out(r, gt) for r, gt in zip(grp, gates)]
    br_a = [_dot(oa_ref[r, :], wa_ref[...]) for r in grp]
    br_b = [_dot(o, wb_ref[...]) for o in obn]
    merged = [(jax.nn.sigmoid(gt[:, MIX_W:MIX_W + d_model]) * a
               + jax.nn.sigmoid(gt[:, MIX_W + d_model:]) * b).astype(BF16)
              for gt, a, b in zip(gates, br_a, br_b)]
    res = [_dot(m, wo_ref[...]) for m in merged]
    h1 = jnp.concatenate([_layer_norm(alpha * x + r, l1g_ref[...], l1b_ref[...])
                          for x, r in zip(xs, res)], axis=0)

    hb = h1.astype(BF16)
    chunks = [(c, min(c + FFN_CHUNK, d_ff)) for c in range(0, d_ff, FFN_CHUNK)]

    def up(lo, hi):
        return _dot(hb, wu_ref[:, lo:hi]), _dot(hb, wu_ref[:, d_ff + lo:d_ff + hi])

    if seq_rows is None:
        (a_ref,) = scratch

        @pl.when(pl.program_id(1) == 0)
        def _():
            a_ref[6:8, :] = c0_ref[...]

    y = None
    nxt = up(*chunks[0])
    for i, (lo, hi) in enumerate(chunks):
        a, g = nxt
        if i + 1 < len(chunks):
            nxt = up(*chunks[i + 1])
        if seq_rows is None:
            a_ref[8:8 + tm, lo:hi] = a
            a1 = a_ref[7:7 + tm, lo:hi]
            a2 = a_ref[6:6 + tm, lo:hi]
            tail = a_ref[6 + tm:8 + tm, lo:hi]
            a_ref[6:8, lo:hi] = tail
            conv_ref[:, lo:hi] = tail
        else:
            pos = lax.broadcasted_iota(jnp.int32, a.shape, 0) % seq_rows
            a1 = jnp.where(pos >= 1, pltpu.roll(a, 1, 0), 0.0) + c1_ref[:, lo:hi]
            a2 = jnp.where(pos >= 2, pltpu.roll(a, 2, 0), 0.0) + c0_ref[:, lo:hi]
            conv_ref[:, lo:hi] = a
        c = (cb_ref[:, lo:hi] + a2 * cw_ref[0:1, lo:hi] + a1 * cw_ref[1:2, lo:hi]
             + a * cw_ref[2:3, lo:hi])
        act = 0.5 * c * (1.0 + lax.erf(c * (2.0 ** -0.5))) * g
        yi = _dot(act.astype(BF16), wd_ref[lo:hi, :])
        y = yi if y is None else y + yi
    out_ref[...] = _layer_norm(alpha * h1 + y, l2g_ref[...], l2b_ref[...])


def _finish(x, oa, ob, c0, c1, weights, alpha, n_seq, seq_tiles, tm, seq_rows):
    n, d_model = x.shape
    d_ff = weights[-3].shape[0]
    row = lambda s, t: (s * seq_tiles + t, 0)
    const = lambda s, t: (0, 0)
    once = pl.Buffered(1)

    def wspec(w):
        return pl.BlockSpec(w.shape, const, pipeline_mode=once)

    if seq_rows is None:
        c_specs = [pl.BlockSpec((None, CONV_W - 1, d_ff), lambda s, t: (s, 0, 0))] * 2
        conv_shape = jax.ShapeDtypeStruct((n_seq, CONV_W - 1, d_ff), F32)
        conv_spec = pl.BlockSpec((None, CONV_W - 1, d_ff), lambda s, t: (s, 0, 0))
        scratch = [pltpu.VMEM((tm + 8, d_ff), F32)]
    else:
        c_specs = [pl.BlockSpec((tm, d_ff), row)] * 2
        conv_shape = jax.ShapeDtypeStruct((n, d_ff), F32)
        conv_spec = pl.BlockSpec((tm, d_ff), row)
        scratch = []
    return pl.pallas_call(
        functools.partial(_finish_kernel, alpha, d_ff, seq_rows),
        grid=(n_seq, seq_tiles),
        in_specs=[pl.BlockSpec((tm, d_model), row),
                  pl.BlockSpec((tm, MIX_W), row),
                  pl.BlockSpec((tm, MIX_W), row)] + c_specs + [wspec(w) for w in weights],
        out_specs=[pl.BlockSpec((tm, d_model), row), conv_spec],
        out_shape=[jax.ShapeDtypeStruct((n, d_model), F32), conv_shape],
        scratch_shapes=scratch,
        compiler_params=pltpu.CompilerParams(
            dimension_semantics=("parallel", "arbitrary"), vmem_limit_bytes=VMEM_LIMIT),
        name="finish_layer",
    )(x, oa, ob, c0, c1, *weights)


MIXER_TILE = 512
HGRN_STEP = 16
HGRN_UNROLL = 8
FINISH_TILE = 512
FINISH_ROW_GROUPS = 2
FFN_CHUNK = 512
PAGES_PER_STEP = 16
NEW_KEY_PAD = 16


def _rope_tables(pos):
    half = HEAD_DIM_A // 2
    inv = ROPE_THETA ** (-jnp.arange(half, dtype=F32) * 2.0 / HEAD_DIM_A)
    ang = pos.astype(F32)[:, None] * inv[None, :]
    cos = jnp.cos(ang)
    sin = jnp.sin(ang)
    reps = LANES // HEAD_DIM_A
    return (jnp.tile(jnp.concatenate([cos, cos], axis=1), (1, reps)),
            jnp.tile(jnp.concatenate([-sin, sin], axis=1), (1, reps)))


def kernel(x_prompt, x_sample, cache_k, cache_v, state_hgrn, state_conv, page_table, w_in, lambda_q1, lambda_k1, lambda_q2, lambda_k2, subln_g, lb_logits, hgrn_norm_g, w_branch_a, w_branch_b, w_out, ln1_g, ln1_b, w_up, conv_w, conv_b, w_down, ln2_g, ln2_b):
    bp, tp, d_model = x_prompt.shape
    bs, ts, _ = x_sample.shape
    depth = w_in.shape[0]
    n_phys, _, page, _, _, _ = cache_k.shape
    past_len = page_table.shape[1] * page
    d_ff = w_down.shape[1]
    alpha = (2 * depth) ** 0.25
    n_mix = 6 * MIX_W
    ts_pad = 8
    lb_logits = lb_logits.astype(F32)

    cos_p, sin_p = _rope_tables(jnp.arange(tp))
    cos_s, sin_s = _rope_tables(past_len + jnp.arange(bs * ts) % ts)
    cache_kt = jnp.transpose(cache_k, (0, 1, 3, 4, 5, 2)).reshape(n_phys, depth, MIX_W, page)
    cache_vr = cache_v.reshape(n_phys, depth, page * N_HEADS_A, HEAD_W)

    xp = x_prompt.reshape(bp * tp, d_model)
    xs = x_sample.reshape(bs * ts, d_model)
    outs = {name: [] for name in ("kp", "vp", "sp", "cp", "ks", "vs", "ss", "cs")}
    for l in range(depth):
        lam_init = 0.8 - 0.6 * math.exp(-0.3 * l)
        w_mix = w_in[l, :, :n_mix].astype(BF16)
        lam_vecs = jnp.stack([lambda_q1[l], lambda_k1[l], lambda_q2[l], lambda_k2[l]]).astype(F32)
        g_row = subln_g[l].reshape(1, HEAD_W).astype(F32)
        weights = (
            w_in[l, :, n_mix:].astype(BF16),
            hgrn_norm_g[l].reshape(1, KEY_DIM_B).astype(F32),
            w_branch_a[l].astype(BF16), w_branch_b[l].astype(BF16), w_out[l].astype(BF16),
            ln1_g[l].reshape(1, d_model), ln1_b[l].reshape(1, d_model),
            w_up[l].astype(BF16), conv_w[l], conv_b[l].reshape(1, d_ff),
            w_down[l].astype(BF16), ln2_g[l].reshape(1, d_model), ln2_b[l].reshape(1, d_model))

        k_p, v_p, qt3, k_bf, vt3, qb, kk, lf, ib = _mixer_proj(
            xp, w_mix, cos_p, sin_p, lb_logits, l, True, MIXER_TILE)
        k_s, v_s, q_s, qb_s, kk_s, lf_s, ib_s = _mixer_proj(
            xs, w_mix, cos_s, sin_s, lb_logits, l, False, bs * ts)
        pad_t = lambda a, rows: jnp.pad(a.reshape(bs, ts, MIX_W), ((0, 0), (0, rows - ts), (0, 0)))
        q_rep = jnp.tile(q_s.reshape(bs, ts, MIX_W), (1, 2 * N_HEADS_A, 1))

        oa = _prompt_attention(qt3, k_bf, vt3, lam_vecs, g_row.reshape(HEAD_W, 1), lam_init,
                               bp, tp, MIXER_TILE)
        s0 = jnp.zeros((bp, N_HEADS_B, KEY_DIM_B, KEY_DIM_B), F32)
        ob, s_p, oa_s = _hgrn_scan_with_decode(
            qb, kk, lf, ib, s0, bp, tp, HGRN_STEP,
            page_table, q_rep, pad_t(k_s, NEW_KEY_PAD), pad_t(v_s, NEW_KEY_PAD),
            cache_kt, cache_vr, lam_vecs, g_row, lam_init, l, ts, PAGES_PER_STEP)
        conv0 = jnp.zeros((bp, CONV_W - 1, d_ff), F32)
        xp_new, conv_p = _finish(xp, oa, ob, conv0, conv0, weights, alpha,
                                 bp, tp // FINISH_TILE, FINISH_TILE, None)
        outs["kp"].append(jnp.transpose(k_p.reshape(bp, N_HEADS_A, 2, HEAD_DIM_A, tp), (0, 4, 1, 2, 3)))
        outs["vp"].append(v_p.reshape(bp, tp, N_HEADS_A, HEAD_W))
        outs["sp"].append(s_p)
        outs["cp"].append(conv_p)

        flat = lambda a: pad_t(a, ts_pad).reshape(bs * ts_pad, MIX_W)
        ob, s_s = _hgrn_scan(flat(qb_s), flat(kk_s), flat(lf_s), flat(ib_s), state_hgrn[:, l],
                             bs, ts_pad, ts_pad, ts_pad)
        ob = ob.reshape(bs, ts_pad, MIX_W)[:, :ts].reshape(bs * ts, MIX_W)
        st = state_conv[:, l]
        c0 = jnp.concatenate([st, jnp.zeros((bs, ts - 2, d_ff), F32)], axis=1)
        c1 = jnp.concatenate([st[:, 1:], jnp.zeros((bs, ts - 1, d_ff), F32)], axis=1)
        xs_new, a_s = _finish(xs, oa_s.reshape(bs * ts, MIX_W), ob,
                              c0.reshape(bs * ts, d_ff), c1.reshape(bs * ts, d_ff),
                              weights, alpha, 1, 1, bs * ts, ts)
        conv_s = jnp.concatenate([st, a_s.reshape(bs, ts, d_ff)], axis=1)[:, -(CONV_W - 1):]
        outs["ks"].append(k_s.reshape(bs, ts, N_HEADS_A, 2, HEAD_DIM_A))
        outs["vs"].append(v_s.reshape(bs, ts, N_HEADS_A, HEAD_W))
        outs["ss"].append(s_s)
        outs["cs"].append(conv_s)
        xp, xs = xp_new, xs_new

    stack = lambda name: jnp.stack(outs[name], axis=1)
    return (xp.reshape(bp, tp, d_model), xs.reshape(bs, ts, d_model),
            stack("kp"), stack("vp"), stack("sp"), stack("cp"),
            stack("ks"), stack("vs"), stack("ss"), stack("cs"))
```

```python
import functools
import math

import jax
import jax.numpy as jnp
from jax import lax
from jax.experimental import pallas as pl
from jax.experimental.pallas import tpu as pltpu

F32 = jnp.float32
BF16 = jnp.bfloat16

N_HEADS_A = 4
HEAD_DIM_A = 64
N_HEADS_B = 4
KEY_DIM_B = 128
CONV_W = 3
ROPE_THETA = 10000.0
LN_EPS = 1e-5
RMS_EPS = 1e-5

LANES = 128
HEAD_W = 2 * HEAD_DIM_A
MIX_W = N_HEADS_A * HEAD_W
NEG_BIG = -1e30
SUM_ROWS = 16

VMEM_LIMIT = 56 * 1024 * 1024


def _dot(a, b):
    return jnp.dot(a, b, preferred_element_type=F32)


def _dot_nt(a, b):
    return lax.dot_general(a, b, (((1,), (1,)), ((), ())), preferred_element_type=F32)


def _dot_tn(a, b):
    return lax.dot_general(a, b, (((0,), (0,)), ((), ())), preferred_element_type=F32)


def _layer_norm(x, g, b):
    mu = jnp.mean(x, axis=-1, keepdims=True)
    xc = x - mu
    var = jnp.mean(xc * xc, axis=-1, keepdims=True)
    return xc * lax.rsqrt(var + LN_EPS) * g + b


def _lambda(lam_ref, lam_init):
    lv = lam_ref[...]
    e1 = jnp.exp(jnp.sum(lv[0:1] * lv[1:2], axis=-1, keepdims=True))
    e2 = jnp.exp(jnp.sum(lv[2:3] * lv[3:4], axis=-1, keepdims=True))
    return e1 - e2 + lam_init


def _rope(x, cos, sin_signed):
    lane = lax.broadcasted_iota(jnp.int32, x.shape, 1)
    first_half = (lane % HEAD_DIM_A) < (HEAD_DIM_A // 2)
    partner = jnp.where(first_half, pltpu.roll(x, LANES - HEAD_DIM_A // 2, 1),
                        pltpu.roll(x, HEAD_DIM_A // 2, 1))
    return x * cos + partner * sin_signed


def _mixer_kernel(layer, transposed, x_ref, w_ref, cos_ref, sin_ref, lbl_ref, *out_refs):
    if transposed:
        k_ref, v_ref, qt_ref, kb_ref, vt_ref, qb_ref, kk_ref, lf_ref, ib_ref = out_refs
    else:
        k_ref, v_ref, q_ref, qb_ref, kk_ref, lf_ref, ib_ref = out_refs
    xb = x_ref[...].astype(BF16)
    cos = cos_ref[...]
    sin = sin_ref[...]

    def section(i):
        return _dot(xb, w_ref[:, i * MIX_W:(i + 1) * MIX_W])

    scale = HEAD_DIM_A ** -0.5 * math.log2(math.e)
    qa = section(0)
    qa = jnp.concatenate([_rope(qa[:, h * HEAD_W:(h + 1) * HEAD_W], cos, sin) * scale
                          for h in range(N_HEADS_A)], axis=1)
    ka = section(1)
    ka = jnp.concatenate([_rope(ka[:, h * HEAD_W:(h + 1) * HEAD_W], cos, sin)
                          for h in range(N_HEADS_A)], axis=1)
    va = section(2)
    if transposed:
        k_ref[...] = ka.T
        for h in range(N_HEADS_A):
            v_ref[pl.ds(h, va.shape[0], stride=N_HEADS_A), :] = va[:, h * HEAD_W:(h + 1) * HEAD_W]
        qt_ref[...] = qa.T.astype(BF16)
        kb_ref[...] = ka.astype(BF16)
        vt_ref[...] = va.T.astype(BF16)
    else:
        k_ref[...] = ka
        v_ref[...] = va
        q_ref[...] = qa

    qb_ref[...] = section(3)
    fr = section(4)
    ib_ref[...] = section(5)

    lg = lbl_ref[...]
    e = jnp.exp(lg - jnp.max(lg, axis=0, keepdims=True))
    lb = jnp.sum(e[:layer + 1], axis=0, keepdims=True) / jnp.sum(e, axis=0, keepdims=True)
    f = lb + (1.0 - lb) * (1.0 / (1.0 + jnp.exp(-fr)))
    lf_ref[...] = jnp.log2(f)
    kk_ref[...] = jnp.log2((1.0 - lb) * (1.0 / (1.0 + jnp.exp(fr))))


def _mixer_proj(x, w_mix, cos, sin, lb_logits, layer, transposed, tm):
    n = x.shape[0]
    d = x.shape[1]
    n_tiles = n // tm
    pos_tiles = cos.shape[0] // tm
    row = lambda i: (i, 0)
    f32_out = jax.ShapeDtypeStruct((n, MIX_W), F32)
    row_spec = pl.BlockSpec((tm, MIX_W), row)
    if transposed:
        t_shape = jax.ShapeDtypeStruct((n_tiles, MIX_W, tm), BF16)
        t_spec = pl.BlockSpec((None, MIX_W, tm), lambda i: (i, 0, 0))
        kt_shape = jax.ShapeDtypeStruct((n // cos.shape[0], MIX_W, cos.shape[0]), F32)
        kt_spec = pl.BlockSpec((None, MIX_W, tm), lambda i: (i // pos_tiles, 0, i % pos_tiles))
        v_shape = jax.ShapeDtypeStruct((n * N_HEADS_A, HEAD_W), F32)
        v_spec = pl.BlockSpec((tm * N_HEADS_A, HEAD_W), row)
        out_shape = [kt_shape, v_shape, t_shape, jax.ShapeDtypeStruct((n, MIX_W), BF16), t_shape,
                     f32_out, f32_out, f32_out, f32_out]
        out_specs = [kt_spec, v_spec, t_spec, row_spec, t_spec,
                     row_spec, row_spec, row_spec, row_spec]
    else:
        out_shape = [f32_out] * 7
        out_specs = [row_spec] * 7
    return pl.pallas_call(
        functools.partial(_mixer_kernel, layer, transposed),
        grid=(n_tiles,),
        in_specs=[
            pl.BlockSpec((tm, d), row),
            pl.BlockSpec(w_mix.shape, lambda i: (0, 0)),
            pl.BlockSpec((tm, LANES), lambda i: (i % pos_tiles, 0)),
            pl.BlockSpec((tm, LANES), lambda i: (i % pos_tiles, 0)),
            pl.BlockSpec(lb_logits.shape, lambda i: (0, 0)),
        ],
        out_specs=out_specs,
        out_shape=out_shape,
        compiler_params=pltpu.CompilerParams(
            dimension_semantics=("parallel",), vmem_limit_bytes=VMEM_LIMIT),
        name="mixer_proj",
    )(x, w_mix, cos, sin, lb_logits)


def _prompt_attn_kernel(lam_init, tq, qt_ref, k_ref, vt_ref, lam_ref, g_ref, o_ref,
                        m_ref, acc_ref, sa_ref, sb_ref):
    qi = pl.program_id(2)
    qt = qt_ref[...]
    qrow = lax.broadcasted_iota(jnp.int32, qt.shape, 0)
    zero = jnp.zeros_like(qt)
    qmaps = (jnp.where(qrow < HEAD_DIM_A, qt, zero), jnp.where(qrow >= HEAD_DIM_A, qt, zero))

    m_ref[...] = jnp.full(m_ref.shape, NEG_BIG, F32)
    acc_ref[...] = jnp.zeros(acc_ref.shape, F32)
    ones = jnp.ones((SUM_ROWS, tq), BF16)

    def scores(j, s_ref):
        kj = k_ref[pl.ds(pl.multiple_of(j * tq, tq), tq), :]
        for mp in range(2):
            s_ref[mp] = _dot(kj, qmaps[mp])

    def consume(j, s_ref, masked):
        vj = jnp.concatenate([vt_ref[j], ones], axis=0)
        for mp in range(2):
            s = s_ref[mp]
            if masked:
                krow = lax.broadcasted_iota(jnp.int32, s.shape, 0)
                qcol = lax.broadcasted_iota(jnp.int32, s.shape, 1)
                s = jnp.where(krow <= qcol, s, NEG_BIG)
            m_old = m_ref[mp]
            m_new = jnp.maximum(m_old, jnp.max(s, axis=0, keepdims=True))
            alpha = jnp.exp2(m_old - m_new)
            p = jnp.exp2(s - m_new).astype(BF16)
            acc_ref[mp] = alpha * acc_ref[mp] + _dot(vj, p)
            m_ref[mp] = m_new

    def pair(j):
        scores(j + 1, sb_ref)
        consume(j, sa_ref, False)
        scores(j + 2, sa_ref)
        consume(j + 1, sb_ref, False)

    def trips(n_pairs, first, count):
        def body(jj, carry):
            for i in range(n_pairs):
                pair(first + 2 * (n_pairs * jj + i))
            return carry
        lax.fori_loop(0, count, body, 0)

    scores(0, sa_ref)
    trips(4, 0, qi // 8)
    trips(2, 8 * (qi // 8), (qi % 8) // 4)
    trips(1, 4 * (qi // 4), (qi % 4) // 2)

    @pl.when(qi % 2 == 0)
    def _():
        consume(qi, sa_ref, True)

    @pl.when(qi % 2 == 1)
    def _():
        scores(qi, sb_ref)
        consume(qi - 1, sa_ref, False)
        consume(qi, sb_ref, True)

    lam = _lambda(lam_ref, lam_init)
    acc1 = acc_ref[0]
    acc2 = acc_ref[1]
    o = (acc1[:HEAD_W] / acc1[HEAD_W:HEAD_W + 1]
         - lam * (acc2[:HEAD_W] / acc2[HEAD_W:HEAD_W + 1]))
    ms = jnp.mean(o * o, axis=0, keepdims=True)
    o = o * lax.rsqrt(ms + RMS_EPS) * g_ref[...] * (1.0 - lam_init)
    o_ref[...] = o.T.astype(o_ref.dtype)


def _prompt_attention(qt3, k_bf, vt3, lam_vecs, g_col, lam_init, batch, seq, tq):
    nq = seq // tq
    n = batch * seq
    return pl.pallas_call(
        functools.partial(_prompt_attn_kernel, lam_init, tq),
        grid=(batch, N_HEADS_A, nq),
        in_specs=[
            pl.BlockSpec((None, HEAD_W, tq), lambda b, h, i: (b * nq + i, h, 0)),
            pl.BlockSpec((seq, HEAD_W), lambda b, h, i: (b, h)),
            pl.BlockSpec((nq, HEAD_W, tq), lambda b, h, i: (b, h, 0)),
            pl.BlockSpec(lam_vecs.shape, lambda b, h, i: (0, 0)),
            pl.BlockSpec(g_col.shape, lambda b, h, i: (0, 0)),
        ],
        out_specs=pl.BlockSpec((tq, HEAD_W), lambda b, h, i: (b * nq + i, h)),
        out_shape=jax.ShapeDtypeStruct((n, MIX_W), BF16),
        scratch_shapes=[
            pltpu.VMEM((2, 1, tq), F32),
            pltpu.VMEM((2, HEAD_W + SUM_ROWS, tq), F32),
            pltpu.VMEM((2, tq, tq), F32),
            pltpu.VMEM((2, tq, tq), F32),
        ],
        compiler_params=pltpu.CompilerParams(
            dimension_semantics=("parallel", "parallel", "arbitrary"),
            vmem_limit_bytes=VMEM_LIMIT),
        name="prompt_diff_attention",
    )(qt3, k_bf, vt3, lam_vecs, g_col)


class _DecodeState:
    def __init__(self, q_ref, m_ref, l_ref, acc_ref, dec_seq):
        self.m_ref, self.l_ref, self.acc_ref, self.dec_seq = m_ref, l_ref, acc_ref, dec_seq
        qrep = q_ref[...]
        rrow = lax.broadcasted_iota(jnp.int32, qrep.shape, 0)
        rcol = lax.broadcasted_iota(jnp.int32, qrep.shape, 1)
        self.qbd = jnp.where(rcol // HEAD_DIM_A == rrow // dec_seq, qrep, 0.0).astype(BF16)

    def init(self):
        self.m_ref[...] = jnp.full(self.m_ref.shape, NEG_BIG, F32)
        self.l_ref[...] = jnp.zeros(self.l_ref.shape, F32)
        self.acc_ref[...] = jnp.zeros(self.acc_ref.shape, F32)

    def update(self, s, v_bf):
        m_old = self.m_ref[...]
        m_new = jnp.maximum(m_old, jnp.max(s, axis=-1, keepdims=True))
        alpha = jnp.exp2(m_old - m_new)
        p = jnp.exp2(s - m_new)
        self.l_ref[...] = alpha * self.l_ref[...] + jnp.sum(p, axis=-1, keepdims=True)
        self.acc_ref[...] = alpha * self.acc_ref[...] + _dot(p.astype(BF16), v_bf)
        self.m_ref[...] = m_new

    def pages(self, k_refs, v_refs):
        page = k_refs[0].shape[1]
        kt = jnp.concatenate([r[...] for r in k_refs], axis=1).astype(BF16)
        vp = jnp.concatenate(
            [jnp.concatenate([r[pl.ds(h, page, stride=N_HEADS_A), :] for h in range(N_HEADS_A)],
                             axis=1) for r in v_refs], axis=0).astype(BF16)
        self.update(_dot(self.qbd, kt), vp)

    def finish(self, kn_ref, vn_ref, lam_ref, g_ref, o_ref, lam_init):
        dec_seq = self.dec_seq
        kn = kn_ref[...].astype(BF16)
        vn = vn_ref[...].astype(BF16)
        s = _dot_nt(self.qbd, kn)
        key = lax.broadcasted_iota(jnp.int32, s.shape, 1)
        tok = lax.broadcasted_iota(jnp.int32, s.shape, 0) % dec_seq
        self.update(jnp.where(key <= tok, s, NEG_BIG), vn)
        lam = _lambda(lam_ref, lam_init)
        acc = self.acc_ref[...] / self.l_ref[...]
        rows_per_head = 2 * dec_seq
        outs = []
        for h in range(N_HEADS_A):
            blk = acc[h * rows_per_head:(h + 1) * rows_per_head, h * HEAD_W:(h + 1) * HEAD_W]
            o = blk[:dec_seq] - lam * blk[dec_seq:]
            ms = jnp.mean(o * o, axis=-1, keepdims=True)
            outs.append(o * lax.rsqrt(ms + RMS_EPS) * g_ref[...] * (1.0 - lam_init))
        o_ref[...] = jnp.concatenate(outs, axis=1).astype(o_ref.dtype)


def _hgrn_steps(step, q_ref, k_ref, lf_ref, v_ref, o_ref, st_ref):
    ones = jnp.ones((KEY_DIM_B, LANES), BF16)
    rowi = lax.broadcasted_iota(jnp.int32, (step, KEY_DIM_B), 0)

    def advance(r0):
        for h in range(N_HEADS_B):
            cols = slice(h * KEY_DIM_B, (h + 1) * KEY_DIM_B)
            q = q_ref[pl.ds(r0, step), cols]
            lk = k_ref[pl.ds(r0, step), cols]
            v = v_ref[pl.ds(r0, step), cols]
            b = lf_ref[pl.ds(r0, step), cols]
            d = 1
            while d < step:
                b = b + jnp.where(rowi >= d, pltpu.roll(b, d, 0), 0.0)
                d *= 2
            st = st_ref[h]
            o = _dot_nt((q * jnp.exp2(b)).astype(BF16), st.astype(BF16))
            c = b - lk
            xs = []
            for s in range(step):
                lo = (s // 8) * 8
                e = jnp.exp2(jnp.where(rowi[lo:] >= s, b[lo:] - c[s:s + 1], NEG_BIG))
                xs.append(q[lo:] * e)
            att = _dot(jnp.concatenate(xs, axis=0).astype(BF16), ones)
            parts = [o[lo:lo + 8] for lo in range(0, step, 8)]
            off = 0
            for s in range(step):
                lo = (s // 8) * 8
                for g in range(lo // 8, step // 8):
                    parts[g] = parts[g] + att[off:off + 8] * v[s:s + 1]
                    off += 8
            o_ref[pl.ds(r0, step), cols] = jnp.concatenate(parts, axis=0)
            b_last = b[step - 1:step]
            kd = jnp.exp2(b_last - c)
            st_ref[h] = st * jnp.exp2(b_last) + _dot_tn(v.astype(BF16), kd.astype(BF16))

    return advance


def _hgrn_state_io(s0_ref, sout_ref, st_ref):
    ti = pl.program_id(1)

    def load():
        @pl.when(ti == 0)
        def _():
            for h in range(N_HEADS_B):
                st_ref[h] = s0_ref[h].T

    def store():
        @pl.when(ti == pl.num_programs(1) - 1)
        def _():
            for h in range(N_HEADS_B):
                sout_ref[h] = st_ref[h].T

    return load, store


def _hgrn_kernel(step, n_steps, q_ref, k_ref, lf_ref, v_ref, s0_ref, o_ref, sout_ref, st_ref):
    load, store = _hgrn_state_io(s0_ref, sout_ref, st_ref)
    advance = _hgrn_steps(step, q_ref, k_ref, lf_ref, v_ref, o_ref, st_ref)
    load()

    def body(i, carry):
        advance(pl.multiple_of(i * step, step))
        return carry

    lax.fori_loop(0, n_steps, body, 0, unroll=math.gcd(n_steps, HGRN_UNROLL))
    store()


def _hgrn_scan(q, k, lf, v, s0, n_seq, seq_rows, tile_rows, step):
    n = q.shape[0]
    tiles = seq_rows // tile_rows
    row = lambda s, t: (s * tiles + t, 0)
    spec = pl.BlockSpec((tile_rows, MIX_W), row)
    st_spec = pl.BlockSpec((None, N_HEADS_B, KEY_DIM_B, KEY_DIM_B), lambda s, t: (s, 0, 0, 0))
    return pl.pallas_call(
        functools.partial(_hgrn_kernel, step, tile_rows // step),
        grid=(n_seq, tiles),
        in_specs=[spec, spec, spec, spec, st_spec],
        out_specs=[spec, st_spec],
        out_shape=[jax.ShapeDtypeStruct((n, MIX_W), F32),
                   jax.ShapeDtypeStruct(s0.shape, F32)],
        scratch_shapes=[pltpu.VMEM((N_HEADS_B, KEY_DIM_B, KEY_DIM_B), F32)],
        compiler_params=pltpu.CompilerParams(
            dimension_semantics=("parallel", "arbitrary"), vmem_limit_bytes=VMEM_LIMIT),
        name="hgrn2_scan",
    )(q, k, lf, v, s0)


def _hgrn_decode_kernel(lam_init, layer, step, n_steps, n_pages, steps_per_sample, dec_seq, pt_ref,
                        q_ref, k_ref, lf_ref, v_ref, s0_ref,
                        qs_ref, kn_ref, vn_ref, lam_ref, g_ref, ck_hbm, cv_hbm,
                        o_ref, sout_ref, oa_ref,
                        st_ref, m_ref, l_ref, acc_ref, kbuf, vbuf, sems):
    lin = pl.program_id(0) * pl.num_programs(1) + pl.program_id(1)
    last = pl.num_programs(0) * pl.num_programs(1) - 1
    phase = pl.program_id(1) % steps_per_sample
    slot = lin % 2

    def page_copies(at_step, at_slot):
        copies = []
        for pg in range(n_pages):
            page = pt_ref[at_step, pg]
            copies.append(pltpu.make_async_copy(ck_hbm.at[page, layer], kbuf.at[at_slot, pg],
                                                sems.at[at_slot, 0]))
            copies.append(pltpu.make_async_copy(cv_hbm.at[page, layer], vbuf.at[at_slot, pg],
                                                sems.at[at_slot, 1]))
        return copies

    @pl.when(lin == 0)
    def _():
        for c in page_copies(0, 0):
            c.start()

    @pl.when(lin < last)
    def _():
        for c in page_copies(lin + 1, 1 - slot):
            c.start()

    load, store = _hgrn_state_io(s0_ref, sout_ref, st_ref)
    advance = _hgrn_steps(step, q_ref, k_ref, lf_ref, v_ref, o_ref, st_ref)
    dec = _DecodeState(qs_ref, m_ref, l_ref, acc_ref, dec_seq)
    load()
    pl.when(phase == 0)(dec.init)

    for c in page_copies(lin, slot):
        c.wait()
    k_pages = [kbuf.at[slot, pg] for pg in range(n_pages)]
    v_pages = [vbuf.at[slot, pg] for pg in range(n_pages)]

    n_groups = math.gcd(math.gcd(n_steps, n_pages), DECODE_GROUPS)
    for i in range(n_groups):
        lo, hi = i * n_pages // n_groups, (i + 1) * n_pages // n_groups
        dec.pages(k_pages[lo:hi], v_pages[lo:hi])
        for r in range(i * n_steps // n_groups, (i + 1) * n_steps // n_groups):
            advance(r * step)

    store()

    @pl.when(phase == steps_per_sample - 1)
    def _():
        dec.finish(kn_ref, vn_ref, lam_ref, g_ref, oa_ref, lam_init)


def _hgrn_scan_with_decode(q, k, lf, v, s0, n_seq, seq_rows, step,
                           page_table, q_rep, k_new, v_new, cache_kt, cache_v, lam_vecs, g_row,
                           lam_init, layer, dec_seq, pages_per_step):
    n = q.shape[0]
    bs, n_pages = page_table.shape
    page = cache_kt.shape[3]
    n_rows = q_rep.shape[1]
    steps_per_sample = n_pages // pages_per_step
    total_steps = bs * steps_per_sample
    tiles = total_steps // n_seq
    tile_rows = seq_rows // tiles
    assert tiles * n_seq == total_steps and tile_rows * tiles == seq_rows
    assert tile_rows % step == 0 and tiles % steps_per_sample == 0

    row = lambda s, t, pt: (s * tiles + t, 0)
    spec = pl.BlockSpec((tile_rows, MIX_W), row)
    st_spec = pl.BlockSpec((None, N_HEADS_B, KEY_DIM_B, KEY_DIM_B), lambda s, t, pt: (s, 0, 0, 0))
    sample = lambda s, t, pt: ((s * tiles + t) // steps_per_sample, 0, 0)
    const = lambda s, t, pt: (0, 0)

    hbm = pl.BlockSpec(memory_space=pl.ANY)
    grid_spec = pltpu.PrefetchScalarGridSpec(
        num_scalar_prefetch=1,
        grid=(n_seq, tiles),
        in_specs=[spec, spec, spec, spec, st_spec,
                  pl.BlockSpec((None, n_rows, MIX_W), sample),
                  pl.BlockSpec((None, k_new.shape[1], MIX_W), sample),
                  pl.BlockSpec((None, k_new.shape[1], MIX_W), sample),
                  pl.BlockSpec(lam_vecs.shape, const),
                  pl.BlockSpec(g_row.shape, const), hbm, hbm],
        out_specs=[spec, st_spec, pl.BlockSpec((None, dec_seq, MIX_W), sample)],
        scratch_shapes=[
            pltpu.VMEM((N_HEADS_B, KEY_DIM_B, KEY_DIM_B), F32),
            pltpu.VMEM((n_rows, 1), F32),
            pltpu.VMEM((n_rows, 1), F32),
            pltpu.VMEM((n_rows, MIX_W), F32),
            pltpu.VMEM((2, pages_per_step, MIX_W, page), F32),
            pltpu.VMEM((2, pages_per_step, page * N_HEADS_A, HEAD_W), F32),
            pltpu.SemaphoreType.DMA((2, 2)),
        ],
    )
    return pl.pallas_call(
        functools.partial(_hgrn_decode_kernel, lam_init, layer, step, tile_rows // step,
                          pages_per_step, steps_per_sample, dec_seq),
        grid_spec=grid_spec,
        out_shape=[jax.ShapeDtypeStruct((n, MIX_W), F32),
                   jax.ShapeDtypeStruct(s0.shape, F32),
                   jax.ShapeDtypeStruct((bs, dec_seq, MIX_W), BF16)],
        compiler_params=pltpu.CompilerParams(
            dimension_semantics=("arbitrary", "arbitrary"), vmem_limit_bytes=VMEM_LIMIT),
        name="hgrn2_scan_with_decode_attention",
    )(page_table.reshape(total_steps, pages_per_step), q, k, lf, v, s0, q_rep, k_new, v_new,
      lam_vecs, g_row, cache_kt, cache_v)


def _finish_kernel(alpha, d_ff, seq_rows, x_ref, oa_ref, ob_ref, c0_ref, c1_ref,
                   wg_ref, ng_ref, wa_ref, wb_ref, wo_ref, l1g_ref, l1b_ref, wu_ref,
                   cw_ref, cb_ref, wd_ref, l2g_ref, l2b_ref, out_ref, conv_ref, *scratch):
    tm, d_model = x_ref.shape
    n_grp = FINISH_ROW_GROUPS if tm % (FINISH_ROW_GROUPS * LANES) == 0 else 1
    grp = [slice(i * (tm // n_grp), (i + 1) * (tm // n_grp)) for i in range(n_grp)]
    xs = [x_ref[r, :] for r in grp]
    gates = [_dot(x.astype(BF16), wg_ref[...]) for x in xs]

    def hgrn_out(r, gt):
        ob = ob_ref[r, :]
        gb = gt[:, :MIX_W]
        normed = []
        for h in range(N_HEADS_B):
            oh = ob[:, h * KEY_DIM_B:(h + 1) * KEY_DIM_B]
            ms = jnp.mean(oh * oh, axis=-1, keepdims=True)
            normed.append(oh * lax.rsqrt(ms + RMS_EPS) * ng_ref[...])
        return (jnp.concatenate(normed, axis=1) * (gb * jax.nn.sigmoid(gb))).astype(BF16)

    obn = [hgrn_out(r, gt) for r, gt in zip(grp, gates)]
    br_a = [_dot(oa_ref[r, :], wa_ref[...]) for r in grp]
    br_b = [_dot(o, wb_ref[...]) for o in obn]
    merged = [(jax.nn.sigmoid(gt[:, MIX_W:MIX_W + d_model]) * a
               + jax.nn.sigmoid(gt[:, MIX_W + d_model:]) * b).astype(BF16)
              for gt, a, b in zip(gates, br_a, br_b)]
    res = [_dot(m, wo_ref[...]) for m in merged]
    h1 = jnp.concatenate([_layer_norm(alpha * x + r, l1g_ref[...], l1b_ref[...])
                          for x, r in zip(xs, res)], axis=0)

    hb = h1.astype(BF16)
    chunks = [(c, min(c + FFN_CHUNK, d_ff)) for c in range(0, d_ff, FFN_CHUNK)]

    def up(lo, hi):
        return _dot(hb, wu_ref[:, lo:hi]), _dot(hb, wu_ref[:, d_ff + lo:d_ff + hi])

    if seq_rows is None:
        (a_ref,) = scratch

        @pl.when(pl.program_id(1) == 0)
        def _():
            a_ref[6:8, :] = c0_ref[...]

    y = None
    nxt = up(*chunks[0])
    for i, (lo, hi) in enumerate(chunks):
        a, g = nxt
        if i + 1 < len(chunks):
            nxt = up(*chunks[i + 1])
        if seq_rows is None:
            a_ref[8:8 + tm, lo:hi] = a
            a1 = a_ref[7:7 + tm, lo:hi]
            a2 = a_ref[6:6 + tm, lo:hi]
            tail = a_ref[6 + tm:8 + tm, lo:hi]
            a_ref[6:8, lo:hi] = tail
            conv_ref[:, lo:hi] = tail
        else:
            pos = lax.broadcasted_iota(jnp.int32, a.shape, 0) % seq_rows
            a1 = jnp.where(pos >= 1, pltpu.roll(a, 1, 0), 0.0) + c1_ref[:, lo:hi]
            a2 = jnp.where(pos >= 2, pltpu.roll(a, 2, 0), 0.0) + c0_ref[:, lo:hi]
            conv_ref[:, lo:hi] = a
        c = (cb_ref[:, lo:hi] + a2 * cw_ref[0:1, lo:hi] + a1 * cw_ref[1:2, lo:hi]
             + a * cw_ref[2:3, lo:hi])
        act = 0.5 * c * (1.0 + lax.erf(c * (2.0 ** -0.5))) * g
        yi = _dot(act.astype(BF16), wd_ref[lo:hi, :])
        y = yi if y is None else y + yi
    out_ref[...] = _layer_norm(alpha * h1 + y, l2g_ref[...], l2b_ref[...])


def _finish(x, oa, ob, c0, c1, weights, alpha, n_seq, seq_tiles, tm, seq_rows):
    n, d_model = x.shape
    d_ff = weights[-3].shape[0]
    row = lambda s, t: (s * seq_tiles + t, 0)
    const = lambda s, t: (0, 0)
    once = pl.Buffered(1)

    def wspec(w):
        return pl.BlockSpec(w.shape, const, pipeline_mode=once)

    if seq_rows is None:
        c_specs = [pl.BlockSpec((None, CONV_W - 1, d_ff), lambda s, t: (s, 0, 0))] * 2
        conv_shape = jax.ShapeDtypeStruct((n_seq, CONV_W - 1, d_ff), F32)
        conv_spec = pl.BlockSpec((None, CONV_W - 1, d_ff), lambda s, t: (s, 0, 0))
        scratch = [pltpu.VMEM((tm + 8, d_ff), F32)]
    else:
        c_specs = [pl.BlockSpec((tm, d_ff), row)] * 2
        conv_shape = jax.ShapeDtypeStruct((n, d_ff), F32)
        conv_spec = pl.BlockSpec((tm, d_ff), row)
        scratch = []
    return pl.pallas_call(
        functools.partial(_finish_kernel, alpha, d_ff, seq_rows),
        grid=(n_seq, seq_tiles),
        in_specs=[pl.BlockSpec((tm, d_model), row),
                  pl.BlockSpec((tm, MIX_W), row),
                  pl.BlockSpec((tm, MIX_W), row)] + c_specs + [wspec(w) for w in weights],
        out_specs=[pl.BlockSpec((tm, d_model), row), conv_spec],
        out_shape=[jax.ShapeDtypeStruct((n, d_model), F32), conv_shape],
        scratch_shapes=scratch,
        compiler_params=pltpu.CompilerParams(
            dimension_semantics=("parallel", "arbitrary"), vmem_limit_bytes=VMEM_LIMIT),
        name="finish_layer",
    )(x, oa, ob, c0, c1, *weights)


MIXER_TILE = 512
HGRN_STEP = 16
HGRN_UNROLL = 8
FINISH_TILE = 512
FINISH_ROW_GROUPS = 2
FFN_CHUNK = 512
PAGES_PER_STEP = 16
DECODE_GROUPS = 8
NEW_KEY_PAD = 16


def _rope_tables(pos):
    half = HEAD_DIM_A // 2
    inv = ROPE_THETA ** (-jnp.arange(half, dtype=F32) * 2.0 / HEAD_DIM_A)
    ang = pos.astype(F32)[:, None] * inv[None, :]
    cos = jnp.cos(ang)
    sin = jnp.sin(ang)
    reps = LANES // HEAD_DIM_A
    return (jnp.tile(jnp.concatenate([cos, cos], axis=1), (1, reps)),
            jnp.tile(jnp.concatenate([-sin, sin], axis=1), (1, reps)))


def kernel(x_prompt, x_sample, cache_k, cache_v, state_hgrn, state_conv, page_table, w_in, lambda_q1, lambda_k1, lambda_q2, lambda_k2, subln_g, lb_logits, hgrn_norm_g, w_branch_a, w_branch_b, w_out, ln1_g, ln1_b, w_up, conv_w, conv_b, w_down, ln2_g, ln2_b):
    bp, tp, d_model = x_prompt.shape
    bs, ts, _ = x_sample.shape
    depth = w_in.shape[0]
    n_phys, _, page, _, _, _ = cache_k.shape
    past_len = page_table.shape[1] * page
    d_ff = w_down.shape[1]
    alpha = (2 * depth) ** 0.25
    n_mix = 6 * MIX_W
    ts_pad = 8
    lb_logits = lb_logits.astype(F32)

    cos_p, sin_p = _rope_tables(jnp.arange(tp))
    cos_s, sin_s = _rope_tables(past_len + jnp.arange(bs * ts) % ts)
    cache_kt = jnp.transpose(cache_k, (0, 1, 3, 4, 5, 2)).reshape(n_phys, depth, MIX_W, page)
    cache_vr = cache_v.reshape(n_phys, depth, page * N_HEADS_A, HEAD_W)

    xp = x_prompt.reshape(bp * tp, d_model)
    xs = x_sample.reshape(bs * ts, d_model)
    outs = {name: [] for name in ("kp", "vp", "sp", "cp", "ks", "vs", "ss", "cs")}
    for l in range(depth):
        lam_init = 0.8 - 0.6 * math.exp(-0.3 * l)
        w_mix = w_in[l, :, :n_mix].astype(BF16)
        lam_vecs = jnp.stack([lambda_q1[l], lambda_k1[l], lambda_q2[l], lambda_k2[l]]).astype(F32)
        g_row = subln_g[l].reshape(1, HEAD_W).astype(F32)
        weights = (
            w_in[l, :, n_mix:].astype(BF16),
            hgrn_norm_g[l].reshape(1, KEY_DIM_B).astype(F32),
            w_branch_a[l].astype(BF16), w_branch_b[l].astype(BF16), w_out[l].astype(BF16),
            ln1_g[l].reshape(1, d_model), ln1_b[l].reshape(1, d_model),
            w_up[l].astype(BF16), conv_w[l], conv_b[l].reshape(1, d_ff),
            w_down[l].astype(BF16), ln2_g[l].reshape(1, d_model), ln2_b[l].reshape(1, d_model))

        k_p, v_p, qt3, k_bf, vt3, qb, kk, lf, ib = _mixer_proj(
            xp, w_mix, cos_p, sin_p, lb_logits, l, True, MIXER_TILE)
        k_s, v_s, q_s, qb_s, kk_s, lf_s, ib_s = _mixer_proj(
            xs, w_mix, cos_s, sin_s, lb_logits, l, False, bs * ts)
        pad_t = lambda a, rows: jnp.pad(a.reshape(bs, ts, MIX_W), ((0, 0), (0, rows - ts), (0, 0)))
        q_rep = jnp.tile(q_s.reshape(bs, ts, MIX_W), (1, 2 * N_HEADS_A, 1))

        oa = _prompt_attention(qt3, k_bf, vt3, lam_vecs, g_row.reshape(HEAD_W, 1), lam_init,
                               bp, tp, MIXER_TILE)
        s0 = jnp.zeros((bp, N_HEADS_B, KEY_DIM_B, KEY_DIM_B), F32)
        ob, s_p, oa_s = _hgrn_scan_with_decode(
            qb, kk, lf, ib, s0, bp, tp, HGRN_STEP,
            page_table, q_rep, pad_t(k_s, NEW_KEY_PAD), pad_t(v_s, NEW_KEY_PAD),
            cache_kt, cache_vr, lam_vecs, g_row, lam_init, l, ts, PAGES_PER_STEP)
        conv0 = jnp.zeros((bp, CONV_W - 1, d_ff), F32)
        xp_new, conv_p = _finish(xp, oa, ob, conv0, conv0, weights, alpha,
                                 bp, tp // FINISH_TILE, FINISH_TILE, None)
        outs["kp"].append(jnp.transpose(k_p.reshape(bp, N_HEADS_A, 2, HEAD_DIM_A, tp), (0, 4, 1, 2, 3)))
        outs["vp"].append(v_p.reshape(bp, tp, N_HEADS_A, HEAD_W))
        outs["sp"].append(s_p)
        outs["cp"].append(conv_p)

        flat = lambda a: pad_t(a, ts_pad).reshape(bs * ts_pad, MIX_W)
        ob, s_s = _hgrn_scan(flat(qb_s), flat(kk_s), flat(lf_s), flat(ib_s), state_hgrn[:, l],
                             bs, ts_pad, ts_pad, ts_pad)
        ob = ob.reshape(bs, ts_pad, MIX_W)[:, :ts].reshape(bs * ts, MIX_W)
        st = state_conv[:, l]
        c0 = jnp.concatenate([st, jnp.zeros((bs, ts - 2, d_ff), F32)], axis=1)
        c1 = jnp.concatenate([st[:, 1:], jnp.zeros((bs, ts - 1, d_ff), F32)], axis=1)
        xs_new, a_s = _finish(xs, oa_s.reshape(bs * ts, MIX_W), ob,
                              c0.reshape(bs * ts, d_ff), c1.reshape(bs * ts, d_ff),
                              weights, alpha, 1, 1, bs * ts, ts)
        conv_s = jnp.concatenate([st, a_s.reshape(bs, ts, d_ff)], axis=1)[:, -(CONV_W - 1):]
        outs["ks"].append(k_s.reshape(bs, ts, N_HEADS_A, 2, HEAD_DIM_A))
        outs["vs"].append(v_s.reshape(bs, ts, N_HEADS_A, HEAD_W))
        outs["ss"].append(s_s)
        outs["cs"].append(conv_s)
        xp, xs = xp_new, xs_new

    stack = lambda name: jnp.stack(outs[name], axis=1)
    return (xp.reshape(bp, tp, d_model), xs.reshape(bs, ts, d_model),
            stack("kp"), stack("vp"), stack("sp"), stack("cp"),
            stack("ks"), stack("vs"), stack("ss"), stack("cs"))
```

```python
import functools
import math

import jax
import jax.numpy as jnp
import numpy as np
from jax import lax
from jax.experimental import pallas as pl
from jax.experimental.pallas import tpu as pltpu

F32 = jnp.float32
BF16 = jnp.bfloat16

N_HEADS_A = 4
HEAD_DIM_A = 64
N_HEADS_B = 4
KEY_DIM_B = 128
CONV_W = 3
ROPE_THETA = 10000.0
LN_EPS = 1e-5
RMS_EPS = 1e-5

LANES = 128
HEAD_W = 2 * HEAD_DIM_A
MIX_W = N_HEADS_A * HEAD_W
NEG_BIG = -1e30
SUM_ROWS = 16

VMEM_LIMIT = 56 * 1024 * 1024


def _dot(a, b):
    return jnp.dot(a, b, preferred_element_type=F32)


def _dot_nt(a, b):
    return lax.dot_general(a, b, (((1,), (1,)), ((), ())), preferred_element_type=F32)


def _dot_tn(a, b):
    return lax.dot_general(a, b, (((0,), (0,)), ((), ())), preferred_element_type=F32)


def _layer_norm(x, g, b):
    mu = jnp.mean(x, axis=-1, keepdims=True)
    xc = x - mu
    var = jnp.mean(xc * xc, axis=-1, keepdims=True)
    return xc * lax.rsqrt(var + LN_EPS) * g + b


def _lambda(lam_ref, lam_init):
    lv = lam_ref[...]
    e1 = jnp.exp(jnp.sum(lv[0:1] * lv[1:2], axis=-1, keepdims=True))
    e2 = jnp.exp(jnp.sum(lv[2:3] * lv[3:4], axis=-1, keepdims=True))
    return e1 - e2 + lam_init


def _rope(x, cos, sin_signed):
    lane = lax.broadcasted_iota(jnp.int32, x.shape, 1)
    first_half = (lane % HEAD_DIM_A) < (HEAD_DIM_A // 2)
    partner = jnp.where(first_half, pltpu.roll(x, LANES - HEAD_DIM_A // 2, 1),
                        pltpu.roll(x, HEAD_DIM_A // 2, 1))
    return x * cos + partner * sin_signed


def _mixer_kernel(layer, transposed, x_ref, w_ref, cos_ref, sin_ref, lbl_ref, *out_refs):
    if transposed:
        k_ref, v_ref, qt_ref, kb_ref, vt_ref, qb_ref, kk_ref, lf_ref, ib_ref = out_refs
    else:
        k_ref, v_ref, q_ref, qb_ref, kk_ref, lf_ref, ib_ref = out_refs
    xb = x_ref[...].astype(BF16)
    cos = cos_ref[...]
    sin = sin_ref[...]

    def section(i):
        return _dot(xb, w_ref[:, i * MIX_W:(i + 1) * MIX_W])

    scale = HEAD_DIM_A ** -0.5 * math.log2(math.e)
    qa = section(0)
    qa = jnp.concatenate([_rope(qa[:, h * HEAD_W:(h + 1) * HEAD_W], cos, sin) * scale
                          for h in range(N_HEADS_A)], axis=1)
    ka = section(1)
    ka = jnp.concatenate([_rope(ka[:, h * HEAD_W:(h + 1) * HEAD_W], cos, sin)
                          for h in range(N_HEADS_A)], axis=1)
    va = section(2)
    if transposed:
        k_ref[...] = ka.T
        for h in range(N_HEADS_A):
            v_ref[pl.ds(h, va.shape[0], stride=N_HEADS_A), :] = va[:, h * HEAD_W:(h + 1) * HEAD_W]
        qt_ref[...] = qa.T.astype(BF16)
        kb_ref[...] = ka.astype(BF16)
        vt_ref[...] = va.T.astype(BF16)
    else:
        k_ref[...] = ka
        v_ref[...] = va
        q_ref[...] = qa

    qb_ref[...] = section(3)
    fr = section(4)
    ib_ref[...] = section(5)

    lg = lbl_ref[...]
    e = jnp.exp(lg - jnp.max(lg, axis=0, keepdims=True))
    lb = jnp.sum(e[:layer + 1], axis=0, keepdims=True) / jnp.sum(e, axis=0, keepdims=True)
    f = lb + (1.0 - lb) * (1.0 / (1.0 + jnp.exp(-fr)))
    lf_ref[...] = jnp.log2(f)
    kk_ref[...] = jnp.log2((1.0 - lb) * (1.0 / (1.0 + jnp.exp(fr))))


def _mixer_proj(x, w_mix, cos, sin, lb_logits, layer, transposed, tm):
    n = x.shape[0]
    d = x.shape[1]
    n_tiles = n // tm
    pos_tiles = cos.shape[0] // tm
    row = lambda i: (i, 0)
    f32_out = jax.ShapeDtypeStruct((n, MIX_W), F32)
    row_spec = pl.BlockSpec((tm, MIX_W), row)
    if transposed:
        t_shape = jax.ShapeDtypeStruct((n_tiles, MIX_W, tm), BF16)
        t_spec = pl.BlockSpec((None, MIX_W, tm), lambda i: (i, 0, 0))
        kt_shape = jax.ShapeDtypeStruct((n // cos.shape[0], MIX_W, cos.shape[0]), F32)
        kt_spec = pl.BlockSpec((None, MIX_W, tm), lambda i: (i // pos_tiles, 0, i % pos_tiles))
        v_shape = jax.ShapeDtypeStruct((n * N_HEADS_A, HEAD_W), F32)
        v_spec = pl.BlockSpec((tm * N_HEADS_A, HEAD_W), row)
        out_shape = [kt_shape, v_shape, t_shape, jax.ShapeDtypeStruct((n, MIX_W), BF16), t_shape,
                     f32_out, f32_out, f32_out, f32_out]
        out_specs = [kt_spec, v_spec, t_spec, row_spec, t_spec,
                     row_spec, row_spec, row_spec, row_spec]
    else:
        out_shape = [f32_out] * 7
        out_specs = [row_spec] * 7
    return pl.pallas_call(
        functools.partial(_mixer_kernel, layer, transposed),
        grid=(n_tiles,),
        in_specs=[
            pl.BlockSpec((tm, d), row),
            pl.BlockSpec(w_mix.shape, lambda i: (0, 0)),
            pl.BlockSpec((tm, LANES), lambda i: (i % pos_tiles, 0)),
            pl.BlockSpec((tm, LANES), lambda i: (i % pos_tiles, 0)),
            pl.BlockSpec(lb_logits.shape, lambda i: (0, 0)),
        ],
        out_specs=out_specs,
        out_shape=out_shape,
        compiler_params=pltpu.CompilerParams(
            dimension_semantics=("parallel",), vmem_limit_bytes=VMEM_LIMIT),
        name="mixer_proj",
    )(x, w_mix, cos, sin, lb_logits)


def _prompt_attn_kernel(lam_init, tq, qt_ref, k_ref, vt_ref, lam_ref, g_ref, o_ref,
                        m_ref, acc_ref, sa_ref, sb_ref, qm_ref):
    nq = qt_ref.shape[0]
    ones = jnp.ones((SUM_ROWS, tq), BF16)
    lam = _lambda(lam_ref, lam_init)

    def query_maps(t):
        qt = qt_ref[t]
        qrow = lax.broadcasted_iota(jnp.int32, qt.shape, 0)
        zero = jnp.zeros_like(qt)
        return jnp.where(qrow < HEAD_DIM_A, qt, zero), jnp.where(qrow >= HEAD_DIM_A, qt, zero)

    def first_scores(t, s_ref):
        kj = k_ref[0:tq, :]
        for mp, q in enumerate(query_maps(t)):
            s_ref[mp] = _dot(kj, q)

    def scores(j, s_ref):
        kj = k_ref[pl.ds(pl.multiple_of(j * tq, tq), tq), :]
        for mp in range(2):
            s_ref[mp] = _dot(kj, qm_ref[mp])

    def consume(j, s_ref, masked):
        vj = jnp.concatenate([vt_ref[j], ones], axis=0)
        for mp in range(2):
            s = s_ref[mp]
            if masked:
                krow = lax.broadcasted_iota(jnp.int32, s.shape, 0)
                qcol = lax.broadcasted_iota(jnp.int32, s.shape, 1)
                s = jnp.where(krow <= qcol, s, NEG_BIG)
            m_old = m_ref[mp]
            m_new = jnp.maximum(m_old, jnp.max(s, axis=0, keepdims=True))
            alpha = jnp.exp2(m_old - m_new)
            p = jnp.exp2(s - m_new).astype(BF16)
            acc_ref[mp] = alpha * acc_ref[mp] + _dot(vj, p)
            m_ref[mp] = m_new

    def pair(j):
        scores(j + 1, sb_ref)
        consume(j, sa_ref, False)
        scores(j + 2, sa_ref)
        consume(j + 1, sb_ref, False)

    def trips(n_pairs, first, count):
        def body(jj, carry):
            for i in range(n_pairs):
                pair(first + 2 * (n_pairs * jj + i))
            return carry
        lax.fori_loop(0, count, body, 0)

    def tile(qi, carry):
        for mp, q in enumerate(query_maps(qi)):
            qm_ref[mp] = q
        m_ref[...] = jnp.full(m_ref.shape, NEG_BIG, F32)
        acc_ref[...] = jnp.zeros(acc_ref.shape, F32)
        nxt = jnp.minimum(qi + 1, nq - 1)

        trips(4, 0, qi // 8)
        trips(2, 8 * (qi // 8), (qi % 8) // 4)
        trips(1, 4 * (qi // 4), (qi % 4) // 2)

        @pl.when(qi % 2 == 0)
        def _():
            first_scores(nxt, sb_ref)
            consume(qi, sa_ref, True)
            sa_ref[...] = sb_ref[...]

        @pl.when(qi % 2 == 1)
        def _():
            scores(qi, sb_ref)
            consume(qi - 1, sa_ref, False)
            first_scores(nxt, sa_ref)
            consume(qi, sb_ref, True)

        acc1 = acc_ref[0]
        acc2 = acc_ref[1]
        o = (acc1[:HEAD_W] / acc1[HEAD_W:HEAD_W + 1]
             - lam * (acc2[:HEAD_W] / acc2[HEAD_W:HEAD_W + 1]))
        ms = jnp.mean(o * o, axis=0, keepdims=True)
        o = o * lax.rsqrt(ms + RMS_EPS) * g_ref[...] * (1.0 - lam_init)
        o_ref[pl.ds(pl.multiple_of(qi * tq, tq), tq), :] = o.T.astype(o_ref.dtype)
        return carry

    first_scores(0, sa_ref)
    lax.fori_loop(0, nq, tile, 0)


def _prompt_attention(qt3, k_bf, vt3, lam_vecs, g_col, lam_init, batch, seq, tq):
    nq = seq // tq
    n = batch * seq
    return pl.pallas_call(
        functools.partial(_prompt_attn_kernel, lam_init, tq),
        grid=(batch, N_HEADS_A),
        in_specs=[
            pl.BlockSpec((nq, HEAD_W, tq), lambda b, h: (b, h, 0)),
            pl.BlockSpec((seq, HEAD_W), lambda b, h: (b, h)),
            pl.BlockSpec((nq, HEAD_W, tq), lambda b, h: (b, h, 0)),
            pl.BlockSpec(lam_vecs.shape, lambda b, h: (0, 0)),
            pl.BlockSpec(g_col.shape, lambda b, h: (0, 0)),
        ],
        out_specs=pl.BlockSpec((seq, HEAD_W), lambda b, h: (b, h)),
        out_shape=jax.ShapeDtypeStruct((n, MIX_W), BF16),
        scratch_shapes=[
            pltpu.VMEM((2, 1, tq), F32),
            pltpu.VMEM((2, HEAD_W + SUM_ROWS, tq), F32),
            pltpu.VMEM((2, tq, tq), F32),
            pltpu.VMEM((2, tq, tq), F32),
            pltpu.VMEM((2, HEAD_W, tq), BF16),
        ],
        compiler_params=pltpu.CompilerParams(
            dimension_semantics=("parallel", "parallel"),
            vmem_limit_bytes=VMEM_LIMIT),
        name="prompt_diff_attention",
    )(qt3, k_bf, vt3, lam_vecs, g_col)


class _DecodeState:
    def __init__(self, q_ref, m_ref, l_ref, acc_ref, dec_seq):
        self.m_ref, self.l_ref, self.acc_ref, self.dec_seq = m_ref, l_ref, acc_ref, dec_seq
        qrep = q_ref[...]
        rrow = lax.broadcasted_iota(jnp.int32, qrep.shape, 0)
        rcol = lax.broadcasted_iota(jnp.int32, qrep.shape, 1)
        self.qbd = jnp.where(rcol // HEAD_DIM_A == rrow // dec_seq, qrep, 0.0).astype(BF16)

    def init(self):
        self.m_ref[...] = jnp.full(self.m_ref.shape, NEG_BIG, F32)
        self.l_ref[...] = jnp.zeros(self.l_ref.shape, F32)
        self.acc_ref[...] = jnp.zeros(self.acc_ref.shape, F32)

    def update(self, s, v_bf):
        m_old = self.m_ref[...]
        m_new = jnp.maximum(m_old, jnp.max(s, axis=-1, keepdims=True))
        alpha = jnp.exp2(m_old - m_new)
        p = jnp.exp2(s - m_new)
        self.l_ref[...] = alpha * self.l_ref[...] + jnp.sum(p, axis=-1, keepdims=True)
        self.acc_ref[...] = alpha * self.acc_ref[...] + _dot(p.astype(BF16), v_bf)
        self.m_ref[...] = m_new

    def pages(self, k_refs, v_refs):
        page = k_refs[0].shape[1]
        kt = jnp.concatenate([r[...] for r in k_refs], axis=1).astype(BF16)
        vp = jnp.concatenate(
            [jnp.concatenate([r[pl.ds(h, page, stride=N_HEADS_A), :] for h in range(N_HEADS_A)],
                             axis=1) for r in v_refs], axis=0).astype(BF16)
        self.update(_dot(self.qbd, kt), vp)

    def finish(self, kn_ref, vn_ref, lam_ref, g_ref, o_ref, lam_init):
        dec_seq = self.dec_seq
        kn = kn_ref[...].astype(BF16)
        vn = vn_ref[...].astype(BF16)
        s = _dot_nt(self.qbd, kn)
        key = lax.broadcasted_iota(jnp.int32, s.shape, 1)
        tok = lax.broadcasted_iota(jnp.int32, s.shape, 0) % dec_seq
        self.update(jnp.where(key <= tok, s, NEG_BIG), vn)
        lam = _lambda(lam_ref, lam_init)
        acc = self.acc_ref[...] / self.l_ref[...]
        rows_per_head = 2 * dec_seq
        outs = []
        for h in range(N_HEADS_A):
            blk = acc[h * rows_per_head:(h + 1) * rows_per_head, h * HEAD_W:(h + 1) * HEAD_W]
            o = blk[:dec_seq] - lam * blk[dec_seq:]
            ms = jnp.mean(o * o, axis=-1, keepdims=True)
            outs.append(o * lax.rsqrt(ms + RMS_EPS) * g_ref[...] * (1.0 - lam_init))
        o_ref[...] = jnp.concatenate(outs, axis=1).astype(o_ref.dtype)


def _hgrn_steps(step, q_ref, k_ref, lf_ref, v_ref, o_ref, st_ref):
    ones = jnp.ones((KEY_DIM_B, LANES), BF16)
    rowi = lax.broadcasted_iota(jnp.int32, (step, KEY_DIM_B), 0)

    def advance(r0):
        for h in range(N_HEADS_B):
            cols = slice(h * KEY_DIM_B, (h + 1) * KEY_DIM_B)
            q = q_ref[pl.ds(r0, step), cols]
            lk = k_ref[pl.ds(r0, step), cols]
            v = v_ref[pl.ds(r0, step), cols]
            b = lf_ref[pl.ds(r0, step), cols]
            d = 1
            while d < step:
                b = b + jnp.where(rowi >= d, pltpu.roll(b, d, 0), 0.0)
                d *= 2
            st = st_ref[h]
            o = _dot_nt((q * jnp.exp2(b)).astype(BF16), st.astype(BF16))
            c = b - lk
            xs = []
            for s in range(step):
                lo = (s // 8) * 8
                e = jnp.exp2(jnp.where(rowi[lo:] >= s, b[lo:] - c[s:s + 1], NEG_BIG))
                xs.append(q[lo:] * e)
            att = _dot(jnp.concatenate(xs, axis=0).astype(BF16), ones)
            parts = [o[lo:lo + 8] for lo in range(0, step, 8)]
            off = 0
            for s in range(step):
                lo = (s // 8) * 8
                for g in range(lo // 8, step // 8):
                    parts[g] = parts[g] + att[off:off + 8] * v[s:s + 1]
                    off += 8
            o_ref[pl.ds(r0, step), cols] = jnp.concatenate(parts, axis=0)
            b_last = b[step - 1:step]
            kd = jnp.exp2(b_last - c)
            st_ref[h] = st * jnp.exp2(b_last) + _dot_tn(v.astype(BF16), kd.astype(BF16))

    return advance


def _hgrn_state_io(s0_ref, sout_ref, st_ref):
    ti = pl.program_id(1)

    def load():
        @pl.when(ti == 0)
        def _():
            for h in range(N_HEADS_B):
                st_ref[h] = s0_ref[h].T

    def store():
        @pl.when(ti == pl.num_programs(1) - 1)
        def _():
            for h in range(N_HEADS_B):
                sout_ref[h] = st_ref[h].T

    return load, store


def _hgrn_kernel(step, n_steps, group, q_ref, k_ref, lf_ref, v_ref, s0_ref, o_ref, sout_ref,
                 st_ref):
    advance = _hgrn_steps(step, q_ref, k_ref, lf_ref, v_ref, o_ref, st_ref)
    for g in range(group):
        load, store = _hgrn_state_io(s0_ref.at[g], sout_ref.at[g], st_ref)
        load()

        def body(i, carry, first=g * n_steps * step):
            advance(pl.multiple_of(first + i * step, step))
            return carry

        lax.fori_loop(0, n_steps, body, 0, unroll=math.gcd(n_steps, HGRN_UNROLL))
        store()


def _hgrn_scan(q, k, lf, v, s0, n_seq, seq_rows, tile_rows, step, group=1):
    n = q.shape[0]
    tiles = seq_rows // tile_rows
    assert group == 1 or (tiles == 1 and n_seq % group == 0)
    row = lambda s, t: (s * tiles + t, 0)
    spec = pl.BlockSpec((group * tile_rows, MIX_W), row)
    st_spec = pl.BlockSpec((group, N_HEADS_B, KEY_DIM_B, KEY_DIM_B), lambda s, t: (s, 0, 0, 0))
    return pl.pallas_call(
        functools.partial(_hgrn_kernel, step, tile_rows // step, group),
        grid=(n_seq // group, tiles),
        in_specs=[spec, spec, spec, spec, st_spec],
        out_specs=[spec, st_spec],
        out_shape=[jax.ShapeDtypeStruct((n, MIX_W), F32),
                   jax.ShapeDtypeStruct(s0.shape, F32)],
        scratch_shapes=[pltpu.VMEM((N_HEADS_B, KEY_DIM_B, KEY_DIM_B), F32)],
        compiler_params=pltpu.CompilerParams(
            dimension_semantics=("parallel", "arbitrary"), vmem_limit_bytes=VMEM_LIMIT),
        name="hgrn2_scan",
    )(q, k, lf, v, s0)


def _hgrn_decode_kernel(lam_init, layer, step, n_steps, n_pages, steps_per_sample, dec_seq, pt_ref,
                        q_ref, k_ref, lf_ref, v_ref, s0_ref,
                        qs_ref, kn_ref, vn_ref, lam_ref, g_ref, ck_hbm, cv_hbm,
                        o_ref, sout_ref, oa_ref,
                        st_ref, m_ref, l_ref, acc_ref, kbuf, vbuf, sems):
    lin = pl.program_id(0) * pl.num_programs(1) + pl.program_id(1)
    last = pl.num_programs(0) * pl.num_programs(1) - 1
    phase = pl.program_id(1) % steps_per_sample
    slot = lin % 2

    def page_copies(at_step, at_slot):
        copies = []
        for pg in range(n_pages):
            page = pt_ref[at_step, pg]
            copies.append(pltpu.make_async_copy(ck_hbm.at[page, layer], kbuf.at[at_slot, pg],
                                                sems.at[at_slot, 0]))
            copies.append(pltpu.make_async_copy(cv_hbm.at[page, layer], vbuf.at[at_slot, pg],
                                                sems.at[at_slot, 1]))
        return copies

    @pl.when(lin == 0)
    def _():
        for c in page_copies(0, 0):
            c.start()

    @pl.when(lin < last)
    def _():
        for c in page_copies(lin + 1, 1 - slot):
            c.start()

    load, store = _hgrn_state_io(s0_ref, sout_ref, st_ref)
    advance = _hgrn_steps(step, q_ref, k_ref, lf_ref, v_ref, o_ref, st_ref)
    dec = _DecodeState(qs_ref, m_ref, l_ref, acc_ref, dec_seq)
    load()
    pl.when(phase == 0)(dec.init)

    for c in page_copies(lin, slot):
        c.wait()
    k_pages = [kbuf.at[slot, pg] for pg in range(n_pages)]
    v_pages = [vbuf.at[slot, pg] for pg in range(n_pages)]

    n_groups = math.gcd(math.gcd(n_steps, n_pages), DECODE_GROUPS)
    for i in range(n_groups):
        lo, hi = i * n_pages // n_groups, (i + 1) * n_pages // n_groups
        dec.pages(k_pages[lo:hi], v_pages[lo:hi])
        for r in range(i * n_steps // n_groups, (i + 1) * n_steps // n_groups):
            advance(r * step)

    store()

    @pl.when(phase == steps_per_sample - 1)
    def _():
        dec.finish(kn_ref, vn_ref, lam_ref, g_ref, oa_ref, lam_init)


def _hgrn_scan_with_decode(q, k, lf, v, s0, n_seq, seq_rows, step,
                           page_table, q_rep, k_new, v_new, cache_kt, cache_v, lam_vecs, g_row,
                           lam_init, layer, dec_seq, pages_per_step):
    n = q.shape[0]
    bs, n_pages = page_table.shape
    page = cache_kt.shape[3]
    n_rows = q_rep.shape[1]
    steps_per_sample = n_pages // pages_per_step
    total_steps = bs * steps_per_sample
    tiles = total_steps // n_seq
    tile_rows = seq_rows // tiles
    assert tiles * n_seq == total_steps and tile_rows * tiles == seq_rows
    assert tile_rows % step == 0 and tiles % steps_per_sample == 0

    row = lambda s, t, pt: (s * tiles + t, 0)
    spec = pl.BlockSpec((tile_rows, MIX_W), row)
    st_spec = pl.BlockSpec((None, N_HEADS_B, KEY_DIM_B, KEY_DIM_B), lambda s, t, pt: (s, 0, 0, 0))
    sample = lambda s, t, pt: ((s * tiles + t) // steps_per_sample, 0, 0)
    const = lambda s, t, pt: (0, 0)

    hbm = pl.BlockSpec(memory_space=pl.ANY)
    grid_spec = pltpu.PrefetchScalarGridSpec(
        num_scalar_prefetch=1,
        grid=(n_seq, tiles),
        in_specs=[spec, spec, spec, spec, st_spec,
                  pl.BlockSpec((None, n_rows, MIX_W), sample),
                  pl.BlockSpec((None, k_new.shape[1], MIX_W), sample),
                  pl.BlockSpec((None, k_new.shape[1], MIX_W), sample),
                  pl.BlockSpec(lam_vecs.shape, const),
                  pl.BlockSpec(g_row.shape, const), hbm, hbm],
        out_specs=[spec, st_spec, pl.BlockSpec((None, dec_seq, MIX_W), sample)],
        scratch_shapes=[
            pltpu.VMEM((N_HEADS_B, KEY_DIM_B, KEY_DIM_B), F32),
            pltpu.VMEM((n_rows, 1), F32),
            pltpu.VMEM((n_rows, 1), F32),
            pltpu.VMEM((n_rows, MIX_W), F32),
            pltpu.VMEM((2, pages_per_step, MIX_W, page), F32),
            pltpu.VMEM((2, pages_per_step, page * N_HEADS_A, HEAD_W), F32),
            pltpu.SemaphoreType.DMA((2, 2)),
        ],
    )
    return pl.pallas_call(
        functools.partial(_hgrn_decode_kernel, lam_init, layer, step, tile_rows // step,
                          pages_per_step, steps_per_sample, dec_seq),
        grid_spec=grid_spec,
        out_shape=[jax.ShapeDtypeStruct((n, MIX_W), F32),
                   jax.ShapeDtypeStruct(s0.shape, F32),
                   jax.ShapeDtypeStruct((bs, dec_seq, MIX_W), BF16)],
        compiler_params=pltpu.CompilerParams(
            dimension_semantics=("arbitrary", "arbitrary"), vmem_limit_bytes=VMEM_LIMIT),
        name="hgrn2_scan_with_decode_attention",
    )(page_table.reshape(total_steps, pages_per_step), q, k, lf, v, s0, q_rep, k_new, v_new,
      lam_vecs, g_row, cache_kt, cache_v)


def _finish_kernel(alpha, d_ff, seq_rows, x_ref, oa_ref, ob_ref, c0_ref, c1_ref,
                   wg_ref, ng_ref, wa_ref, wb_ref, wo_ref, l1g_ref, l1b_ref, wu_ref,
                   cw_ref, cb_ref, wd_ref, l2g_ref, l2b_ref, out_ref, conv_ref, *scratch):
    tm, d_model = x_ref.shape
    n_grp = FINISH_ROW_GROUPS if tm % (FINISH_ROW_GROUPS * LANES) == 0 else 1
    grp = [slice(i * (tm // n_grp), (i + 1) * (tm // n_grp)) for i in range(n_grp)]
    xs = [x_ref[r, :] for r in grp]
    gates = [_dot(x.astype(BF16), wg_ref[...]) for x in xs]

    def hgrn_out(r, gt):
        ob = ob_ref[r, :]
        gb = gt[:, :MIX_W]
        normed = []
        for h in range(N_HEADS_B):
            oh = ob[:, h * KEY_DIM_B:(h + 1) * KEY_DIM_B]
            ms = jnp.mean(oh * oh, axis=-1, keepdims=True)
            normed.append(oh * lax.rsqrt(ms + RMS_EPS) * ng_ref[...])
        return (jnp.concatenate(normed, axis=1) * (gb * jax.nn.sigmoid(gb))).astype(BF16)

    obn = [hgrn_out(r, gt) for r, gt in zip(grp, gates)]
    br_a = [_dot(oa_ref[r, :], wa_ref[...]) for r in grp]
    br_b = [_dot(o, wb_ref[...]) for o in obn]
    merged = [(jax.nn.sigmoid(gt[:, MIX_W:MIX_W + d_model]) * a
               + jax.nn.sigmoid(gt[:, MIX_W + d_model:]) * b).astype(BF16)
              for gt, a, b in zip(gates, br_a, br_b)]
    res = [_dot(m, wo_ref[...]) for m in merged]
    h1 = jnp.concatenate([_layer_norm(alpha * x + r, l1g_ref[...], l1b_ref[...])
                          for x, r in zip(xs, res)], axis=0)

    hb = h1.astype(BF16)
    chunks = [(c, min(c + FFN_CHUNK, d_ff)) for c in range(0, d_ff, FFN_CHUNK)]

    def up(lo, hi):
        return _dot(hb, wu_ref[:, lo:hi]), _dot(hb, wu_ref[:, d_ff + lo:d_ff + hi])

    if seq_rows is None:
        (a_ref,) = scratch

        @pl.when(pl.program_id(1) == 0)
        def _():
            a_ref[6:8, :] = c0_ref[...]

    y = None
    nxt = up(*chunks[0])
    for i, (lo, hi) in enumerate(chunks):
        a, g = nxt
        if i + 1 < len(chunks):
            nxt = up(*chunks[i + 1])
        if seq_rows is None:
            a_ref[8:8 + tm, lo:hi] = a
            a1 = a_ref[7:7 + tm, lo:hi]
            a2 = a_ref[6:6 + tm, lo:hi]
            tail = a_ref[6 + tm:8 + tm, lo:hi]
            a_ref[6:8, lo:hi] = tail
            conv_ref[:, lo:hi] = tail
        else:
            pos = lax.broadcasted_iota(jnp.int32, a.shape, 0) % seq_rows
            a1 = jnp.where(pos >= 1, pltpu.roll(a, 1, 0), 0.0) + c1_ref[:, lo:hi]
            a2 = jnp.where(pos >= 2, pltpu.roll(a, 2, 0), 0.0) + c0_ref[:, lo:hi]
            conv_ref[:, lo:hi] = a
        c = (cb_ref[:, lo:hi] + a2 * cw_ref[0:1, lo:hi] + a1 * cw_ref[1:2, lo:hi]
             + a * cw_ref[2:3, lo:hi])
        act = 0.5 * c * (1.0 + lax.erf(c * (2.0 ** -0.5))) * g
        yi = _dot(act.astype(BF16), wd_ref[lo:hi, :])
        y = yi if y is None else y + yi
    out_ref[...] = _layer_norm(alpha * h1 + y, l2g_ref[...], l2b_ref[...])


def _finish(x, oa, ob, c0, c1, weights, alpha, n_seq, seq_tiles, tm, seq_rows):
    n, d_model = x.shape
    d_ff = weights[-3].shape[0]
    row = lambda s, t: (s * seq_tiles + t, 0)
    const = lambda s, t: (0, 0)
    once = pl.Buffered(1)

    def wspec(w):
        return pl.BlockSpec(w.shape, const, pipeline_mode=once)

    if seq_rows is None:
        c_specs = [pl.BlockSpec((None, CONV_W - 1, d_ff), lambda s, t: (s, 0, 0))] * 2
        conv_shape = jax.ShapeDtypeStruct((n_seq, CONV_W - 1, d_ff), F32)
        conv_spec = pl.BlockSpec((None, CONV_W - 1, d_ff), lambda s, t: (s, 0, 0))
        scratch = [pltpu.VMEM((tm + 8, d_ff), F32)]
    else:
        c_specs = [pl.BlockSpec((tm, d_ff), row)] * 2
        conv_shape = jax.ShapeDtypeStruct((n, d_ff), F32)
        conv_spec = pl.BlockSpec((tm, d_ff), row)
        scratch = []
    return pl.pallas_call(
        functools.partial(_finish_kernel, alpha, d_ff, seq_rows),
        grid=(n_seq, seq_tiles),
        in_specs=[pl.BlockSpec((tm, d_model), row),
                  pl.BlockSpec((tm, MIX_W), row),
                  pl.BlockSpec((tm, MIX_W), row)] + c_specs + [wspec(w) for w in weights],
        out_specs=[pl.BlockSpec((tm, d_model), row), conv_spec],
        out_shape=[jax.ShapeDtypeStruct((n, d_model), F32), conv_shape],
        scratch_shapes=scratch,
        compiler_params=pltpu.CompilerParams(
            dimension_semantics=("parallel", "arbitrary"), vmem_limit_bytes=VMEM_LIMIT),
        name="finish_layer",
    )(x, oa, ob, c0, c1, *weights)


MIXER_TILE = 512
HGRN_STEP = 16
HGRN_UNROLL = 8
SAMPLE_SEQS_PER_STEP = 8
FINISH_TILE = 512
FINISH_ROW_GROUPS = 2
FFN_CHUNK = 512
PAGES_PER_STEP = 16
DECODE_GROUPS = 8
NEW_KEY_PAD = 16


def _rope_tables(pos):
    half = HEAD_DIM_A // 2
    inv = ROPE_THETA ** (-np.arange(half, dtype=np.float64) * 2.0 / HEAD_DIM_A)
    ang = np.asarray(pos, np.float64)[:, None] * inv[None, :]
    cos = np.cos(ang)
    sin = np.sin(ang)
    reps = LANES // HEAD_DIM_A
    return (jnp.asarray(np.tile(np.concatenate([cos, cos], axis=1), (1, reps)), F32),
            jnp.asarray(np.tile(np.concatenate([-sin, sin], axis=1), (1, reps)), F32))


def kernel(x_prompt, x_sample, cache_k, cache_v, state_hgrn, state_conv, page_table, w_in, lambda_q1, lambda_k1, lambda_q2, lambda_k2, subln_g, lb_logits, hgrn_norm_g, w_branch_a, w_branch_b, w_out, ln1_g, ln1_b, w_up, conv_w, conv_b, w_down, ln2_g, ln2_b):
    bp, tp, d_model = x_prompt.shape
    bs, ts, _ = x_sample.shape
    depth = w_in.shape[0]
    n_phys, _, page, _, _, _ = cache_k.shape
    past_len = page_table.shape[1] * page
    d_ff = w_down.shape[1]
    alpha = (2 * depth) ** 0.25
    n_mix = 6 * MIX_W
    ts_pad = 8
    lb_logits = lb_logits.astype(F32)

    cos_p, sin_p = _rope_tables(np.arange(tp))
    cos_s, sin_s = _rope_tables(past_len + np.arange(bs * ts) % ts)
    cache_kt = jnp.transpose(cache_k, (0, 1, 3, 4, 5, 2)).reshape(n_phys, depth, MIX_W, page)
    cache_vr = cache_v.reshape(n_phys, depth, page * N_HEADS_A, HEAD_W)

    xp = x_prompt.reshape(bp * tp, d_model)
    xs = x_sample.reshape(bs * ts, d_model)
    outs = {name: [] for name in ("kp", "vp", "sp", "cp", "ks", "vs", "ss", "cs")}
    for l in range(depth):
        lam_init = 0.8 - 0.6 * math.exp(-0.3 * l)
        w_mix = w_in[l, :, :n_mix].astype(BF16)
        lam_vecs = jnp.stack([lambda_q1[l], lambda_k1[l], lambda_q2[l], lambda_k2[l]]).astype(F32)
        g_row = subln_g[l].reshape(1, HEAD_W).astype(F32)
        weights = (
            w_in[l, :, n_mix:].astype(BF16),
            hgrn_norm_g[l].reshape(1, KEY_DIM_B).astype(F32),
            w_branch_a[l].astype(BF16), w_branch_b[l].astype(BF16), w_out[l].astype(BF16),
            ln1_g[l].reshape(1, d_model), ln1_b[l].reshape(1, d_model),
            w_up[l].astype(BF16), conv_w[l], conv_b[l].reshape(1, d_ff),
            w_down[l].astype(BF16), ln2_g[l].reshape(1, d_model), ln2_b[l].reshape(1, d_model))

        k_p, v_p, qt3, k_bf, vt3, qb, kk, lf, ib = _mixer_proj(
            xp, w_mix, cos_p, sin_p, lb_logits, l, True, MIXER_TILE)
        k_s, v_s, q_s, qb_s, kk_s, lf_s, ib_s = _mixer_proj(
            xs, w_mix, cos_s, sin_s, lb_logits, l, False, bs * ts)
        pad_t = lambda a, rows: jnp.pad(a.reshape(bs, ts, MIX_W), ((0, 0), (0, rows - ts), (0, 0)))
        q_rep = jnp.tile(q_s.reshape(bs, ts, MIX_W), (1, 2 * N_HEADS_A, 1))

        oa = _prompt_attention(qt3, k_bf, vt3, lam_vecs, g_row.reshape(HEAD_W, 1), lam_init,
                               bp, tp, MIXER_TILE)
        s0 = jnp.zeros((bp, N_HEADS_B, KEY_DIM_B, KEY_DIM_B), F32)
        ob, s_p, oa_s = _hgrn_scan_with_decode(
            qb, kk, lf, ib, s0, bp, tp, HGRN_STEP,
            page_table, q_rep, pad_t(k_s, NEW_KEY_PAD), pad_t(v_s, NEW_KEY_PAD),
            cache_kt, cache_vr, lam_vecs, g_row, lam_init, l, ts, PAGES_PER_STEP)
        conv0 = jnp.zeros((bp, CONV_W - 1, d_ff), F32)
        xp_new, conv_p = _finish(xp, oa, ob, conv0, conv0, weights, alpha,
                                 bp, tp // FINISH_TILE, FINISH_TILE, None)
        outs["kp"].append(jnp.transpose(k_p.reshape(bp, N_HEADS_A, 2, HEAD_DIM_A, tp), (0, 4, 1, 2, 3)))
        outs["vp"].append(v_p.reshape(bp, tp, N_HEADS_A, HEAD_W))
        outs["sp"].append(s_p)
        outs["cp"].append(conv_p)

        flat = lambda a: pad_t(a, ts_pad).reshape(bs * ts_pad, MIX_W)
        ob, s_s = _hgrn_scan(flat(qb_s), flat(kk_s), flat(lf_s), flat(ib_s), state_hgrn[:, l],
                             bs, ts_pad, ts_pad, ts_pad, math.gcd(bs, SAMPLE_SEQS_PER_STEP))
        ob = ob.reshape(bs, ts_pad, MIX_W)[:, :ts].reshape(bs * ts, MIX_W)
        st = state_conv[:, l]
        c0 = jnp.concatenate([st, jnp.zeros((bs, ts - 2, d_ff), F32)], axis=1)
        c1 = jnp.concatenate([st[:, 1:], jnp.zeros((bs, ts - 1, d_ff), F32)], axis=1)
        xs_new, a_s = _finish(xs, oa_s.reshape(bs * ts, MIX_W), ob,
                              c0.reshape(bs * ts, d_ff), c1.reshape(bs * ts, d_ff),
                              weights, alpha, 1, 1, bs * ts, ts)
        conv_s = jnp.concatenate([st, a_s.reshape(bs, ts, d_ff)], axis=1)[:, -(CONV_W - 1):]
        outs["ks"].append(k_s.reshape(bs, ts, N_HEADS_A, 2, HEAD_DIM_A))
        outs["vs"].append(v_s.reshape(bs, ts, N_HEADS_A, HEAD_W))
        outs["ss"].append(s_s)
        outs["cs"].append(conv_s)
        xp, xs = xp_new, xs_new

    stack = lambda name: jnp.stack(outs[name], axis=1)
    return (xp.reshape(bp, tp, d_model), xs.reshape(bs, ts, d_model),
            stack("kp"), stack("vp"), stack("sp"), stack("cp"),
            stack("ks"), stack("vs"), stack("ss"), stack("cs"))
```

```python
import functools
import math

import jax
import jax.numpy as jnp
import numpy as np
from jax import lax
from jax.experimental import pallas as pl
from jax.experimental.pallas import tpu as pltpu

F32 = jnp.float32
BF16 = jnp.bfloat16

N_HEADS_A = 4
HEAD_DIM_A = 64
N_HEADS_B = 4
KEY_DIM_B = 128
CONV_W = 3
ROPE_THETA = 10000.0
LN_EPS = 1e-5
RMS_EPS = 1e-5

LANES = 128
HEAD_W = 2 * HEAD_DIM_A
MIX_W = N_HEADS_A * HEAD_W
NEG_BIG = -1e30
SUM_ROWS = 16

VMEM_LIMIT = 56 * 1024 * 1024


def _dot(a, b):
    return jnp.dot(a, b, preferred_element_type=F32)


def _dot_nt(a, b):
    return lax.dot_general(a, b, (((1,), (1,)), ((), ())), preferred_element_type=F32)


def _dot_tn(a, b):
    return lax.dot_general(a, b, (((0,), (0,)), ((), ())), preferred_element_type=F32)


def _layer_norm(x, g, b):
    mu = jnp.mean(x, axis=-1, keepdims=True)
    xc = x - mu
    var = jnp.mean(xc * xc, axis=-1, keepdims=True)
    return xc * lax.rsqrt(var + LN_EPS) * g + b


def _lambda(lam_ref, lam_init):
    lv = lam_ref[...]
    e1 = jnp.exp(jnp.sum(lv[0:1] * lv[1:2], axis=-1, keepdims=True))
    e2 = jnp.exp(jnp.sum(lv[2:3] * lv[3:4], axis=-1, keepdims=True))
    return e1 - e2 + lam_init


def _rope(x, cos, sin_signed):
    lane = lax.broadcasted_iota(jnp.int32, x.shape, 1)
    first_half = (lane % HEAD_DIM_A) < (HEAD_DIM_A // 2)
    partner = jnp.where(first_half, pltpu.roll(x, LANES - HEAD_DIM_A // 2, 1),
                        pltpu.roll(x, HEAD_DIM_A // 2, 1))
    return x * cos + partner * sin_signed


def _mixer_kernel(layer, transposed, x_ref, w_ref, cos_ref, sin_ref, lbl_ref, *out_refs):
    if transposed:
        k_ref, v_ref, qt_ref, kb_ref, vt_ref, qb_ref, kk_ref, lf_ref, ib_ref = out_refs
    else:
        k_ref, v_ref, q_ref, qb_ref, kk_ref, lf_ref, ib_ref = out_refs
    xb = x_ref[...].astype(BF16)
    cos = cos_ref[...]
    sin = sin_ref[...]

    def section(i):
        return _dot(xb, w_ref[:, i * MIX_W:(i + 1) * MIX_W])

    scale = HEAD_DIM_A ** -0.5 * math.log2(math.e)
    qa = section(0)
    qa = jnp.concatenate([_rope(qa[:, h * HEAD_W:(h + 1) * HEAD_W], cos, sin) * scale
                          for h in range(N_HEADS_A)], axis=1)
    ka = section(1)
    ka = jnp.concatenate([_rope(ka[:, h * HEAD_W:(h + 1) * HEAD_W], cos, sin)
                          for h in range(N_HEADS_A)], axis=1)
    va = section(2)
    if transposed:
        k_ref[...] = ka.T
        for h in range(N_HEADS_A):
            v_ref[pl.ds(h, va.shape[0], stride=N_HEADS_A), :] = va[:, h * HEAD_W:(h + 1) * HEAD_W]
        qt_ref[...] = qa.T.astype(BF16)
        kb_ref[...] = ka.astype(BF16)
        vt_ref[...] = va.T.astype(BF16)
    else:
        k_ref[...] = ka
        v_ref[...] = va
        q_ref[...] = qa

    qb_ref[...] = section(3)
    fr = section(4)
    ib_ref[...] = section(5)

    lg = lbl_ref[...]
    e = jnp.exp(lg - jnp.max(lg, axis=0, keepdims=True))
    lb = jnp.sum(e[:layer + 1], axis=0, keepdims=True) / jnp.sum(e, axis=0, keepdims=True)
    f = lb + (1.0 - lb) * (1.0 / (1.0 + jnp.exp(-fr)))
    lf_ref[...] = jnp.log2(f)
    kk_ref[...] = jnp.log2((1.0 - lb) * (1.0 / (1.0 + jnp.exp(fr))))


def _mixer_proj(x, w_mix, cos, sin, lb_logits, layer, transposed, tm):
    n = x.shape[0]
    d = x.shape[1]
    n_tiles = n // tm
    pos_tiles = cos.shape[0] // tm
    row = lambda i: (i, 0)
    f32_out = jax.ShapeDtypeStruct((n, MIX_W), F32)
    row_spec = pl.BlockSpec((tm, MIX_W), row)
    if transposed:
        t_shape = jax.ShapeDtypeStruct((n_tiles, MIX_W, tm), BF16)
        t_spec = pl.BlockSpec((None, MIX_W, tm), lambda i: (i, 0, 0))
        kt_shape = jax.ShapeDtypeStruct((n // cos.shape[0], MIX_W, cos.shape[0]), F32)
        kt_spec = pl.BlockSpec((None, MIX_W, tm), lambda i: (i // pos_tiles, 0, i % pos_tiles))
        v_shape = jax.ShapeDtypeStruct((n * N_HEADS_A, HEAD_W), F32)
        v_spec = pl.BlockSpec((tm * N_HEADS_A, HEAD_W), row)
        out_shape = [kt_shape, v_shape, t_shape, jax.ShapeDtypeStruct((n, MIX_W), BF16), t_shape,
                     f32_out, f32_out, f32_out, f32_out]
        out_specs = [kt_spec, v_spec, t_spec, row_spec, t_spec,
                     row_spec, row_spec, row_spec, row_spec]
    else:
        out_shape = [f32_out] * 7
        out_specs = [row_spec] * 7
    return pl.pallas_call(
        functools.partial(_mixer_kernel, layer, transposed),
        grid=(n_tiles,),
        in_specs=[
            pl.BlockSpec((tm, d), row),
            pl.BlockSpec(w_mix.shape, lambda i: (0, 0)),
            pl.BlockSpec((tm, LANES), lambda i: (i % pos_tiles, 0)),
            pl.BlockSpec((tm, LANES), lambda i: (i % pos_tiles, 0)),
            pl.BlockSpec(lb_logits.shape, lambda i: (0, 0)),
        ],
        out_specs=out_specs,
        out_shape=out_shape,
        compiler_params=pltpu.CompilerParams(
            dimension_semantics=("parallel",), vmem_limit_bytes=VMEM_LIMIT),
        name="mixer_proj",
    )(x, w_mix, cos, sin, lb_logits)


def _prompt_attn_kernel(lam_init, tq, qt_ref, k_ref, vt_ref, lam_ref, g_ref, o_ref,
                        m_ref, acc_ref, sa_ref, sb_ref, qm_ref, fin_ref):
    nq = qt_ref.shape[0]
    ones = jnp.ones((SUM_ROWS, tq), BF16)
    lam = _lambda(lam_ref, lam_init)

    def query_maps(t):
        qt = qt_ref[t]
        qrow = lax.broadcasted_iota(jnp.int32, qt.shape, 0)
        zero = jnp.zeros_like(qt)
        return jnp.where(qrow < HEAD_DIM_A, qt, zero), jnp.where(qrow >= HEAD_DIM_A, qt, zero)

    def first_scores(t, s_ref):
        kj = k_ref[0:tq, :]
        for mp, q in enumerate(query_maps(t)):
            s_ref[mp] = _dot(kj, q)

    def scores(j, s_ref):
        kj = k_ref[pl.ds(pl.multiple_of(j * tq, tq), tq), :]
        for mp in range(2):
            s_ref[mp] = _dot(kj, qm_ref[mp])

    def consume(j, s_ref, masked):
        vj = jnp.concatenate([vt_ref[j], ones], axis=0)
        for mp in range(2):
            s = s_ref[mp]
            if masked:
                krow = lax.broadcasted_iota(jnp.int32, s.shape, 0)
                qcol = lax.broadcasted_iota(jnp.int32, s.shape, 1)
                s = jnp.where(krow <= qcol, s, NEG_BIG)
            m_old = m_ref[mp]
            m_new = jnp.maximum(m_old, jnp.max(s, axis=0, keepdims=True))
            alpha = jnp.exp2(m_old - m_new)
            p = jnp.exp2(s - m_new).astype(BF16)
            acc_ref[mp] = alpha * acc_ref[mp] + _dot(vj, p)
            m_ref[mp] = m_new

    def pair(j):
        scores(j + 1, sb_ref)
        consume(j, sa_ref, False)
        scores(j + 2, sa_ref)
        consume(j + 1, sb_ref, False)

    def trips(n_pairs, first, count):
        def body(jj, carry):
            for i in range(n_pairs):
                pair(first + 2 * (n_pairs * jj + i))
            return carry
        lax.fori_loop(0, count, body, 0)

    def tile(qi, carry):
        for mp, q in enumerate(query_maps(qi)):
            qm_ref[mp] = q
        m_ref[...] = jnp.full(m_ref.shape, NEG_BIG, F32)
        acc_ref[...] = jnp.zeros(acc_ref.shape, F32)
        nxt = jnp.minimum(qi + 1, nq - 1)

        trips(4, 0, qi // 8)
        trips(2, 8 * (qi // 8), (qi % 8) // 4)
        trips(1, 4 * (qi // 4), (qi % 4) // 2)

        prev = jnp.maximum(qi - 1, 0)

        @pl.when(qi % 2 == 0)
        def _():
            first_scores(nxt, sb_ref)
            emit(prev)
            consume(qi, sa_ref, True)
            sa_ref[...] = sb_ref[...]

        @pl.when(qi % 2 == 1)
        def _():
            scores(qi, sb_ref)
            emit(prev)
            consume(qi - 1, sa_ref, False)
            first_scores(nxt, sa_ref)
            consume(qi, sb_ref, True)

        fin_ref[...] = acc_ref[...]
        return carry

    def emit(t):
        acc1 = fin_ref[0]
        acc2 = fin_ref[1]
        o = (acc1[:HEAD_W] / acc1[HEAD_W:HEAD_W + 1]
             - lam * (acc2[:HEAD_W] / acc2[HEAD_W:HEAD_W + 1]))
        ms = jnp.mean(o * o, axis=0, keepdims=True)
        o = o * lax.rsqrt(ms + RMS_EPS) * g_ref[...] * (1.0 - lam_init)
        row0 = t * tq if isinstance(t, int) else pl.multiple_of(t * tq, tq)
        o_ref[pl.ds(row0, tq), :] = o.T.astype(o_ref.dtype)

    fin_ref[...] = jnp.ones(fin_ref.shape, F32)
    first_scores(0, sa_ref)
    lax.fori_loop(0, nq, tile, 0)
    emit(nq - 1)


def _prompt_attention(qt3, k_bf, vt3, lam_vecs, g_col, lam_init, batch, seq, tq):
    nq = seq // tq
    n = batch * seq
    return pl.pallas_call(
        functools.partial(_prompt_attn_kernel, lam_init, tq),
        grid=(batch, N_HEADS_A),
        in_specs=[
            pl.BlockSpec((nq, HEAD_W, tq), lambda b, h: (b, h, 0)),
            pl.BlockSpec((seq, HEAD_W), lambda b, h: (b, h)),
            pl.BlockSpec((nq, HEAD_W, tq), lambda b, h: (b, h, 0)),
            pl.BlockSpec(lam_vecs.shape, lambda b, h: (0, 0)),
            pl.BlockSpec(g_col.shape, lambda b, h: (0, 0)),
        ],
        out_specs=pl.BlockSpec((seq, HEAD_W), lambda b, h: (b, h)),
        out_shape=jax.ShapeDtypeStruct((n, MIX_W), BF16),
        scratch_shapes=[
            pltpu.VMEM((2, 1, tq), F32),
            pltpu.VMEM((2, HEAD_W + SUM_ROWS, tq), F32),
            pltpu.VMEM((2, tq, tq), F32),
            pltpu.VMEM((2, tq, tq), F32),
            pltpu.VMEM((2, HEAD_W, tq), BF16),
            pltpu.VMEM((2, HEAD_W + SUM_ROWS, tq), F32),
        ],
        compiler_params=pltpu.CompilerParams(
            dimension_semantics=("parallel", "parallel"),
            vmem_limit_bytes=VMEM_LIMIT),
        name="prompt_diff_attention",
    )(qt3, k_bf, vt3, lam_vecs, g_col)


class _DecodeState:
    def __init__(self, q_ref, m_ref, l_ref, acc_ref, dec_seq):
        self.m_ref, self.l_ref, self.acc_ref, self.dec_seq = m_ref, l_ref, acc_ref, dec_seq
        qrep = q_ref[...]
        rrow = lax.broadcasted_iota(jnp.int32, qrep.shape, 0)
        rcol = lax.broadcasted_iota(jnp.int32, qrep.shape, 1)
        self.qbd = jnp.where(rcol // HEAD_DIM_A == rrow // dec_seq, qrep, 0.0).astype(BF16)

    def init(self):
        self.m_ref[...] = jnp.full(self.m_ref.shape, NEG_BIG, F32)
        self.l_ref[...] = jnp.zeros(self.l_ref.shape, F32)
        self.acc_ref[...] = jnp.zeros(self.acc_ref.shape, F32)

    def update(self, s, v_bf):
        m_old = self.m_ref[...]
        m_new = jnp.maximum(m_old, jnp.max(s, axis=-1, keepdims=True))
        alpha = jnp.exp2(m_old - m_new)
        p = jnp.exp2(s - m_new)
        self.l_ref[...] = alpha * self.l_ref[...] + jnp.sum(p, axis=-1, keepdims=True)
        self.acc_ref[...] = alpha * self.acc_ref[...] + _dot(p.astype(BF16), v_bf)
        self.m_ref[...] = m_new

    def pages(self, k_refs, v_refs):
        page = k_refs[0].shape[1]
        kt = jnp.concatenate([r[...] for r in k_refs], axis=1).astype(BF16)
        vp = jnp.concatenate(
            [jnp.concatenate([r[pl.ds(h, page, stride=N_HEADS_A), :] for h in range(N_HEADS_A)],
                             axis=1) for r in v_refs], axis=0).astype(BF16)
        self.update(_dot(self.qbd, kt), vp)

    def finish(self, kn_ref, vn_ref, lam_ref, g_ref, o_ref, lam_init):
        dec_seq = self.dec_seq
        kn = kn_ref[...].astype(BF16)
        vn = vn_ref[...].astype(BF16)
        s = _dot_nt(self.qbd, kn)
        key = lax.broadcasted_iota(jnp.int32, s.shape, 1)
        tok = lax.broadcasted_iota(jnp.int32, s.shape, 0) % dec_seq
        self.update(jnp.where(key <= tok, s, NEG_BIG), vn)
        lam = _lambda(lam_ref, lam_init)
        acc = self.acc_ref[...] / self.l_ref[...]
        rows_per_head = 2 * dec_seq
        outs = []
        for h in range(N_HEADS_A):
            blk = acc[h * rows_per_head:(h + 1) * rows_per_head, h * HEAD_W:(h + 1) * HEAD_W]
            o = blk[:dec_seq] - lam * blk[dec_seq:]
            ms = jnp.mean(o * o, axis=-1, keepdims=True)
            outs.append(o * lax.rsqrt(ms + RMS_EPS) * g_ref[...] * (1.0 - lam_init))
        o_ref[...] = jnp.concatenate(outs, axis=1).astype(o_ref.dtype)


def _hgrn_steps(step, q_ref, k_ref, lf_ref, v_ref, o_ref, st_ref):
    ones = jnp.ones((KEY_DIM_B, LANES), BF16)
    rowi = lax.broadcasted_iota(jnp.int32, (step, KEY_DIM_B), 0)

    def advance(r0):
        for h in range(N_HEADS_B):
            cols = slice(h * KEY_DIM_B, (h + 1) * KEY_DIM_B)
            q = q_ref[pl.ds(r0, step), cols]
            lk = k_ref[pl.ds(r0, step), cols]
            v = v_ref[pl.ds(r0, step), cols]
            b = lf_ref[pl.ds(r0, step), cols]
            d = 1
            while d < step:
                b = b + jnp.where(rowi >= d, pltpu.roll(b, d, 0), 0.0)
                d *= 2
            st = st_ref[h]
            o = _dot_nt((q * jnp.exp2(b)).astype(BF16), st.astype(BF16))
            c = b - lk
            xs = []
            for s in range(step):
                lo = (s // 8) * 8
                e = jnp.exp2(jnp.where(rowi[lo:] >= s, b[lo:] - c[s:s + 1], NEG_BIG))
                xs.append(q[lo:] * e)
            att = _dot(jnp.concatenate(xs, axis=0).astype(BF16), ones)
            parts = [o[lo:lo + 8] for lo in range(0, step, 8)]
            off = 0
            for s in range(step):
                lo = (s // 8) * 8
                for g in range(lo // 8, step // 8):
                    parts[g] = parts[g] + att[off:off + 8] * v[s:s + 1]
                    off += 8
            o_ref[pl.ds(r0, step), cols] = jnp.concatenate(parts, axis=0)
            b_last = b[step - 1:step]
            kd = jnp.exp2(b_last - c)
            st_ref[h] = st * jnp.exp2(b_last) + _dot_tn(v.astype(BF16), kd.astype(BF16))

    return advance


def _hgrn_state_io(s0_ref, sout_ref, st_ref):
    ti = pl.program_id(1)

    def load():
        @pl.when(ti == 0)
        def _():
            for h in range(N_HEADS_B):
                st_ref[h] = s0_ref[h].T

    def store():
        @pl.when(ti == pl.num_programs(1) - 1)
        def _():
            for h in range(N_HEADS_B):
                sout_ref[h] = st_ref[h].T

    return load, store


def _hgrn_kernel(step, n_steps, group, q_ref, k_ref, lf_ref, v_ref, s0_ref, o_ref, sout_ref,
                 st_ref):
    advance = _hgrn_steps(step, q_ref, k_ref, lf_ref, v_ref, o_ref, st_ref)
    for g in range(group):
        load, store = _hgrn_state_io(s0_ref.at[g], sout_ref.at[g], st_ref)
        load()

        def body(i, carry, first=g * n_steps * step):
            advance(pl.multiple_of(first + i * step, step))
            return carry

        lax.fori_loop(0, n_steps, body, 0, unroll=math.gcd(n_steps, HGRN_UNROLL))
        store()


def _hgrn_scan(q, k, lf, v, s0, n_seq, seq_rows, tile_rows, step, group=1):
    n = q.shape[0]
    tiles = seq_rows // tile_rows
    assert group == 1 or (tiles == 1 and n_seq % group == 0)
    row = lambda s, t: (s * tiles + t, 0)
    spec = pl.BlockSpec((group * tile_rows, MIX_W), row)
    st_spec = pl.BlockSpec((group, N_HEADS_B, KEY_DIM_B, KEY_DIM_B), lambda s, t: (s, 0, 0, 0))
    return pl.pallas_call(
        functools.partial(_hgrn_kernel, step, tile_rows // step, group),
        grid=(n_seq // group, tiles),
        in_specs=[spec, spec, spec, spec, st_spec],
        out_specs=[spec, st_spec],
        out_shape=[jax.ShapeDtypeStruct((n, MIX_W), F32),
                   jax.ShapeDtypeStruct(s0.shape, F32)],
        scratch_shapes=[pltpu.VMEM((N_HEADS_B, KEY_DIM_B, KEY_DIM_B), F32)],
        compiler_params=pltpu.CompilerParams(
            dimension_semantics=("parallel", "arbitrary"), vmem_limit_bytes=VMEM_LIMIT),
        name="hgrn2_scan",
    )(q, k, lf, v, s0)


def _hgrn_decode_kernel(lam_init, layer, step, n_steps, n_pages, steps_per_sample, dec_seq, pt_ref,
                        q_ref, k_ref, lf_ref, v_ref, s0_ref,
                        qs_ref, kn_ref, vn_ref, lam_ref, g_ref, ck_hbm, cv_hbm,
                        o_ref, sout_ref, oa_ref,
                        st_ref, m_ref, l_ref, acc_ref, kbuf, vbuf, sems):
    lin = pl.program_id(0) * pl.num_programs(1) + pl.program_id(1)
    last = pl.num_programs(0) * pl.num_programs(1) - 1
    phase = pl.program_id(1) % steps_per_sample
    slot = lin % 2

    def page_copies(at_step, at_slot):
        copies = []
        for pg in range(n_pages):
            page = pt_ref[at_step, pg]
            copies.append(pltpu.make_async_copy(ck_hbm.at[page, layer], kbuf.at[at_slot, pg],
                                                sems.at[at_slot, 0]))
            copies.append(pltpu.make_async_copy(cv_hbm.at[page, layer], vbuf.at[at_slot, pg],
                                                sems.at[at_slot, 1]))
        return copies

    def start_all(copies):
        for i, c in enumerate(copies):
            c.start(priority=i % 2)

    @pl.when(lin == 0)
    def _():
        start_all(page_copies(0, 0))

    @pl.when(lin < last)
    def _():
        start_all(page_copies(lin + 1, 1 - slot))

    load, store = _hgrn_state_io(s0_ref, sout_ref, st_ref)
    advance = _hgrn_steps(step, q_ref, k_ref, lf_ref, v_ref, o_ref, st_ref)
    dec = _DecodeState(qs_ref, m_ref, l_ref, acc_ref, dec_seq)
    load()
    pl.when(phase == 0)(dec.init)

    for c in page_copies(lin, slot):
        c.wait()
    k_pages = [kbuf.at[slot, pg] for pg in range(n_pages)]
    v_pages = [vbuf.at[slot, pg] for pg in range(n_pages)]

    n_groups = math.gcd(math.gcd(n_steps, n_pages), DECODE_GROUPS)
    for i in range(n_groups):
        lo, hi = i * n_pages // n_groups, (i + 1) * n_pages // n_groups
        dec.pages(k_pages[lo:hi], v_pages[lo:hi])
        for r in range(i * n_steps // n_groups, (i + 1) * n_steps // n_groups):
            advance(r * step)

    store()

    @pl.when(phase == steps_per_sample - 1)
    def _():
        dec.finish(kn_ref, vn_ref, lam_ref, g_ref, oa_ref, lam_init)


def _hgrn_scan_with_decode(q, k, lf, v, s0, n_seq, seq_rows, step,
                           page_table, q_rep, k_new, v_new, cache_kt, cache_v, lam_vecs, g_row,
                           lam_init, layer, dec_seq, pages_per_step):
    n = q.shape[0]
    bs, n_pages = page_table.shape
    page = cache_kt.shape[3]
    n_rows = q_rep.shape[1]
    steps_per_sample = n_pages // pages_per_step
    total_steps = bs * steps_per_sample
    tiles = total_steps // n_seq
    tile_rows = seq_rows // tiles
    assert tiles * n_seq == total_steps and tile_rows * tiles == seq_rows
    assert tile_rows % step == 0 and tiles % steps_per_sample == 0

    row = lambda s, t, pt: (s * tiles + t, 0)
    spec = pl.BlockSpec((tile_rows, MIX_W), row)
    st_spec = pl.BlockSpec((None, N_HEADS_B, KEY_DIM_B, KEY_DIM_B), lambda s, t, pt: (s, 0, 0, 0))
    sample = lambda s, t, pt: ((s * tiles + t) // steps_per_sample, 0, 0)
    const = lambda s, t, pt: (0, 0)

    hbm = pl.BlockSpec(memory_space=pl.ANY)
    grid_spec = pltpu.PrefetchScalarGridSpec(
        num_scalar_prefetch=1,
        grid=(n_seq, tiles),
        in_specs=[spec, spec, spec, spec, st_spec,
                  pl.BlockSpec((None, n_rows, MIX_W), sample),
                  pl.BlockSpec((None, k_new.shape[1], MIX_W), sample),
                  pl.BlockSpec((None, k_new.shape[1], MIX_W), sample),
                  pl.BlockSpec(lam_vecs.shape, const),
                  pl.BlockSpec(g_row.shape, const), hbm, hbm],
        out_specs=[spec, st_spec, pl.BlockSpec((None, dec_seq, MIX_W), sample)],
        scratch_shapes=[
            pltpu.VMEM((N_HEADS_B, KEY_DIM_B, KEY_DIM_B), F32),
            pltpu.VMEM((n_rows, 1), F32),
            pltpu.VMEM((n_rows, 1), F32),
            pltpu.VMEM((n_rows, MIX_W), F32),
            pltpu.VMEM((2, pages_per_step, MIX_W, page), F32),
            pltpu.VMEM((2, pages_per_step, page * N_HEADS_A, HEAD_W), F32),
            pltpu.SemaphoreType.DMA((2, 2)),
        ],
    )
    return pl.pallas_call(
        functools.partial(_hgrn_decode_kernel, lam_init, layer, step, tile_rows // step,
                          pages_per_step, steps_per_sample, dec_seq),
        grid_spec=grid_spec,
        out_shape=[jax.ShapeDtypeStruct((n, MIX_W), F32),
                   jax.ShapeDtypeStruct(s0.shape, F32),
                   jax.ShapeDtypeStruct((bs, dec_seq, MIX_W), BF16)],
        compiler_params=pltpu.CompilerParams(
            dimension_semantics=("arbitrary", "arbitrary"), vmem_limit_bytes=VMEM_LIMIT),
        name="hgrn2_scan_with_decode_attention",
    )(page_table.reshape(total_steps, pages_per_step), q, k, lf, v, s0, q_rep, k_new, v_new,
      lam_vecs, g_row, cache_kt, cache_v)


def _finish_kernel(alpha, d_ff, seq_rows, x_ref, oa_ref, ob_ref, c0_ref, c1_ref,
                   wg_ref, ng_ref, wa_ref, wb_ref, wo_ref, l1g_ref, l1b_ref, wu_ref,
                   cw_ref, cb_ref, wd_ref, l2g_ref, l2b_ref, out_ref, conv_ref, *scratch):
    tm, d_model = x_ref.shape
    n_grp = FINISH_ROW_GROUPS if tm % (FINISH_ROW_GROUPS * LANES) == 0 else 1
    grp = [slice(i * (tm // n_grp), (i + 1) * (tm // n_grp)) for i in range(n_grp)]
    xs = [x_ref[r, :] for r in grp]
    gates = [_dot(x.astype(BF16), wg_ref[...]) for x in xs]

    def hgrn_out(r, gt):
        ob = ob_ref[r, :]
        gb = gt[:, :MIX_W]
        normed = []
        for h in range(N_HEADS_B):
            oh = ob[:, h * KEY_DIM_B:(h + 1) * KEY_DIM_B]
            ms = jnp.mean(oh * oh, axis=-1, keepdims=True)
            normed.append(oh * lax.rsqrt(ms + RMS_EPS) * ng_ref[...])
        return (jnp.concatenate(normed, axis=1) * (gb * jax.nn.sigmoid(gb))).astype(BF16)

    obn = [hgrn_out(r, gt) for r, gt in zip(grp, gates)]
    br_a = [_dot(oa_ref[r, :], wa_ref[...]) for r in grp]
    br_b = [_dot(o, wb_ref[...]) for o in obn]
    merged = [(jax.nn.sigmoid(gt[:, MIX_W:MIX_W + d_model]) * a
               + jax.nn.sigmoid(gt[:, MIX_W + d_model:]) * b).astype(BF16)
              for gt, a, b in zip(gates, br_a, br_b)]
    res = [_dot(m, wo_ref[...]) for m in merged]
    h1 = jnp.concatenate([_layer_norm(alpha * x + r, l1g_ref[...], l1b_ref[...])
                          for x, r in zip(xs, res)], axis=0)

    hb = h1.astype(BF16)
    chunks = [(c, min(c + FFN_CHUNK, d_ff)) for c in range(0, d_ff, FFN_CHUNK)]

    def up(lo, hi):
        return _dot(hb, wu_ref[:, lo:hi]), _dot(hb, wu_ref[:, d_ff + lo:d_ff + hi])

    if seq_rows is None:
        (a_ref,) = scratch

        @pl.when(pl.program_id(1) == 0)
        def _():
            a_ref[6:8, :] = c0_ref[...]

    y = None
    nxt = up(*chunks[0])
    for i, (lo, hi) in enumerate(chunks):
        a, g = nxt
        if i + 1 < len(chunks):
            nxt = up(*chunks[i + 1])
        if seq_rows is None:
            a_ref[8:8 + tm, lo:hi] = a
            a1 = a_ref[7:7 + tm, lo:hi]
            a2 = a_ref[6:6 + tm, lo:hi]
            tail = a_ref[6 + tm:8 + tm, lo:hi]
            a_ref[6:8, lo:hi] = tail
            conv_ref[:, lo:hi] = tail
        else:
            pos = lax.broadcasted_iota(jnp.int32, a.shape, 0) % seq_rows
            a1 = jnp.where(pos >= 1, pltpu.roll(a, 1, 0), 0.0) + c1_ref[:, lo:hi]
            a2 = jnp.where(pos >= 2, pltpu.roll(a, 2, 0), 0.0) + c0_ref[:, lo:hi]
            conv_ref[:, lo:hi] = a
        c = (cb_ref[:, lo:hi] + a2 * cw_ref[0:1, lo:hi] + a1 * cw_ref[1:2, lo:hi]
             + a * cw_ref[2:3, lo:hi])
        act = 0.5 * c * (1.0 + lax.erf(c * (2.0 ** -0.5))) * g
        yi = _dot(act.astype(BF16), wd_ref[lo:hi, :])
        y = yi if y is None else y + yi
    out_ref[...] = _layer_norm(alpha * h1 + y, l2g_ref[...], l2b_ref[...])


def _finish(x, oa, ob, c0, c1, weights, alpha, n_seq, seq_tiles, tm, seq_rows):
    n, d_model = x.shape
    d_ff = weights[-3].shape[0]
    row = lambda s, t: (s * seq_tiles + t, 0)
    const = lambda s, t: (0, 0)
    once = pl.Buffered(1)

    def wspec(w):
        return pl.BlockSpec(w.shape, const, pipeline_mode=once)

    if seq_rows is None:
        c_specs = [pl.BlockSpec((None, CONV_W - 1, d_ff), lambda s, t: (s, 0, 0))] * 2
        conv_shape = jax.ShapeDtypeStruct((n_seq, CONV_W - 1, d_ff), F32)
        conv_spec = pl.BlockSpec((None, CONV_W - 1, d_ff), lambda s, t: (s, 0, 0))
        scratch = [pltpu.VMEM((tm + 8, d_ff), F32)]
    else:
        c_specs = [pl.BlockSpec((tm, d_ff), row)] * 2
        conv_shape = jax.ShapeDtypeStruct((n, d_ff), F32)
        conv_spec = pl.BlockSpec((tm, d_ff), row)
        scratch = []
    return pl.pallas_call(
        functools.partial(_finish_kernel, alpha, d_ff, seq_rows),
        grid=(n_seq, seq_tiles),
        in_specs=[pl.BlockSpec((tm, d_model), row),
                  pl.BlockSpec((tm, MIX_W), row),
                  pl.BlockSpec((tm, MIX_W), row)] + c_specs + [wspec(w) for w in weights],
        out_specs=[pl.BlockSpec((tm, d_model), row), conv_spec],
        out_shape=[jax.ShapeDtypeStruct((n, d_model), F32), conv_shape],
        scratch_shapes=scratch,
        compiler_params=pltpu.CompilerParams(
            dimension_semantics=("parallel", "arbitrary"), vmem_limit_bytes=VMEM_LIMIT),
        name="finish_layer",
    )(x, oa, ob, c0, c1, *weights)


MIXER_TILE = 512
HGRN_STEP = 16
HGRN_UNROLL = 8
SAMPLE_SEQS_PER_STEP = 8
FINISH_TILE = 512
FINISH_ROW_GROUPS = 2
FFN_CHUNK = 512
PAGES_PER_STEP = 16
DECODE_GROUPS = 8
NEW_KEY_PAD = 16


def _rope_tables(pos):
    half = HEAD_DIM_A // 2
    inv = ROPE_THETA ** (-np.arange(half, dtype=np.float64) * 2.0 / HEAD_DIM_A)
    ang = np.asarray(pos, np.float64)[:, None] * inv[None, :]
    cos = np.cos(ang)
    sin = np.sin(ang)
    reps = LANES // HEAD_DIM_A
    return (jnp.asarray(np.tile(np.concatenate([cos, cos], axis=1), (1, reps)), F32),
            jnp.asarray(np.tile(np.concatenate([-sin, sin], axis=1), (1, reps)), F32))


def kernel(x_prompt, x_sample, cache_k, cache_v, state_hgrn, state_conv, page_table, w_in, lambda_q1, lambda_k1, lambda_q2, lambda_k2, subln_g, lb_logits, hgrn_norm_g, w_branch_a, w_branch_b, w_out, ln1_g, ln1_b, w_up, conv_w, conv_b, w_down, ln2_g, ln2_b):
    bp, tp, d_model = x_prompt.shape
    bs, ts, _ = x_sample.shape
    depth = w_in.shape[0]
    n_phys, _, page, _, _, _ = cache_k.shape
    past_len = page_table.shape[1] * page
    d_ff = w_down.shape[1]
    alpha = (2 * depth) ** 0.25
    n_mix = 6 * MIX_W
    ts_pad = 8
    lb_logits = lb_logits.astype(F32)

    cos_p, sin_p = _rope_tables(np.arange(tp))
    cos_s, sin_s = _rope_tables(past_len + np.arange(bs * ts) % ts)
    cache_kt = jnp.transpose(cache_k, (0, 1, 3, 4, 5, 2)).reshape(n_phys, depth, MIX_W, page)
    cache_vr = cache_v.reshape(n_phys, depth, page * N_HEADS_A, HEAD_W)

    xp = x_prompt.reshape(bp * tp, d_model)
    xs = x_sample.reshape(bs * ts, d_model)
    outs = {name: [] for name in ("kp", "vp", "sp", "cp", "ks", "vs", "ss", "cs")}
    for l in range(depth):
        lam_init = 0.8 - 0.6 * math.exp(-0.3 * l)
        w_mix = w_in[l, :, :n_mix].astype(BF16)
        lam_vecs = jnp.stack([lambda_q1[l], lambda_k1[l], lambda_q2[l], lambda_k2[l]]).astype(F32)
        g_row = subln_g[l].reshape(1, HEAD_W).astype(F32)
        weights = (
            w_in[l, :, n_mix:].astype(BF16),
            hgrn_norm_g[l].reshape(1, KEY_DIM_B).astype(F32),
            w_branch_a[l].astype(BF16), w_branch_b[l].astype(BF16), w_out[l].astype(BF16),
            ln1_g[l].reshape(1, d_model), ln1_b[l].reshape(1, d_model),
            w_up[l].astype(BF16), conv_w[l], conv_b[l].reshape(1, d_ff),
            w_down[l].astype(BF16), ln2_g[l].reshape(1, d_model), ln2_b[l].reshape(1, d_model))

        k_p, v_p, qt3, k_bf, vt3, qb, kk, lf, ib = _mixer_proj(
            xp, w_mix, cos_p, sin_p, lb_logits, l, True, MIXER_TILE)
        k_s, v_s, q_s, qb_s, kk_s, lf_s, ib_s = _mixer_proj(
            xs, w_mix, cos_s, sin_s, lb_logits, l, False, bs * ts)
        pad_t = lambda a, rows: jnp.pad(a.reshape(bs, ts, MIX_W), ((0, 0), (0, rows - ts), (0, 0)))
        q_rep = jnp.tile(q_s.reshape(bs, ts, MIX_W), (1, 2 * N_HEADS_A, 1))

        oa = _prompt_attention(qt3, k_bf, vt3, lam_vecs, g_row.reshape(HEAD_W, 1), lam_init,
                               bp, tp, MIXER_TILE)
        s0 = jnp.zeros((bp, N_HEADS_B, KEY_DIM_B, KEY_DIM_B), F32)
        ob, s_p, oa_s = _hgrn_scan_with_decode(
            qb, kk, lf, ib, s0, bp, tp, HGRN_STEP,
            page_table, q_rep, pad_t(k_s, NEW_KEY_PAD), pad_t(v_s, NEW_KEY_PAD),
            cache_kt, cache_vr, lam_vecs, g_row, lam_init, l, ts, PAGES_PER_STEP)
        conv0 = jnp.zeros((bp, CONV_W - 1, d_ff), F32)
        xp_new, conv_p = _finish(xp, oa, ob, conv0, conv0, weights, alpha,
                                 bp, tp // FINISH_TILE, FINISH_TILE, None)
        outs["kp"].append(jnp.transpose(k_p.reshape(bp, N_HEADS_A, 2, HEAD_DIM_A, tp), (0, 4, 1, 2, 3)))
        outs["vp"].append(v_p.reshape(bp, tp, N_HEADS_A, HEAD_W))
        outs["sp"].append(s_p)
        outs["cp"].append(conv_p)

        flat = lambda a: pad_t(a, ts_pad).reshape(bs * ts_pad, MIX_W)
        ob, s_s = _hgrn_scan(flat(qb_s), flat(kk_s), flat(lf_s), flat(ib_s), state_hgrn[:, l],
                             bs, ts_pad, ts_pad, ts_pad, math.gcd(bs, SAMPLE_SEQS_PER_STEP))
        ob = ob.reshape(bs, ts_pad, MIX_W)[:, :ts].reshape(bs * ts, MIX_W)
        st = state_conv[:, l]
        c0 = jnp.concatenate([st, jnp.zeros((bs, ts - 2, d_ff), F32)], axis=1)
        c1 = jnp.concatenate([st[:, 1:], jnp.zeros((bs, ts - 1, d_ff), F32)], axis=1)
        xs_new, a_s = _finish(xs, oa_s.reshape(bs * ts, MIX_W), ob,
                              c0.reshape(bs * ts, d_ff), c1.reshape(bs * ts, d_ff),
                              weights, alpha, 1, 1, bs * ts, ts)
        conv_s = jnp.concatenate([st, a_s.reshape(bs, ts, d_ff)], axis=1)[:, -(CONV_W - 1):]
        outs["ks"].append(k_s.reshape(bs, ts, N_HEADS_A, 2, HEAD_DIM_A))
        outs["vs"].append(v_s.reshape(bs, ts, N_HEADS_A, HEAD_W))
        outs["ss"].append(s_s)
        outs["cs"].append(conv_s)
        xp, xs = xp_new, xs_new

    stack = lambda name: jnp.stack(outs[name], axis=1)
    return (xp.reshape(bp, tp, d_model), xs.reshape(bs, ts, d_model),
            stack("kp"), stack("vp"), stack("sp"), stack("cp"),
            stack("ks"), stack("vs"), stack("ss"), stack("cs"))
```

```python
import functools
import math

import jax
import jax.numpy as jnp
import numpy as np
from jax import lax
from jax.experimental import pallas as pl
from jax.experimental.pallas import tpu as pltpu

F32 = jnp.float32
BF16 = jnp.bfloat16

N_HEADS_A = 4
HEAD_DIM_A = 64
N_HEADS_B = 4
KEY_DIM_B = 128
CONV_W = 3
ROPE_THETA = 10000.0
LN_EPS = 1e-5
RMS_EPS = 1e-5

LANES = 128
HEAD_W = 2 * HEAD_DIM_A
MIX_W = N_HEADS_A * HEAD_W
NEG_BIG = -1e30
SUM_ROWS = 16

VMEM_LIMIT = 56 * 1024 * 1024


def _dot(a, b):
    return jnp.dot(a, b, preferred_element_type=F32)


def _dot_nt(a, b):
    return lax.dot_general(a, b, (((1,), (1,)), ((), ())), preferred_element_type=F32)


def _dot_tn(a, b):
    return lax.dot_general(a, b, (((0,), (0,)), ((), ())), preferred_element_type=F32)


def _layer_norm(x, g, b):
    mu = jnp.mean(x, axis=-1, keepdims=True)
    xc = x - mu
    var = jnp.mean(xc * xc, axis=-1, keepdims=True)
    return xc * lax.rsqrt(var + LN_EPS) * g + b


def _lambda(lam_ref, lam_init):
    lv = lam_ref[...]
    e1 = jnp.exp(jnp.sum(lv[0:1] * lv[1:2], axis=-1, keepdims=True))
    e2 = jnp.exp(jnp.sum(lv[2:3] * lv[3:4], axis=-1, keepdims=True))
    return e1 - e2 + lam_init


def _rope(x, cos, sin_signed):
    lane = lax.broadcasted_iota(jnp.int32, x.shape, 1)
    first_half = (lane % HEAD_DIM_A) < (HEAD_DIM_A // 2)
    partner = jnp.where(first_half, pltpu.roll(x, LANES - HEAD_DIM_A // 2, 1),
                        pltpu.roll(x, HEAD_DIM_A // 2, 1))
    return x * cos + partner * sin_signed


def _mixer_kernel(layer, transposed, x_ref, w_ref, cos_ref, sin_ref, lbl_ref, *out_refs):
    if transposed:
        k_ref, v_ref, qt_ref, kb_ref, vt_ref, qb_ref, kk_ref, lf_ref, ib_ref = out_refs
    else:
        k_ref, v_ref, q_ref, qb_ref, kk_ref, lf_ref, ib_ref = out_refs
    xb = x_ref[...].astype(BF16)
    cos = cos_ref[...]
    sin = sin_ref[...]

    def section(i):
        return _dot(xb, w_ref[:, i * MIX_W:(i + 1) * MIX_W])

    scale = HEAD_DIM_A ** -0.5 * math.log2(math.e)
    qa = section(0)
    qa = jnp.concatenate([_rope(qa[:, h * HEAD_W:(h + 1) * HEAD_W], cos, sin) * scale
                          for h in range(N_HEADS_A)], axis=1)
    ka = section(1)
    ka = jnp.concatenate([_rope(ka[:, h * HEAD_W:(h + 1) * HEAD_W], cos, sin)
                          for h in range(N_HEADS_A)], axis=1)
    va = section(2)
    if transposed:
        k_ref[...] = ka.T
        for h in range(N_HEADS_A):
            v_ref[pl.ds(h, va.shape[0], stride=N_HEADS_A), :] = va[:, h * HEAD_W:(h + 1) * HEAD_W]
        qt_ref[...] = qa.T.astype(BF16)
        kb_ref[...] = ka.astype(BF16)
        vt_ref[...] = va.T.astype(BF16)
    else:
        k_ref[...] = ka
        v_ref[...] = va
        q_ref[...] = qa

    qb_ref[...] = section(3)
    fr = section(4)
    ib_ref[...] = section(5)

    lg = lbl_ref[...]
    e = jnp.exp(lg - jnp.max(lg, axis=0, keepdims=True))
    lb = jnp.sum(e[:layer + 1], axis=0, keepdims=True) / jnp.sum(e, axis=0, keepdims=True)
    f = lb + (1.0 - lb) * (1.0 / (1.0 + jnp.exp(-fr)))
    lf_ref[...] = jnp.log2(f)
    kk_ref[...] = jnp.log2((1.0 - lb) * (1.0 / (1.0 + jnp.exp(fr))))


def _mixer_proj(x, w_mix, cos, sin, lb_logits, layer, transposed, tm):
    n = x.shape[0]
    d = x.shape[1]
    n_tiles = n // tm
    pos_tiles = cos.shape[0] // tm
    row = lambda i: (i, 0)
    f32_out = jax.ShapeDtypeStruct((n, MIX_W), F32)
    row_spec = pl.BlockSpec((tm, MIX_W), row)
    if transposed:
        t_shape = jax.ShapeDtypeStruct((n_tiles, MIX_W, tm), BF16)
        t_spec = pl.BlockSpec((None, MIX_W, tm), lambda i: (i, 0, 0))
        kt_shape = jax.ShapeDtypeStruct((n // cos.shape[0], MIX_W, cos.shape[0]), F32)
        kt_spec = pl.BlockSpec((None, MIX_W, tm), lambda i: (i // pos_tiles, 0, i % pos_tiles))
        v_shape = jax.ShapeDtypeStruct((n * N_HEADS_A, HEAD_W), F32)
        v_spec = pl.BlockSpec((tm * N_HEADS_A, HEAD_W), row)
        out_shape = [kt_shape, v_shape, t_shape, jax.ShapeDtypeStruct((n, MIX_W), BF16), t_shape,
                     f32_out, f32_out, f32_out, f32_out]
        out_specs = [kt_spec, v_spec, t_spec, row_spec, t_spec,
                     row_spec, row_spec, row_spec, row_spec]
    else:
        out_shape = [f32_out] * 7
        out_specs = [row_spec] * 7
    return pl.pallas_call(
        functools.partial(_mixer_kernel, layer, transposed),
        grid=(n_tiles,),
        in_specs=[
            pl.BlockSpec((tm, d), row),
            pl.BlockSpec(w_mix.shape, lambda i: (0, 0)),
            pl.BlockSpec((tm, LANES), lambda i: (i % pos_tiles, 0)),
            pl.BlockSpec((tm, LANES), lambda i: (i % pos_tiles, 0)),
            pl.BlockSpec(lb_logits.shape, lambda i: (0, 0)),
        ],
        out_specs=out_specs,
        out_shape=out_shape,
        compiler_params=pltpu.CompilerParams(
            dimension_semantics=("parallel",), vmem_limit_bytes=VMEM_LIMIT),
        name="mixer_proj",
    )(x, w_mix, cos, sin, lb_logits)


def _prompt_attn_kernel(lam_init, tq, qt_ref, k_ref, vt_ref, lam_ref, g_ref, o_ref,
                        m_ref, acc_ref, sa_ref, sb_ref, qm_ref, fin_ref):
    nq = qt_ref.shape[0]
    ones = jnp.ones((SUM_ROWS, tq), BF16)
    lam = _lambda(lam_ref, lam_init)

    def query_maps(t):
        qt = qt_ref[t]
        qrow = lax.broadcasted_iota(jnp.int32, qt.shape, 0)
        zero = jnp.zeros_like(qt)
        return jnp.where(qrow < HEAD_DIM_A, qt, zero), jnp.where(qrow >= HEAD_DIM_A, qt, zero)

    def first_scores(t, s_ref):
        kj = k_ref[0:tq, :]
        for mp, q in enumerate(query_maps(t)):
            s_ref[mp] = _dot(kj, q)

    def scores(j, s_ref):
        kj = k_ref[pl.ds(pl.multiple_of(j * tq, tq), tq), :]
        for mp in range(2):
            s_ref[mp] = _dot(kj, qm_ref[mp])

    def consume(j, s_ref, masked):
        vj = jnp.concatenate([vt_ref[j], ones], axis=0)
        for mp in range(2):
            s = s_ref[mp]
            if masked:
                krow = lax.broadcasted_iota(jnp.int32, s.shape, 0)
                qcol = lax.broadcasted_iota(jnp.int32, s.shape, 1)
                s = jnp.where(krow <= qcol, s, NEG_BIG)
            m_old = m_ref[mp]
            m_new = jnp.maximum(m_old, jnp.max(s, axis=0, keepdims=True))
            alpha = jnp.exp2(m_old - m_new)
            p = jnp.exp2(s - m_new).astype(BF16)
            acc_ref[mp] = alpha * acc_ref[mp] + _dot(vj, p)
            m_ref[mp] = m_new

    def pair(j):
        scores(j + 1, sb_ref)
        consume(j, sa_ref, False)
        scores(j + 2, sa_ref)
        consume(j + 1, sb_ref, False)

    def trips(n_pairs, first, count):
        def body(jj, carry):
            for i in range(n_pairs):
                pair(first + 2 * (n_pairs * jj + i))
            return carry
        lax.fori_loop(0, count, body, 0)

    def tile(qi, carry):
        for mp, q in enumerate(query_maps(qi)):
            qm_ref[mp] = q
        m_ref[...] = jnp.full(m_ref.shape, NEG_BIG, F32)
        acc_ref[...] = jnp.zeros(acc_ref.shape, F32)
        nxt = jnp.minimum(qi + 1, nq - 1)

        trips(4, 0, qi // 8)
        trips(2, 8 * (qi // 8), (qi % 8) // 4)
        trips(1, 4 * (qi // 4), (qi % 4) // 2)

        prev = jnp.maximum(qi - 1, 0)

        @pl.when(qi % 2 == 0)
        def _():
            first_scores(nxt, sb_ref)
            emit(prev)
            consume(qi, sa_ref, True)
            sa_ref[...] = sb_ref[...]

        @pl.when(qi % 2 == 1)
        def _():
            scores(qi, sb_ref)
            emit(prev)
            consume(qi - 1, sa_ref, False)
            first_scores(nxt, sa_ref)
            consume(qi, sb_ref, True)

        fin_ref[...] = acc_ref[...]
        return carry

    def emit(t):
        acc1 = fin_ref[0]
        acc2 = fin_ref[1]
        o = (acc1[:HEAD_W] / acc1[HEAD_W:HEAD_W + 1]
             - lam * (acc2[:HEAD_W] / acc2[HEAD_W:HEAD_W + 1]))
        ms = jnp.mean(o * o, axis=0, keepdims=True)
        o = o * lax.rsqrt(ms + RMS_EPS) * g_ref[...] * (1.0 - lam_init)
        row0 = t * tq if isinstance(t, int) else pl.multiple_of(t * tq, tq)
        o_ref[pl.ds(row0, tq), :] = o.T.astype(o_ref.dtype)

    fin_ref[...] = jnp.ones(fin_ref.shape, F32)
    first_scores(0, sa_ref)
    lax.fori_loop(0, nq, tile, 0)
    emit(nq - 1)


def _prompt_attention(qt3, k_bf, vt3, lam_vecs, g_col, lam_init, batch, seq, tq):
    nq = seq // tq
    n = batch * seq
    return pl.pallas_call(
        functools.partial(_prompt_attn_kernel, lam_init, tq),
        grid=(batch, N_HEADS_A),
        in_specs=[
            pl.BlockSpec((nq, HEAD_W, tq), lambda b, h: (b, h, 0)),
            pl.BlockSpec((seq, HEAD_W), lambda b, h: (b, h)),
            pl.BlockSpec((nq, HEAD_W, tq), lambda b, h: (b, h, 0)),
            pl.BlockSpec(lam_vecs.shape, lambda b, h: (0, 0)),
            pl.BlockSpec(g_col.shape, lambda b, h: (0, 0)),
        ],
        out_specs=pl.BlockSpec((seq, HEAD_W), lambda b, h: (b, h)),
        out_shape=jax.ShapeDtypeStruct((n, MIX_W), BF16),
        scratch_shapes=[
            pltpu.VMEM((2, 1, tq), F32),
            pltpu.VMEM((2, HEAD_W + SUM_ROWS, tq), F32),
            pltpu.VMEM((2, tq, tq), F32),
            pltpu.VMEM((2, tq, tq), F32),
            pltpu.VMEM((2, HEAD_W, tq), BF16),
            pltpu.VMEM((2, HEAD_W + SUM_ROWS, tq), F32),
        ],
        compiler_params=pltpu.CompilerParams(
            dimension_semantics=("parallel", "parallel"),
            vmem_limit_bytes=VMEM_LIMIT),
        name="prompt_diff_attention",
    )(qt3, k_bf, vt3, lam_vecs, g_col)


class _DecodeState:
    def __init__(self, q_ref, m_ref, l_ref, acc_ref, dec_seq):
        self.m_ref, self.l_ref, self.acc_ref, self.dec_seq = m_ref, l_ref, acc_ref, dec_seq
        qrep = q_ref[...]
        rrow = lax.broadcasted_iota(jnp.int32, qrep.shape, 0)
        rcol = lax.broadcasted_iota(jnp.int32, qrep.shape, 1)
        self.qbd = jnp.where(rcol // HEAD_DIM_A == rrow // dec_seq, qrep, 0.0).astype(BF16)

    def init(self):
        self.m_ref[...] = jnp.full(self.m_ref.shape, NEG_BIG, F32)
        self.l_ref[...] = jnp.zeros(self.l_ref.shape, F32)
        self.acc_ref[...] = jnp.zeros(self.acc_ref.shape, F32)

    def update(self, s, v_bf):
        m_old = self.m_ref[...]
        m_new = jnp.maximum(m_old, jnp.max(s, axis=-1, keepdims=True))
        alpha = jnp.exp2(m_old - m_new)
        p = jnp.exp2(s - m_new)
        self.l_ref[...] = alpha * self.l_ref[...] + jnp.sum(p, axis=-1, keepdims=True)
        self.acc_ref[...] = alpha * self.acc_ref[...] + _dot(p.astype(BF16), v_bf)
        self.m_ref[...] = m_new

    def pages(self, k_refs, v_refs):
        page = k_refs[0].shape[1]
        kt = jnp.concatenate([r[...] for r in k_refs], axis=1).astype(BF16)
        vp = jnp.concatenate(
            [jnp.concatenate([r[pl.ds(h, page, stride=N_HEADS_A), :] for h in range(N_HEADS_A)],
                             axis=1) for r in v_refs], axis=0).astype(BF16)
        self.update(_dot(self.qbd, kt), vp)

    def finish(self, kn_ref, vn_ref, lam_ref, g_ref, o_ref, lam_init):
        dec_seq = self.dec_seq
        kn = kn_ref[...].astype(BF16)
        vn = vn_ref[...].astype(BF16)
        s = _dot_nt(self.qbd, kn)
        key = lax.broadcasted_iota(jnp.int32, s.shape, 1)
        tok = lax.broadcasted_iota(jnp.int32, s.shape, 0) % dec_seq
        self.update(jnp.where(key <= tok, s, NEG_BIG), vn)
        lam = _lambda(lam_ref, lam_init)
        acc = self.acc_ref[...] / self.l_ref[...]
        rows_per_head = 2 * dec_seq
        outs = []
        for h in range(N_HEADS_A):
            blk = acc[h * rows_per_head:(h + 1) * rows_per_head, h * HEAD_W:(h + 1) * HEAD_W]
            o = blk[:dec_seq] - lam * blk[dec_seq:]
            ms = jnp.mean(o * o, axis=-1, keepdims=True)
            outs.append(o * lax.rsqrt(ms + RMS_EPS) * g_ref[...] * (1.0 - lam_init))
        o_ref[...] = jnp.concatenate(outs, axis=1).astype(o_ref.dtype)


def _hgrn_steps(step, q_ref, k_ref, lf_ref, v_ref, o_ref, st_ref):
    ones = jnp.ones((KEY_DIM_B, LANES), BF16)
    rowi = lax.broadcasted_iota(jnp.int32, (step, KEY_DIM_B), 0)

    def advance(r0):
        for h in range(N_HEADS_B):
            cols = slice(h * KEY_DIM_B, (h + 1) * KEY_DIM_B)
            q = q_ref[pl.ds(r0, step), cols]
            lk = k_ref[pl.ds(r0, step), cols]
            v = v_ref[pl.ds(r0, step), cols]
            b = lf_ref[pl.ds(r0, step), cols]
            d = 1
            while d < step:
                b = b + jnp.where(rowi >= d, pltpu.roll(b, d, 0), 0.0)
                d *= 2
            st = st_ref[h]
            o = _dot_nt((q * jnp.exp2(b)).astype(BF16), st.astype(BF16))
            c = b - lk
            xs = []
            for s in range(step):
                lo = (s // 8) * 8
                e = jnp.exp2(jnp.where(rowi[lo:] >= s, b[lo:] - c[s:s + 1], NEG_BIG))
                xs.append(q[lo:] * e)
            att = _dot(jnp.concatenate(xs, axis=0).astype(BF16), ones)
            parts = [o[lo:lo + 8] for lo in range(0, step, 8)]
            off = 0
            for s in range(step):
                lo = (s // 8) * 8
                for g in range(lo // 8, step // 8):
                    parts[g] = parts[g] + att[off:off + 8] * v[s:s + 1]
                    off += 8
            o_ref[pl.ds(r0, step), cols] = jnp.concatenate(parts, axis=0)
            b_last = b[step - 1:step]
            kd = jnp.exp2(b_last - c)
            st_ref[h] = st * jnp.exp2(b_last) + _dot_tn(v.astype(BF16), kd.astype(BF16))

    return advance


def _hgrn_state_io(s0_ref, sout_ref, st_ref):
    ti = pl.program_id(1)

    def load():
        @pl.when(ti == 0)
        def _():
            for h in range(N_HEADS_B):
                st_ref[h] = s0_ref[h].T

    def store():
        @pl.when(ti == pl.num_programs(1) - 1)
        def _():
            for h in range(N_HEADS_B):
                sout_ref[h] = st_ref[h].T

    return load, store


def _hgrn_kernel(step, n_steps, group, q_ref, k_ref, lf_ref, v_ref, s0_ref, o_ref, sout_ref,
                 st_ref):
    advance = _hgrn_steps(step, q_ref, k_ref, lf_ref, v_ref, o_ref, st_ref)
    for g in range(group):
        load, store = _hgrn_state_io(s0_ref.at[g], sout_ref.at[g], st_ref)
        load()

        def body(i, carry, first=g * n_steps * step):
            advance(pl.multiple_of(first + i * step, step))
            return carry

        lax.fori_loop(0, n_steps, body, 0, unroll=math.gcd(n_steps, HGRN_UNROLL))
        store()


def _hgrn_scan(q, k, lf, v, s0, n_seq, seq_rows, tile_rows, step, group=1):
    n = q.shape[0]
    tiles = seq_rows // tile_rows
    assert group == 1 or (tiles == 1 and n_seq % group == 0)
    row = lambda s, t: (s * tiles + t, 0)
    spec = pl.BlockSpec((group * tile_rows, MIX_W), row)
    st_spec = pl.BlockSpec((group, N_HEADS_B, KEY_DIM_B, KEY_DIM_B), lambda s, t: (s, 0, 0, 0))
    return pl.pallas_call(
        functools.partial(_hgrn_kernel, step, tile_rows // step, group),
        grid=(n_seq // group, tiles),
        in_specs=[spec, spec, spec, spec, st_spec],
        out_specs=[spec, st_spec],
        out_shape=[jax.ShapeDtypeStruct((n, MIX_W), F32),
                   jax.ShapeDtypeStruct(s0.shape, F32)],
        scratch_shapes=[pltpu.VMEM((N_HEADS_B, KEY_DIM_B, KEY_DIM_B), F32)],
        compiler_params=pltpu.CompilerParams(
            dimension_semantics=("parallel", "arbitrary"), vmem_limit_bytes=VMEM_LIMIT),
        name="hgrn2_scan",
    )(q, k, lf, v, s0)


def _hgrn_decode_kernel(lam_init, layer, step, n_steps, n_pages, steps_per_sample, dec_seq, pt_ref,
                        q_ref, k_ref, lf_ref, v_ref, s0_ref,
                        qs_ref, kn_ref, vn_ref, lam_ref, g_ref, ck_hbm, cv_hbm,
                        o_ref, sout_ref, oa_ref,
                        st_ref, m_ref, l_ref, acc_ref, kbuf, vbuf, sems):
    lin = pl.program_id(0) * pl.num_programs(1) + pl.program_id(1)
    last = pl.num_programs(0) * pl.num_programs(1) - 1
    phase = pl.program_id(1) % steps_per_sample
    slot = lin % 2

    per_wave = n_pages // PAGE_WAVES

    def page_copies(at_step, at_slot):
        waves = [[] for _ in range(PAGE_WAVES)]
        for pg in range(n_pages):
            page = pt_ref[at_step, pg]
            w = pg // per_wave
            waves[w].append(pltpu.make_async_copy(ck_hbm.at[page, layer], kbuf.at[at_slot, pg],
                                                  sems.at[at_slot, w, 0]))
            waves[w].append(pltpu.make_async_copy(cv_hbm.at[page, layer], vbuf.at[at_slot, pg],
                                                  sems.at[at_slot, w, 1]))
        return waves

    def start_all(waves):
        for wave in waves:
            for c in wave:
                c.start()

    @pl.when(lin == 0)
    def _():
        start_all(page_copies(0, 0))

    @pl.when(lin < last)
    def _():
        start_all(page_copies(lin + 1, 1 - slot))

    load, store = _hgrn_state_io(s0_ref, sout_ref, st_ref)
    advance = _hgrn_steps(step, q_ref, k_ref, lf_ref, v_ref, o_ref, st_ref)
    dec = _DecodeState(qs_ref, m_ref, l_ref, acc_ref, dec_seq)
    load()
    pl.when(phase == 0)(dec.init)

    waves = page_copies(lin, slot)
    k_pages = [kbuf.at[slot, pg] for pg in range(n_pages)]
    v_pages = [vbuf.at[slot, pg] for pg in range(n_pages)]

    n_groups = math.gcd(math.gcd(n_steps, n_pages), DECODE_GROUPS)
    for i in range(n_groups):
        lo, hi = i * n_pages // n_groups, (i + 1) * n_pages // n_groups
        if lo % per_wave == 0:
            for c in waves[lo // per_wave]:
                c.wait()
        dec.pages(k_pages[lo:hi], v_pages[lo:hi])
        for r in range(i * n_steps // n_groups, (i + 1) * n_steps // n_groups):
            advance(r * step)

    store()

    @pl.when(phase == steps_per_sample - 1)
    def _():
        dec.finish(kn_ref, vn_ref, lam_ref, g_ref, oa_ref, lam_init)


def _hgrn_scan_with_decode(q, k, lf, v, s0, n_seq, seq_rows, step,
                           page_table, q_rep, k_new, v_new, cache_kt, cache_v, lam_vecs, g_row,
                           lam_init, layer, dec_seq, pages_per_step):
    n = q.shape[0]
    bs, n_pages = page_table.shape
    page = cache_kt.shape[3]
    n_rows = q_rep.shape[1]
    steps_per_sample = n_pages // pages_per_step
    total_steps = bs * steps_per_sample
    tiles = total_steps // n_seq
    tile_rows = seq_rows // tiles
    assert tiles * n_seq == total_steps and tile_rows * tiles == seq_rows
    assert tile_rows % step == 0 and tiles % steps_per_sample == 0

    row = lambda s, t, pt: (s * tiles + t, 0)
    spec = pl.BlockSpec((tile_rows, MIX_W), row)
    st_spec = pl.BlockSpec((None, N_HEADS_B, KEY_DIM_B, KEY_DIM_B), lambda s, t, pt: (s, 0, 0, 0))
    sample = lambda s, t, pt: ((s * tiles + t) // steps_per_sample, 0, 0)
    const = lambda s, t, pt: (0, 0)

    hbm = pl.BlockSpec(memory_space=pl.ANY)
    grid_spec = pltpu.PrefetchScalarGridSpec(
        num_scalar_prefetch=1,
        grid=(n_seq, tiles),
        in_specs=[spec, spec, spec, spec, st_spec,
                  pl.BlockSpec((None, n_rows, MIX_W), sample),
                  pl.BlockSpec((None, k_new.shape[1], MIX_W), sample),
                  pl.BlockSpec((None, k_new.shape[1], MIX_W), sample),
                  pl.BlockSpec(lam_vecs.shape, const),
                  pl.BlockSpec(g_row.shape, const), hbm, hbm],
        out_specs=[spec, st_spec, pl.BlockSpec((None, dec_seq, MIX_W), sample)],
        scratch_shapes=[
            pltpu.VMEM((N_HEADS_B, KEY_DIM_B, KEY_DIM_B), F32),
            pltpu.VMEM((n_rows, 1), F32),
            pltpu.VMEM((n_rows, 1), F32),
            pltpu.VMEM((n_rows, MIX_W), F32),
            pltpu.VMEM((2, pages_per_step, MIX_W, page), F32),
            pltpu.VMEM((2, pages_per_step, page * N_HEADS_A, HEAD_W), F32),
            pltpu.SemaphoreType.DMA((2, PAGE_WAVES, 2)),
        ],
    )
    return pl.pallas_call(
        functools.partial(_hgrn_decode_kernel, lam_init, layer, step, tile_rows // step,
                          pages_per_step, steps_per_sample, dec_seq),
        grid_spec=grid_spec,
        out_shape=[jax.ShapeDtypeStruct((n, MIX_W), F32),
                   jax.ShapeDtypeStruct(s0.shape, F32),
                   jax.ShapeDtypeStruct((bs, dec_seq, MIX_W), BF16)],
        compiler_params=pltpu.CompilerParams(
            dimension_semantics=("arbitrary", "arbitrary"), vmem_limit_bytes=VMEM_LIMIT),
        name="hgrn2_scan_with_decode_attention",
    )(page_table.reshape(total_steps, pages_per_step), q, k, lf, v, s0, q_rep, k_new, v_new,
      lam_vecs, g_row, cache_kt, cache_v)


def _finish_kernel(alpha, d_ff, seq_rows, x_ref, oa_ref, ob_ref, c0_ref, c1_ref,
                   wg_ref, ng_ref, wa_ref, wb_ref, wo_ref, l1g_ref, l1b_ref, wu_ref,
                   cw_ref, cb_ref, wd_ref, l2g_ref, l2b_ref, out_ref, conv_ref, *scratch):
    tm, d_model = x_ref.shape
    n_grp = FINISH_ROW_GROUPS if tm % (FINISH_ROW_GROUPS * LANES) == 0 else 1
    grp = [slice(i * (tm // n_grp), (i + 1) * (tm // n_grp)) for i in range(n_grp)]
    xs = [x_ref[r, :] for r in grp]
    gates = [_dot(x.astype(BF16), wg_ref[...]) for x in xs]

    def hgrn_out(r, gt):
        ob = ob_ref[r, :]
        gb = gt[:, :MIX_W]
        normed = []
        for h in range(N_HEADS_B):
            oh = ob[:, h * KEY_DIM_B:(h + 1) * KEY_DIM_B]
            ms = jnp.mean(oh * oh, axis=-1, keepdims=True)
            normed.append(oh * lax.rsqrt(ms + RMS_EPS) * ng_ref[...])
        return (jnp.concatenate(normed, axis=1) * (gb * jax.nn.sigmoid(gb))).astype(BF16)

    obn = [hgrn_out(r, gt) for r, gt in zip(grp, gates)]
    br_a = [_dot(oa_ref[r, :], wa_ref[...]) for r in grp]
    br_b = [_dot(o, wb_ref[...]) for o in obn]
    merged = [(jax.nn.sigmoid(gt[:, MIX_W:MIX_W + d_model]) * a
               + jax.nn.sigmoid(gt[:, MIX_W + d_model:]) * b).astype(BF16)
              for gt, a, b in zip(gates, br_a, br_b)]
    res = [_dot(m, wo_ref[...]) for m in merged]
    h1 = jnp.concatenate([_layer_norm(alpha * x + r, l1g_ref[...], l1b_ref[...])
                          for x, r in zip(xs, res)], axis=0)

    hb = h1.astype(BF16)
    chunks = [(c, min(c + FFN_CHUNK, d_ff)) for c in range(0, d_ff, FFN_CHUNK)]

    def up(lo, hi):
        return _dot(hb, wu_ref[:, lo:hi]), _dot(hb, wu_ref[:, d_ff + lo:d_ff + hi])

    if seq_rows is None:
        (a_ref,) = scratch

        @pl.when(pl.program_id(1) == 0)
        def _():
            a_ref[6:8, :] = c0_ref[...]

    y = None
    nxt = up(*chunks[0])
    for i, (lo, hi) in enumerate(chunks):
        a, g = nxt
        if i + 1 < len(chunks):
            nxt = up(*chunks[i + 1])
        if seq_rows is None:
            a_ref[8:8 + tm, lo:hi] = a
            a1 = a_ref[7:7 + tm, lo:hi]
            a2 = a_ref[6:6 + tm, lo:hi]
            tail = a_ref[6 + tm:8 + tm, lo:hi]
            a_ref[6:8, lo:hi] = tail
            conv_ref[:, lo:hi] = tail
        else:
            pos = lax.broadcasted_iota(jnp.int32, a.shape, 0) % seq_rows
            a1 = jnp.where(pos >= 1, pltpu.roll(a, 1, 0), 0.0) + c1_ref[:, lo:hi]
            a2 = jnp.where(pos >= 2, pltpu.roll(a, 2, 0), 0.0) + c0_ref[:, lo:hi]
            conv_ref[:, lo:hi] = a
        c = (cb_ref[:, lo:hi] + a2 * cw_ref[0:1, lo:hi] + a1 * cw_ref[1:2, lo:hi]
             + a * cw_ref[2:3, lo:hi])
        act = 0.5 * c * (1.0 + lax.erf(c * (2.0 ** -0.5))) * g
        yi = _dot(act.astype(BF16), wd_ref[lo:hi, :])
        y = yi if y is None else y + yi
    out_ref[...] = _layer_norm(alpha * h1 + y, l2g_ref[...], l2b_ref[...])


def _finish(x, oa, ob, c0, c1, weights, alpha, n_seq, seq_tiles, tm, seq_rows):
    n, d_model = x.shape
    d_ff = weights[-3].shape[0]
    row = lambda s, t: (s * seq_tiles + t, 0)
    const = lambda s, t: (0, 0)
    once = pl.Buffered(1)

    def wspec(w):
        return pl.BlockSpec(w.shape, const, pipeline_mode=once)

    if seq_rows is None:
        c_specs = [pl.BlockSpec((None, CONV_W - 1, d_ff), lambda s, t: (s, 0, 0))] * 2
        conv_shape = jax.ShapeDtypeStruct((n_seq, CONV_W - 1, d_ff), F32)
        conv_spec = pl.BlockSpec((None, CONV_W - 1, d_ff), lambda s, t: (s, 0, 0))
        scratch = [pltpu.VMEM((tm + 8, d_ff), F32)]
    else:
        c_specs = [pl.BlockSpec((tm, d_ff), row)] * 2
        conv_shape = jax.ShapeDtypeStruct((n, d_ff), F32)
        conv_spec = pl.BlockSpec((tm, d_ff), row)
        scratch = []
    return pl.pallas_call(
        functools.partial(_finish_kernel, alpha, d_ff, seq_rows),
        grid=(n_seq, seq_tiles),
        in_specs=[pl.BlockSpec((tm, d_model), row),
                  pl.BlockSpec((tm, MIX_W), row),
                  pl.BlockSpec((tm, MIX_W), row)] + c_specs + [wspec(w) for w in weights],
        out_specs=[pl.BlockSpec((tm, d_model), row), conv_spec],
        out_shape=[jax.ShapeDtypeStruct((n, d_model), F32), conv_shape],
        scratch_shapes=scratch,
        compiler_params=pltpu.CompilerParams(
            dimension_semantics=("parallel", "arbitrary"), vmem_limit_bytes=VMEM_LIMIT),
        name="finish_layer",
    )(x, oa, ob, c0, c1, *weights)


MIXER_TILE = 512
HGRN_STEP = 16
HGRN_UNROLL = 8
SAMPLE_SEQS_PER_STEP = 8
FINISH_TILE = 512
FINISH_ROW_GROUPS = 2
FFN_CHUNK = 512
PAGES_PER_STEP = 16
PAGE_WAVES = 2
DECODE_GROUPS = 8
NEW_KEY_PAD = 16


def _rope_tables(pos):
    half = HEAD_DIM_A // 2
    inv = ROPE_THETA ** (-np.arange(half, dtype=np.float64) * 2.0 / HEAD_DIM_A)
    ang = np.asarray(pos, np.float64)[:, None] * inv[None, :]
    cos = np.cos(ang)
    sin = np.sin(ang)
    reps = LANES // HEAD_DIM_A
    return (jnp.asarray(np.tile(np.concatenate([cos, cos], axis=1), (1, reps)), F32),
            jnp.asarray(np.tile(np.concatenate([-sin, sin], axis=1), (1, reps)), F32))


def kernel(x_prompt, x_sample, cache_k, cache_v, state_hgrn, state_conv, page_table, w_in, lambda_q1, lambda_k1, lambda_q2, lambda_k2, subln_g, lb_logits, hgrn_norm_g, w_branch_a, w_branch_b, w_out, ln1_g, ln1_b, w_up, conv_w, conv_b, w_down, ln2_g, ln2_b):
    bp, tp, d_model = x_prompt.shape
    bs, ts, _ = x_sample.shape
    depth = w_in.shape[0]
    n_phys, _, page, _, _, _ = cache_k.shape
    past_len = page_table.shape[1] * page
    d_ff = w_down.shape[1]
    alpha = (2 * depth) ** 0.25
    n_mix = 6 * MIX_W
    ts_pad = 8
    lb_logits = lb_logits.astype(F32)

    cos_p, sin_p = _rope_tables(np.arange(tp))
    cos_s, sin_s = _rope_tables(past_len + np.arange(bs * ts) % ts)
    cache_kt = jnp.transpose(cache_k, (0, 1, 3, 4, 5, 2)).reshape(n_phys, depth, MIX_W, page)
    cache_vr = cache_v.reshape(n_phys, depth, page * N_HEADS_A, HEAD_W)

    xp = x_prompt.reshape(bp * tp, d_model)
    xs = x_sample.reshape(bs * ts, d_model)
    outs = {name: [] for name in ("kp", "vp", "sp", "cp", "ks", "vs", "ss", "cs")}
    for l in range(depth):
        lam_init = 0.8 - 0.6 * math.exp(-0.3 * l)
        w_mix = w_in[l, :, :n_mix].astype(BF16)
        lam_vecs = jnp.stack([lambda_q1[l], lambda_k1[l], lambda_q2[l], lambda_k2[l]]).astype(F32)
        g_row = subln_g[l].reshape(1, HEAD_W).astype(F32)
        weights = (
            w_in[l, :, n_mix:].astype(BF16),
            hgrn_norm_g[l].reshape(1, KEY_DIM_B).astype(F32),
            w_branch_a[l].astype(BF16), w_branch_b[l].astype(BF16), w_out[l].astype(BF16),
            ln1_g[l].reshape(1, d_model), ln1_b[l].reshape(1, d_model),
            w_up[l].astype(BF16), conv_w[l], conv_b[l].reshape(1, d_ff),
            w_down[l].astype(BF16), ln2_g[l].reshape(1, d_model), ln2_b[l].reshape(1, d_model))

        k_p, v_p, qt3, k_bf, vt3, qb, kk, lf, ib = _mixer_proj(
            xp, w_mix, cos_p, sin_p, lb_logits, l, True, MIXER_TILE)
        k_s, v_s, q_s, qb_s, kk_s, lf_s, ib_s = _mixer_proj(
            xs, w_mix, cos_s, sin_s, lb_logits, l, False, bs * ts)
        pad_t = lambda a, rows: jnp.pad(a.reshape(bs, ts, MIX_W), ((0, 0), (0, rows - ts), (0, 0)))
        q_rep = jnp.tile(q_s.reshape(bs, ts, MIX_W), (1, 2 * N_HEADS_A, 1))

        oa = _prompt_attention(qt3, k_bf, vt3, lam_vecs, g_row.reshape(HEAD_W, 1), lam_init,
                               bp, tp, MIXER_TILE)
        s0 = jnp.zeros((bp, N_HEADS_B, KEY_DIM_B, KEY_DIM_B), F32)
        ob, s_p, oa_s = _hgrn_scan_with_decode(
            qb, kk, lf, ib, s0, bp, tp, HGRN_STEP,
            page_table, q_rep, pad_t(k_s, NEW_KEY_PAD), pad_t(v_s, NEW_KEY_PAD),
            cache_kt, cache_vr, lam_vecs, g_row, lam_init, l, ts, PAGES_PER_STEP)
        conv0 = jnp.zeros((bp, CONV_W - 1, d_ff), F32)
        xp_new, conv_p = _finish(xp, oa, ob, conv0, conv0, weights, alpha,
                                 bp, tp // FINISH_TILE, FINISH_TILE, None)
        outs["kp"].append(jnp.transpose(k_p.reshape(bp, N_HEADS_A, 2, HEAD_DIM_A, tp), (0, 4, 1, 2, 3)))
        outs["vp"].append(v_p.reshape(bp, tp, N_HEADS_A, HEAD_W))
        outs["sp"].append(s_p)
        outs["cp"].append(conv_p)

        flat = lambda a: pad_t(a, ts_pad).reshape(bs * ts_pad, MIX_W)
        ob, s_s = _hgrn_scan(flat(qb_s), flat(kk_s), flat(lf_s), flat(ib_s), state_hgrn[:, l],
                             bs, ts_pad, ts_pad, ts_pad, math.gcd(bs, SAMPLE_SEQS_PER_STEP))
        ob = ob.reshape(bs, ts_pad, MIX_W)[:, :ts].reshape(bs * ts, MIX_W)
        st = state_conv[:, l]
        c0 = jnp.concatenate([st, jnp.zeros((bs, ts - 2, d_ff), F32)], axis=1)
        c1 = jnp.concatenate([st[:, 1:], jnp.zeros((bs, ts - 1, d_ff), F32)], axis=1)
        xs_new, a_s = _finish(xs, oa_s.reshape(bs * ts, MIX_W), ob,
                              c0.reshape(bs * ts, d_ff), c1.reshape(bs * ts, d_ff),
                              weights, alpha, 1, 1, bs * ts, ts)
        conv_s = jnp.concatenate([st, a_s.reshape(bs, ts, d_ff)], axis=1)[:, -(CONV_W - 1):]
        outs["ks"].append(k_s.reshape(bs, ts, N_HEADS_A, 2, HEAD_DIM_A))
        outs["vs"].append(v_s.reshape(bs, ts, N_HEADS_A, HEAD_W))
        outs["ss"].append(s_s)
        outs["cs"].append(conv_s)
        xp, xs = xp_new, xs_new

    stack = lambda name: jnp.stack(outs[name], axis=1)
    return (xp.reshape(bp, tp, d_model), xs.reshape(bs, ts, d_model),
            stack("kp"), stack("vp"), stack("sp"), stack("cp"),
            stack("ks"), stack("vs"), stack("ss"), stack("cs"))
```

```python
import functools
import math

import jax
import jax.numpy as jnp
import numpy as np
from jax import lax
from jax.experimental import pallas as pl
from jax.experimental.pallas import tpu as pltpu

F32 = jnp.float32
BF16 = jnp.bfloat16

N_HEADS_A = 4
HEAD_DIM_A = 64
N_HEADS_B = 4
KEY_DIM_B = 128
CONV_W = 3
ROPE_THETA = 10000.0
LN_EPS = 1e-5
RMS_EPS = 1e-5

LANES = 128
HEAD_W = 2 * HEAD_DIM_A
MIX_W = N_HEADS_A * HEAD_W
NEG_BIG = -1e30
SUM_ROWS = 16

VMEM_LIMIT = 56 * 1024 * 1024


def _dot(a, b):
    return jnp.dot(a, b, preferred_element_type=F32)


def _dot_nt(a, b):
    return lax.dot_general(a, b, (((1,), (1,)), ((), ())), preferred_element_type=F32)


def _dot_tn(a, b):
    return lax.dot_general(a, b, (((0,), (0,)), ((), ())), preferred_element_type=F32)


def _layer_norm(x, g, b):
    mu = jnp.mean(x, axis=-1, keepdims=True)
    xc = x - mu
    var = jnp.mean(xc * xc, axis=-1, keepdims=True)
    return xc * lax.rsqrt(var + LN_EPS) * g + b


def _lambda(lam_ref, lam_init):
    lv = lam_ref[...]
    e1 = jnp.exp(jnp.sum(lv[0:1] * lv[1:2], axis=-1, keepdims=True))
    e2 = jnp.exp(jnp.sum(lv[2:3] * lv[3:4], axis=-1, keepdims=True))
    return e1 - e2 + lam_init


def _rope(x, cos, sin_signed):
    lane = lax.broadcasted_iota(jnp.int32, x.shape, 1)
    first_half = (lane % HEAD_DIM_A) < (HEAD_DIM_A // 2)
    partner = jnp.where(first_half, pltpu.roll(x, LANES - HEAD_DIM_A // 2, 1),
                        pltpu.roll(x, HEAD_DIM_A // 2, 1))
    return x * cos + partner * sin_signed


def _mixer_kernel(layer, transposed, x_ref, w_ref, cos_ref, sin_ref, lbl_ref, *out_refs):
    if transposed:
        k_ref, v_ref, qt_ref, kb_ref, vt_ref, qb_ref, kk_ref, lf_ref, ib_ref = out_refs
    else:
        k_ref, v_ref, q_ref, qb_ref, kk_ref, lf_ref, ib_ref = out_refs
    xb = x_ref[...].astype(BF16)
    cos = cos_ref[...]
    sin = sin_ref[...]

    def section(i):
        return _dot(xb, w_ref[:, i * MIX_W:(i + 1) * MIX_W])

    scale = HEAD_DIM_A ** -0.5 * math.log2(math.e)
    qa = section(0)
    qa = jnp.concatenate([_rope(qa[:, h * HEAD_W:(h + 1) * HEAD_W], cos, sin) * scale
                          for h in range(N_HEADS_A)], axis=1)
    ka = section(1)
    ka = jnp.concatenate([_rope(ka[:, h * HEAD_W:(h + 1) * HEAD_W], cos, sin)
                          for h in range(N_HEADS_A)], axis=1)
    va = section(2)
    if transposed:
        k_ref[...] = ka.T
        for h in range(N_HEADS_A):
            v_ref[pl.ds(h, va.shape[0], stride=N_HEADS_A), :] = va[:, h * HEAD_W:(h + 1) * HEAD_W]
        qt_ref[...] = qa.T.astype(BF16)
        kb_ref[...] = ka.astype(BF16)
        vt_ref[...] = va.T.astype(BF16)
    else:
        k_ref[...] = ka
        v_ref[...] = va
        q_ref[...] = qa

    qb_ref[...] = section(3)
    fr = section(4)
    ib_ref[...] = section(5)

    lg = lbl_ref[...]
    e = jnp.exp(lg - jnp.max(lg, axis=0, keepdims=True))
    lb = jnp.sum(e[:layer + 1], axis=0, keepdims=True) / jnp.sum(e, axis=0, keepdims=True)
    f = lb + (1.0 - lb) * (1.0 / (1.0 + jnp.exp(-fr)))
    lf_ref[...] = jnp.log2(f)
    kk_ref[...] = jnp.log2((1.0 - lb) * (1.0 / (1.0 + jnp.exp(fr))))


def _mixer_proj(x, w_mix, cos, sin, lb_logits, layer, transposed, tm):
    n = x.shape[0]
    d = x.shape[1]
    n_tiles = n // tm
    pos_tiles = cos.shape[0] // tm
    row = lambda i: (i, 0)
    f32_out = jax.ShapeDtypeStruct((n, MIX_W), F32)
    row_spec = pl.BlockSpec((tm, MIX_W), row)
    if transposed:
        t_shape = jax.ShapeDtypeStruct((n_tiles, MIX_W, tm), BF16)
        t_spec = pl.BlockSpec((None, MIX_W, tm), lambda i: (i, 0, 0))
        kt_shape = jax.ShapeDtypeStruct((n // cos.shape[0], MIX_W, cos.shape[0]), F32)
        kt_spec = pl.BlockSpec((None, MIX_W, tm), lambda i: (i // pos_tiles, 0, i % pos_tiles))
        v_shape = jax.ShapeDtypeStruct((n * N_HEADS_A, HEAD_W), F32)
        v_spec = pl.BlockSpec((tm * N_HEADS_A, HEAD_W), row)
        out_shape = [kt_shape, v_shape, t_shape, jax.ShapeDtypeStruct((n, MIX_W), BF16), t_shape,
                     f32_out, f32_out, f32_out, f32_out]
        out_specs = [kt_spec, v_spec, t_spec, row_spec, t_spec,
                     row_spec, row_spec, row_spec, row_spec]
    else:
        out_shape = [f32_out] * 7
        out_specs = [row_spec] * 7
    return pl.pallas_call(
        functools.partial(_mixer_kernel, layer, transposed),
        grid=(n_tiles,),
        in_specs=[
            pl.BlockSpec((tm, d), row),
            pl.BlockSpec(w_mix.shape, lambda i: (0, 0)),
            pl.BlockSpec((tm, LANES), lambda i: (i % pos_tiles, 0)),
            pl.BlockSpec((tm, LANES), lambda i: (i % pos_tiles, 0)),
            pl.BlockSpec(lb_logits.shape, lambda i: (0, 0)),
        ],
        out_specs=out_specs,
        out_shape=out_shape,
        compiler_params=pltpu.CompilerParams(
            dimension_semantics=("parallel",), vmem_limit_bytes=VMEM_LIMIT),
        name="mixer_proj",
    )(x, w_mix, cos, sin, lb_logits)


def _prompt_attn_kernel(lam_init, tq, qt_ref, k_ref, vt_ref, lam_ref, g_ref, o_ref,
                        m_ref, acc_ref, sa_ref, sb_ref, qm_ref, fin_ref):
    nq = qt_ref.shape[0]
    ones = jnp.ones((SUM_ROWS, tq), BF16)
    lam = _lambda(lam_ref, lam_init)

    def query_maps(t):
        qt = qt_ref[t]
        qrow = lax.broadcasted_iota(jnp.int32, qt.shape, 0)
        zero = jnp.zeros_like(qt)
        return jnp.where(qrow < HEAD_DIM_A, qt, zero), jnp.where(qrow >= HEAD_DIM_A, qt, zero)

    def first_scores(t, s_ref):
        kj = k_ref[0:tq, :]
        for mp, q in enumerate(query_maps(t)):
            s_ref[mp] = _dot(kj, q)

    def scores(j, s_ref):
        kj = k_ref[pl.ds(pl.multiple_of(j * tq, tq), tq), :]
        for mp in range(2):
            s_ref[mp] = _dot(kj, qm_ref[mp])

    def consume(j, s_ref, masked):
        vj = jnp.concatenate([vt_ref[j], ones], axis=0)
        for mp in range(2):
            s = s_ref[mp]
            if masked:
                krow = lax.broadcasted_iota(jnp.int32, s.shape, 0)
                qcol = lax.broadcasted_iota(jnp.int32, s.shape, 1)
                s = jnp.where(krow <= qcol, s, NEG_BIG)
            m_old = m_ref[mp]
            m_new = jnp.maximum(m_old, jnp.max(s, axis=0, keepdims=True))
            alpha = jnp.exp2(m_old - m_new)
            p = jnp.exp2(s - m_new).astype(BF16)
            acc_ref[mp] = alpha * acc_ref[mp] + _dot(vj, p)
            m_ref[mp] = m_new

    def pair(j):
        scores(j + 1, sb_ref)
        consume(j, sa_ref, False)
        scores(j + 2, sa_ref)
        consume(j + 1, sb_ref, False)

    def trips(n_pairs, first, count):
        def body(jj, carry):
            for i in range(n_pairs):
                pair(first + 2 * (n_pairs * jj + i))
            return carry
        lax.fori_loop(0, count, body, 0)

    def tile(qi, carry):
        for mp, q in enumerate(query_maps(qi)):
            qm_ref[mp] = q
        m_ref[...] = jnp.full(m_ref.shape, NEG_BIG, F32)
        acc_ref[...] = jnp.zeros(acc_ref.shape, F32)
        nxt = jnp.minimum(qi + 1, nq - 1)

        trips(4, 0, qi // 8)
        trips(2, 8 * (qi // 8), (qi % 8) // 4)
        trips(1, 4 * (qi // 4), (qi % 4) // 2)

        prev = jnp.maximum(qi - 1, 0)

        @pl.when(qi % 2 == 0)
        def _():
            first_scores(nxt, sb_ref)
            emit(prev)
            consume(qi, sa_ref, True)
            sa_ref[...] = sb_ref[...]

        @pl.when(qi % 2 == 1)
        def _():
            scores(qi, sb_ref)
            emit(prev)
            consume(qi - 1, sa_ref, False)
            first_scores(nxt, sa_ref)
            consume(qi, sb_ref, True)

        fin_ref[...] = acc_ref[...]
        return carry

    def emit(t):
        acc1 = fin_ref[0]
        acc2 = fin_ref[1]
        o = (acc1[:HEAD_W] / acc1[HEAD_W:HEAD_W + 1]
             - lam * (acc2[:HEAD_W] / acc2[HEAD_W:HEAD_W + 1]))
        ms = jnp.mean(o * o, axis=0, keepdims=True)
        o = o * lax.rsqrt(ms + RMS_EPS) * g_ref[...] * (1.0 - lam_init)
        row0 = t * tq if isinstance(t, int) else pl.multiple_of(t * tq, tq)
        o_ref[pl.ds(row0, tq), :] = o.T.astype(o_ref.dtype)

    fin_ref[...] = jnp.ones(fin_ref.shape, F32)
    first_scores(0, sa_ref)
    lax.fori_loop(0, nq, tile, 0)
    emit(nq - 1)


def _prompt_attention(qt3, k_bf, vt3, lam_vecs, g_col, lam_init, batch, seq, tq):
    nq = seq // tq
    n = batch * seq
    return pl.pallas_call(
        functools.partial(_prompt_attn_kernel, lam_init, tq),
        grid=(batch, N_HEADS_A),
        in_specs=[
            pl.BlockSpec((nq, HEAD_W, tq), lambda b, h: (b, h, 0)),
            pl.BlockSpec((seq, HEAD_W), lambda b, h: (b, h)),
            pl.BlockSpec((nq, HEAD_W, tq), lambda b, h: (b, h, 0)),
            pl.BlockSpec(lam_vecs.shape, lambda b, h: (0, 0)),
            pl.BlockSpec(g_col.shape, lambda b, h: (0, 0)),
        ],
        out_specs=pl.BlockSpec((seq, HEAD_W), lambda b, h: (b, h)),
        out_shape=jax.ShapeDtypeStruct((n, MIX_W), BF16),
        scratch_shapes=[
            pltpu.VMEM((2, 1, tq), F32),
            pltpu.VMEM((2, HEAD_W + SUM_ROWS, tq), F32),
            pltpu.VMEM((2, tq, tq), F32),
            pltpu.VMEM((2, tq, tq), F32),
            pltpu.VMEM((2, HEAD_W, tq), BF16),
            pltpu.VMEM((2, HEAD_W + SUM_ROWS, tq), F32),
        ],
        compiler_params=pltpu.CompilerParams(
            dimension_semantics=("parallel", "parallel"),
            vmem_limit_bytes=VMEM_LIMIT),
        name="prompt_diff_attention",
    )(qt3, k_bf, vt3, lam_vecs, g_col)


class _DecodeState:
    def __init__(self, q_ref, m_ref, l_ref, acc_ref, dec_seq):
        self.m_ref, self.l_ref, self.acc_ref, self.dec_seq = m_ref, l_ref, acc_ref, dec_seq
        qrep = q_ref[...]
        rrow = lax.broadcasted_iota(jnp.int32, qrep.shape, 0)
        rcol = lax.broadcasted_iota(jnp.int32, qrep.shape, 1)
        self.qbd = jnp.where(rcol // HEAD_DIM_A == rrow // dec_seq, qrep, 0.0).astype(BF16)

    def init(self):
        self.m_ref[...] = jnp.full(self.m_ref.shape, NEG_BIG, F32)
        self.l_ref[...] = jnp.zeros(self.l_ref.shape, F32)
        self.acc_ref[...] = jnp.zeros(self.acc_ref.shape, F32)

    def update(self, s, v_bf):
        m_old = self.m_ref[...]
        m_new = jnp.maximum(m_old, jnp.max(s, axis=-1, keepdims=True))
        alpha = jnp.exp2(m_old - m_new)
        p = jnp.exp2(s - m_new)
        self.l_ref[...] = alpha * self.l_ref[...] + jnp.sum(p, axis=-1, keepdims=True)
        self.acc_ref[...] = alpha * self.acc_ref[...] + _dot(p.astype(BF16), v_bf)
        self.m_ref[...] = m_new

    def pages(self, k_refs, v_refs):
        page = k_refs[0].shape[1]
        kt = jnp.concatenate([r[...] for r in k_refs], axis=1).astype(BF16)
        vp = jnp.concatenate(
            [jnp.concatenate([r[pl.ds(h, page, stride=N_HEADS_A), :] for h in range(N_HEADS_A)],
                             axis=1) for r in v_refs], axis=0).astype(BF16)
        self.update(_dot(self.qbd, kt), vp)

    def finish(self, kn_ref, vn_ref, lam_ref, g_ref, o_ref, lam_init):
        dec_seq = self.dec_seq
        kn = kn_ref[...].astype(BF16)
        vn = vn_ref[...].astype(BF16)
        s = _dot_nt(self.qbd, kn)
        key = lax.broadcasted_iota(jnp.int32, s.shape, 1)
        tok = lax.broadcasted_iota(jnp.int32, s.shape, 0) % dec_seq
        self.update(jnp.where(key <= tok, s, NEG_BIG), vn)
        lam = _lambda(lam_ref, lam_init)
        acc = self.acc_ref[...] / self.l_ref[...]
        rows_per_head = 2 * dec_seq
        outs = []
        for h in range(N_HEADS_A):
            blk = acc[h * rows_per_head:(h + 1) * rows_per_head, h * HEAD_W:(h + 1) * HEAD_W]
            o = blk[:dec_seq] - lam * blk[dec_seq:]
            ms = jnp.mean(o * o, axis=-1, keepdims=True)
            outs.append(o * lax.rsqrt(ms + RMS_EPS) * g_ref[...] * (1.0 - lam_init))
        o_ref[...] = jnp.concatenate(outs, axis=1).astype(o_ref.dtype)


def _hgrn_steps(step, q_ref, k_ref, lf_ref, v_ref, o_ref, st_ref):
    ones = jnp.ones((KEY_DIM_B, LANES), BF16)
    rowi = lax.broadcasted_iota(jnp.int32, (step, KEY_DIM_B), 0)

    def advance(r0):
        for h in range(N_HEADS_B):
            cols = slice(h * KEY_DIM_B, (h + 1) * KEY_DIM_B)
            q = q_ref[pl.ds(r0, step), cols]
            lk = k_ref[pl.ds(r0, step), cols]
            v = v_ref[pl.ds(r0, step), cols]
            b = lf_ref[pl.ds(r0, step), cols]
            d = 1
            while d < step:
                b = b + jnp.where(rowi >= d, pltpu.roll(b, d, 0), 0.0)
                d *= 2
            st = st_ref[h]
            o = _dot_nt((q * jnp.exp2(b)).astype(BF16), st.astype(BF16))
            c = b - lk
            xs = []
            for s in range(step):
                lo = (s // 8) * 8
                e = jnp.exp2(jnp.where(rowi[lo:] >= s, b[lo:] - c[s:s + 1], NEG_BIG))
                xs.append(q[lo:] * e)
            att = _dot(jnp.concatenate(xs, axis=0).astype(BF16), ones)
            parts = [o[lo:lo + 8] for lo in range(0, step, 8)]
            off = 0
            for s in range(step):
                lo = (s // 8) * 8
                for g in range(lo // 8, step // 8):
                    parts[g] = parts[g] + att[off:off + 8] * v[s:s + 1]
                    off += 8
            o_ref[pl.ds(r0, step), cols] = jnp.concatenate(parts, axis=0)
            b_last = b[step - 1:step]
            kd = jnp.exp2(b_last - c)
            st_ref[h] = st * jnp.exp2(b_last) + _dot_tn(v.astype(BF16), kd.astype(BF16))

    return advance


def _hgrn_state_io(s0_ref, sout_ref, st_ref):
    ti = pl.program_id(1)

    def load():
        @pl.when(ti == 0)
        def _():
            for h in range(N_HEADS_B):
                st_ref[h] = s0_ref[h].T

    def store():
        @pl.when(ti == pl.num_programs(1) - 1)
        def _():
            for h in range(N_HEADS_B):
                sout_ref[h] = st_ref[h].T

    return load, store


def _hgrn_kernel(step, n_steps, group, q_ref, k_ref, lf_ref, v_ref, s0_ref, o_ref, sout_ref,
                 st_ref):
    advance = _hgrn_steps(step, q_ref, k_ref, lf_ref, v_ref, o_ref, st_ref)
    for g in range(group):
        load, store = _hgrn_state_io(s0_ref.at[g], sout_ref.at[g], st_ref)
        load()

        def body(i, carry, first=g * n_steps * step):
            advance(pl.multiple_of(first + i * step, step))
            return carry

        lax.fori_loop(0, n_steps, body, 0, unroll=math.gcd(n_steps, HGRN_UNROLL))
        store()


def _hgrn_scan(q, k, lf, v, s0, n_seq, seq_rows, tile_rows, step, group=1):
    n = q.shape[0]
    tiles = seq_rows // tile_rows
    assert group == 1 or (tiles == 1 and n_seq % group == 0)
    row = lambda s, t: (s * tiles + t, 0)
    spec = pl.BlockSpec((group * tile_rows, MIX_W), row)
    st_spec = pl.BlockSpec((group, N_HEADS_B, KEY_DIM_B, KEY_DIM_B), lambda s, t: (s, 0, 0, 0))
    return pl.pallas_call(
        functools.partial(_hgrn_kernel, step, tile_rows // step, group),
        grid=(n_seq // group, tiles),
        in_specs=[spec, spec, spec, spec, st_spec],
        out_specs=[spec, st_spec],
        out_shape=[jax.ShapeDtypeStruct((n, MIX_W), F32),
                   jax.ShapeDtypeStruct(s0.shape, F32)],
        scratch_shapes=[pltpu.VMEM((N_HEADS_B, KEY_DIM_B, KEY_DIM_B), F32)],
        compiler_params=pltpu.CompilerParams(
            dimension_semantics=("parallel", "arbitrary"), vmem_limit_bytes=VMEM_LIMIT),
        name="hgrn2_scan",
    )(q, k, lf, v, s0)


def _hgrn_decode_kernel(lam_init, layer, step, n_steps, n_pages, steps_per_sample, dec_seq, pt_ref,
                        q_ref, k_ref, lf_ref, v_ref, s0_ref,
                        qs_ref, kn_ref, vn_ref, lam_ref, g_ref, ck_hbm, cv_hbm,
                        o_ref, sout_ref, oa_ref,
                        st_ref, m_ref, l_ref, acc_ref, kbuf, vbuf, sems):
    lin = pl.program_id(0) * pl.num_programs(1) + pl.program_id(1)
    last = pl.num_programs(0) * pl.num_programs(1) - 1
    phase = pl.program_id(1) % steps_per_sample
    slot = lin % 2

    def page_copies(at_step, at_slot):
        waves = [[] for _ in PAGE_WAVE_STARTS]
        for pg in range(n_pages):
            page = pt_ref[at_step, pg]
            w = sum(pg >= start for start in PAGE_WAVE_STARTS) - 1
            waves[w].append(pltpu.make_async_copy(ck_hbm.at[page, layer], kbuf.at[at_slot, pg],
                                                  sems.at[at_slot, w, 0]))
            waves[w].append(pltpu.make_async_copy(cv_hbm.at[page, layer], vbuf.at[at_slot, pg],
                                                  sems.at[at_slot, w, 1]))
        return waves

    def start_all(waves):
        for wave in waves:
            for c in wave:
                c.start()

    @pl.when(lin == 0)
    def _():
        start_all(page_copies(0, 0))

    @pl.when(lin < last)
    def _():
        start_all(page_copies(lin + 1, 1 - slot))

    load, store = _hgrn_state_io(s0_ref, sout_ref, st_ref)
    advance = _hgrn_steps(step, q_ref, k_ref, lf_ref, v_ref, o_ref, st_ref)
    dec = _DecodeState(qs_ref, m_ref, l_ref, acc_ref, dec_seq)
    load()
    pl.when(phase == 0)(dec.init)

    waves = page_copies(lin, slot)
    k_pages = [kbuf.at[slot, pg] for pg in range(n_pages)]
    v_pages = [vbuf.at[slot, pg] for pg in range(n_pages)]

    n_groups = math.gcd(math.gcd(n_steps, n_pages), DECODE_GROUPS)
    for i in range(n_groups):
        lo, hi = i * n_pages // n_groups, (i + 1) * n_pages // n_groups
        if lo in PAGE_WAVE_STARTS:
            for c in waves[PAGE_WAVE_STARTS.index(lo)]:
                c.wait()
        dec.pages(k_pages[lo:hi], v_pages[lo:hi])
        for r in range(i * n_steps // n_groups, (i + 1) * n_steps // n_groups):
            advance(r * step)

    store()

    @pl.when(phase == steps_per_sample - 1)
    def _():
        dec.finish(kn_ref, vn_ref, lam_ref, g_ref, oa_ref, lam_init)


def _hgrn_scan_with_decode(q, k, lf, v, s0, n_seq, seq_rows, step,
                           page_table, q_rep, k_new, v_new, cache_kt, cache_v, lam_vecs, g_row,
                           lam_init, layer, dec_seq, pages_per_step):
    n = q.shape[0]
    bs, n_pages = page_table.shape
    page = cache_kt.shape[3]
    n_rows = q_rep.shape[1]
    steps_per_sample = n_pages // pages_per_step
    total_steps = bs * steps_per_sample
    tiles = total_steps // n_seq
    tile_rows = seq_rows // tiles
    assert tiles * n_seq == total_steps and tile_rows * tiles == seq_rows
    assert tile_rows % step == 0 and tiles % steps_per_sample == 0

    row = lambda s, t, pt: (s * tiles + t, 0)
    spec = pl.BlockSpec((tile_rows, MIX_W), row)
    st_spec = pl.BlockSpec((None, N_HEADS_B, KEY_DIM_B, KEY_DIM_B), lambda s, t, pt: (s, 0, 0, 0))
    sample = lambda s, t, pt: ((s * tiles + t) // steps_per_sample, 0, 0)
    const = lambda s, t, pt: (0, 0)

    hbm = pl.BlockSpec(memory_space=pl.ANY)
    grid_spec = pltpu.PrefetchScalarGridSpec(
        num_scalar_prefetch=1,
        grid=(n_seq, tiles),
        in_specs=[spec, spec, spec, spec, st_spec,
                  pl.BlockSpec((None, n_rows, MIX_W), sample),
                  pl.BlockSpec((None, k_new.shape[1], MIX_W), sample),
                  pl.BlockSpec((None, k_new.shape[1], MIX_W), sample),
                  pl.BlockSpec(lam_vecs.shape, const),
                  pl.BlockSpec(g_row.shape, const), hbm, hbm],
        out_specs=[spec, st_spec, pl.BlockSpec((None, dec_seq, MIX_W), sample)],
        scratch_shapes=[
            pltpu.VMEM((N_HEADS_B, KEY_DIM_B, KEY_DIM_B), F32),
            pltpu.VMEM((n_rows, 1), F32),
            pltpu.VMEM((n_rows, 1), F32),
            pltpu.VMEM((n_rows, MIX_W), F32),
            pltpu.VMEM((2, pages_per_step, MIX_W, page), F32),
            pltpu.VMEM((2, pages_per_step, page * N_HEADS_A, HEAD_W), F32),
            pltpu.SemaphoreType.DMA((2, len(PAGE_WAVE_STARTS), 2)),
        ],
    )
    return pl.pallas_call(
        functools.partial(_hgrn_decode_kernel, lam_init, layer, step, tile_rows // step,
                          pages_per_step, steps_per_sample, dec_seq),
        grid_spec=grid_spec,
        out_shape=[jax.ShapeDtypeStruct((n, MIX_W), F32),
                   jax.ShapeDtypeStruct(s0.shape, F32),
                   jax.ShapeDtypeStruct((bs, dec_seq, MIX_W), BF16)],
        compiler_params=pltpu.CompilerParams(
            dimension_semantics=("arbitrary", "arbitrary"), vmem_limit_bytes=VMEM_LIMIT),
        name="hgrn2_scan_with_decode_attention",
    )(page_table.reshape(total_steps, pages_per_step), q, k, lf, v, s0, q_rep, k_new, v_new,
      lam_vecs, g_row, cache_kt, cache_v)


def _finish_kernel(alpha, d_ff, seq_rows, x_ref, oa_ref, ob_ref, c0_ref, c1_ref,
                   wg_ref, ng_ref, wa_ref, wb_ref, wo_ref, l1g_ref, l1b_ref, wu_ref,
                   cw_ref, cb_ref, wd_ref, l2g_ref, l2b_ref, out_ref, conv_ref, *scratch):
    tm, d_model = x_ref.shape
    n_grp = FINISH_ROW_GROUPS if tm % (FINISH_ROW_GROUPS * LANES) == 0 else 1
    grp = [slice(i * (tm // n_grp), (i + 1) * (tm // n_grp)) for i in range(n_grp)]
    xs = [x_ref[r, :] for r in grp]
    gates = [_dot(x.astype(BF16), wg_ref[...]) for x in xs]

    def hgrn_out(r, gt):
        ob = ob_ref[r, :]
        gb = gt[:, :MIX_W]
        normed = []
        for h in range(N_HEADS_B):
            oh = ob[:, h * KEY_DIM_B:(h + 1) * KEY_DIM_B]
            ms = jnp.mean(oh * oh, axis=-1, keepdims=True)
            normed.append(oh * lax.rsqrt(ms + RMS_EPS) * ng_ref[...])
        return (jnp.concatenate(normed, axis=1) * (gb * jax.nn.sigmoid(gb))).astype(BF16)

    obn = [hgrn_out(r, gt) for r, gt in zip(grp, gates)]
    br_a = [_dot(oa_ref[r, :], wa_ref[...]) for r in grp]
    br_b = [_dot(o, wb_ref[...]) for o in obn]
    merged = [(jax.nn.sigmoid(gt[:, MIX_W:MIX_W + d_model]) * a
               + jax.nn.sigmoid(gt[:, MIX_W + d_model:]) * b).astype(BF16)
              for gt, a, b in zip(gates, br_a, br_b)]
    res = [_dot(m, wo_ref[...]) for m in merged]
    h1 = jnp.concatenate([_layer_norm(alpha * x + r, l1g_ref[...], l1b_ref[...])
                          for x, r in zip(xs, res)], axis=0)

    hb = h1.astype(BF16)
    chunks = [(c, min(c + FFN_CHUNK, d_ff)) for c in range(0, d_ff, FFN_CHUNK)]

    def up(lo, hi):
        return _dot(hb, wu_ref[:, lo:hi]), _dot(hb, wu_ref[:, d_ff + lo:d_ff + hi])

    if seq_rows is None:
        (a_ref,) = scratch

        @pl.when(pl.program_id(1) == 0)
        def _():
            a_ref[6:8, :] = c0_ref[...]

    y = None
    nxt = up(*chunks[0])
    for i, (lo, hi) in enumerate(chunks):
        a, g = nxt
        if i + 1 < len(chunks):
            nxt = up(*chunks[i + 1])
        if seq_rows is None:
            a_ref[8:8 + tm, lo:hi] = a
            a1 = a_ref[7:7 + tm, lo:hi]
            a2 = a_ref[6:6 + tm, lo:hi]
            tail = a_ref[6 + tm:8 + tm, lo:hi]
            a_ref[6:8, lo:hi] = tail
            conv_ref[:, lo:hi] = tail
        else:
            pos = lax.broadcasted_iota(jnp.int32, a.shape, 0) % seq_rows
            a1 = jnp.where(pos >= 1, pltpu.roll(a, 1, 0), 0.0) + c1_ref[:, lo:hi]
            a2 = jnp.where(pos >= 2, pltpu.roll(a, 2, 0), 0.0) + c0_ref[:, lo:hi]
            conv_ref[:, lo:hi] = a
        c = (cb_ref[:, lo:hi] + a2 * cw_ref[0:1, lo:hi] + a1 * cw_ref[1:2, lo:hi]
             + a * cw_ref[2:3, lo:hi])
        act = 0.5 * c * (1.0 + lax.erf(c * (2.0 ** -0.5))) * g
        yi = _dot(act.astype(BF16), wd_ref[lo:hi, :])
        y = yi if y is None else y + yi
    out_ref[...] = _layer_norm(alpha * h1 + y, l2g_ref[...], l2b_ref[...])


def _finish(x, oa, ob, c0, c1, weights, alpha, n_seq, seq_tiles, tm, seq_rows):
    n, d_model = x.shape
    d_ff = weights[-3].shape[0]
    row = lambda s, t: (s * seq_tiles + t, 0)
    const = lambda s, t: (0, 0)
    once = pl.Buffered(1)

    def wspec(w):
        return pl.BlockSpec(w.shape, const, pipeline_mode=once)

    if seq_rows is None:
        c_specs = [pl.BlockSpec((None, CONV_W - 1, d_ff), lambda s, t: (s, 0, 0))] * 2
        conv_shape = jax.ShapeDtypeStruct((n_seq, CONV_W - 1, d_ff), F32)
        conv_spec = pl.BlockSpec((None, CONV_W - 1, d_ff), lambda s, t: (s, 0, 0))
        scratch = [pltpu.VMEM((tm + 8, d_ff), F32)]
    else:
        c_specs = [pl.BlockSpec((tm, d_ff), row)] * 2
        conv_shape = jax.ShapeDtypeStruct((n, d_ff), F32)
        conv_spec = pl.BlockSpec((tm, d_ff), row)
        scratch = []
    return pl.pallas_call(
        functools.partial(_finish_kernel, alpha, d_ff, seq_rows),
        grid=(n_seq, seq_tiles),
        in_specs=[pl.BlockSpec((tm, d_model), row),
                  pl.BlockSpec((tm, MIX_W), row),
                  pl.BlockSpec((tm, MIX_W), row)] + c_specs + [wspec(w) for w in weights],
        out_specs=[pl.BlockSpec((tm, d_model), row), conv_spec],
        out_shape=[jax.ShapeDtypeStruct((n, d_model), F32), conv_shape],
        scratch_shapes=scratch,
        compiler_params=pltpu.CompilerParams(
            dimension_semantics=("parallel", "arbitrary"), vmem_limit_bytes=VMEM_LIMIT),
        name="finish_layer",
    )(x, oa, ob, c0, c1, *weights)


MIXER_TILE = 512
HGRN_STEP = 16
HGRN_UNROLL = 8
SAMPLE_SEQS_PER_STEP = 8
FINISH_TILE = 512
FINISH_ROW_GROUPS = 2
FFN_CHUNK = 512
PAGES_PER_STEP = 16
PAGE_WAVE_STARTS = (0, 4)
DECODE_GROUPS = 8
NEW_KEY_PAD = 16


def _rope_tables(pos):
    half = HEAD_DIM_A // 2
    inv = ROPE_THETA ** (-np.arange(half, dtype=np.float64) * 2.0 / HEAD_DIM_A)
    ang = np.asarray(pos, np.float64)[:, None] * inv[None, :]
    cos = np.cos(ang)
    sin = np.sin(ang)
    reps = LANES // HEAD_DIM_A
    return (jnp.asarray(np.tile(np.concatenate([cos, cos], axis=1), (1, reps)), F32),
            jnp.asarray(np.tile(np.concatenate([-sin, sin], axis=1), (1, reps)), F32))


def kernel(x_prompt, x_sample, cache_k, cache_v, state_hgrn, state_conv, page_table, w_in, lambda_q1, lambda_k1, lambda_q2, lambda_k2, subln_g, lb_logits, hgrn_norm_g, w_branch_a, w_branch_b, w_out, ln1_g, ln1_b, w_up, conv_w, conv_b, w_down, ln2_g, ln2_b):
    bp, tp, d_model = x_prompt.shape
    bs, ts, _ = x_sample.shape
    depth = w_in.shape[0]
    n_phys, _, page, _, _, _ = cache_k.shape
    past_len = page_table.shape[1] * page
    d_ff = w_down.shape[1]
    alpha = (2 * depth) ** 0.25
    n_mix = 6 * MIX_W
    ts_pad = 8
    lb_logits = lb_logits.astype(F32)

    cos_p, sin_p = _rope_tables(np.arange(tp))
    cos_s, sin_s = _rope_tables(past_len + np.arange(bs * ts) % ts)
    cache_kt = jnp.transpose(cache_k, (0, 1, 3, 4, 5, 2)).reshape(n_phys, depth, MIX_W, page)
    cache_vr = cache_v.reshape(n_phys, depth, page * N_HEADS_A, HEAD_W)

    xp = x_prompt.reshape(bp * tp, d_model)
    xs = x_sample.reshape(bs * ts, d_model)
    outs = {name: [] for name in ("kp", "vp", "sp", "cp", "ks", "vs", "ss", "cs")}
    for l in range(depth):
        lam_init = 0.8 - 0.6 * math.exp(-0.3 * l)
        w_mix = w_in[l, :, :n_mix].astype(BF16)
        lam_vecs = jnp.stack([lambda_q1[l], lambda_k1[l], lambda_q2[l], lambda_k2[l]]).astype(F32)
        g_row = subln_g[l].reshape(1, HEAD_W).astype(F32)
        weights = (
            w_in[l, :, n_mix:].astype(BF16),
            hgrn_norm_g[l].reshape(1, KEY_DIM_B).astype(F32),
            w_branch_a[l].astype(BF16), w_branch_b[l].astype(BF16), w_out[l].astype(BF16),
            ln1_g[l].reshape(1, d_model), ln1_b[l].reshape(1, d_model),
            w_up[l].astype(BF16), conv_w[l], conv_b[l].reshape(1, d_ff),
            w_down[l].astype(BF16), ln2_g[l].reshape(1, d_model), ln2_b[l].reshape(1, d_model))

        k_p, v_p, qt3, k_bf, vt3, qb, kk, lf, ib = _mixer_proj(
            xp, w_mix, cos_p, sin_p, lb_logits, l, True, MIXER_TILE)
        k_s, v_s, q_s, qb_s, kk_s, lf_s, ib_s = _mixer_proj(
            xs, w_mix, cos_s, sin_s, lb_logits, l, False, bs * ts)
        pad_t = lambda a, rows: jnp.pad(a.reshape(bs, ts, MIX_W), ((0, 0), (0, rows - ts), (0, 0)))
        q_rep = jnp.tile(q_s.reshape(bs, ts, MIX_W), (1, 2 * N_HEADS_A, 1))

        oa = _prompt_attention(qt3, k_bf, vt3, lam_vecs, g_row.reshape(HEAD_W, 1), lam_init,
                               bp, tp, MIXER_TILE)
        s0 = jnp.zeros((bp, N_HEADS_B, KEY_DIM_B, KEY_DIM_B), F32)
        ob, s_p, oa_s = _hgrn_scan_with_decode(
            qb, kk, lf, ib, s0, bp, tp, HGRN_STEP,
            page_table, q_rep, pad_t(k_s, NEW_KEY_PAD), pad_t(v_s, NEW_KEY_PAD),
            cache_kt, cache_vr, lam_vecs, g_row, lam_init, l, ts, PAGES_PER_STEP)
        conv0 = jnp.zeros((bp, CONV_W - 1, d_ff), F32)
        xp_new, conv_p = _finish(xp, oa, ob, conv0, conv0, weights, alpha,
                                 bp, tp // FINISH_TILE, FINISH_TILE, None)
        outs["kp"].append(jnp.transpose(k_p.reshape(bp, N_HEADS_A, 2, HEAD_DIM_A, tp), (0, 4, 1, 2, 3)))
        outs["vp"].append(v_p.reshape(bp, tp, N_HEADS_A, HEAD_W))
        outs["sp"].append(s_p)
        outs["cp"].append(conv_p)

        flat = lambda a: pad_t(a, ts_pad).reshape(bs * ts_pad, MIX_W)
        ob, s_s = _hgrn_scan(flat(qb_s), flat(kk_s), flat(lf_s), flat(ib_s), state_hgrn[:, l],
                             bs, ts_pad, ts_pad, ts_pad, math.gcd(bs, SAMPLE_SEQS_PER_STEP))
        ob = ob.reshape(bs, ts_pad, MIX_W)[:, :ts].reshape(bs * ts, MIX_W)
        st = state_conv[:, l]
        c0 = jnp.concatenate([st, jnp.zeros((bs, ts - 2, d_ff), F32)], axis=1)
        c1 = jnp.concatenate([st[:, 1:], jnp.zeros((bs, ts - 1, d_ff), F32)], axis=1)
        xs_new, a_s = _finish(xs, oa_s.reshape(bs * ts, MIX_W), ob,
                              c0.reshape(bs * ts, d_ff), c1.reshape(bs * ts, d_ff),
                              weights, alpha, 1, 1, bs * ts, ts)
        conv_s = jnp.concatenate([st, a_s.reshape(bs, ts, d_ff)], axis=1)[:, -(CONV_W - 1):]
        outs["ks"].append(k_s.reshape(bs, ts, N_HEADS_A, 2, HEAD_DIM_A))
        outs["vs"].append(v_s.reshape(bs, ts, N_HEADS_A, HEAD_W))
        outs["ss"].append(s_s)
        outs["cs"].append(conv_s)
        xp, xs = xp_new, xs_new

    stack = lambda name: jnp.stack(outs[name], axis=1)
    return (xp.reshape(bp, tp, d_model), xs.reshape(bs, ts, d_model),
            stack("kp"), stack("vp"), stack("sp"), stack("cp"),
            stack("ks"), stack("vs"), stack("ss"), stack("cs"))
```
